```python
import jax, jax.numpy as jnp
from jax import lax
import numpy as np

D_MODEL = 1024
BATCH = 8
SEQ = 4096
DEPTH = 2

N_MIXERS = 2
EPS = 1e-6

N_MEM = 256
MEM_HEADS = 4
MEM_HEAD_DIM = 64
MEM_WIDTH = MEM_HEADS * MEM_HEAD_DIM
MIX_WIDTH = D_MODEL - MEM_WIDTH

ML_HEADS = 4
ML_V_DIM = MIX_WIDTH // ML_HEADS
ML_QK_DIM = ML_V_DIM // 2
ML_CONV = 4
ML_CHUNK = 128
ML_IN_WIDTH = 2 * ML_HEADS * ML_QK_DIM + 2 * MIX_WIDTH + 2 * ML_HEADS + MEM_WIDTH

MLA_HEADS = 12
MLA_NOPE = 64
MLA_ROPE = 32
MLA_V = MIX_WIDTH // MLA_HEADS
MLA_Q_RANK = 384
MLA_KV_RANK = 256
MLA_IN_WIDTH = MLA_Q_RANK + MLA_KV_RANK + MLA_ROPE + MEM_WIDTH
ROPE_THETA = 10000.0
Q_BLOCK = 128

D_FF = 4 * D_MODEL

kernel_name = 'hybrid_mlstm_mla_memory_trunk'

F32 = jnp.float32


def rms_norm(x, g):
    xf = x.astype(F32)
    y = xf * lax.rsqrt(jnp.mean(xf * xf, axis=-1, keepdims=True) + EPS)
    return (y * g.astype(F32)).astype(x.dtype)


def squared_relu_mlp(h, w1, w2):
    return jnp.square(jax.nn.relu(h @ w1)) @ w2


def memory_cross_attention(q_mem, mem_k, mem_v):
    B, S, _ = q_mem.shape
    q = q_mem.reshape(B, S, MEM_HEADS, MEM_HEAD_DIM)
    s = jnp.einsum('bqhd,bmhd->bhqm', q, mem_k).astype(F32) * (MEM_HEAD_DIM ** -0.5)
    p = jax.nn.softmax(s, axis=-1).astype(mem_v.dtype)
    o = jnp.einsum('bhqm,bmhd->bqhd', p, mem_v)
    return o.reshape(B, S, MEM_WIDTH)


def causal_short_conv(x, w):
    K = w.shape[0]
    S = x.shape[1]
    xp = jnp.pad(x, ((0, 0), (K - 1, 0), (0, 0)))
    return sum(xp[:, j:j + S] * w[j] for j in range(K))


def mlstm_chunkwise(q, k, v, log_i, log_f):
    B, S, H, dk = q.shape
    dv = v.shape[-1]
    L = ML_CHUNK
    nc = S // L

    def chunks(t):
        t = t.astype(F32).reshape((B, nc, L) + t.shape[2:])
        return jnp.moveaxis(t, 1, 0).swapaxes(2, 3)

    causal = jnp.tril(jnp.ones((L, L), dtype=bool))

    def step(carry, inp):
        C, n, m = carry
        qc, kc, vc, ic, fc = inp
        b = jnp.cumsum(fc, axis=-1)
        log_w = jnp.where(causal, b[..., :, None] - b[..., None, :] + ic[..., None, :], -jnp.inf)
        log_inter = b + m[..., None]
        m_t = jnp.maximum(log_inter, jnp.max(log_w, axis=-1))
        w = jnp.exp(log_w - m_t[..., None])
        a_inter = jnp.exp(log_inter - m_t)
        s = jnp.einsum('bhjd,bhsd->bhjs', qc, kc) * w
        num = jnp.einsum('bhjs,bhsv->bhjv', s, vc) + a_inter[..., None] * jnp.einsum('bhjd,bhdv->bhjv', qc, C)
        den = jnp.sum(s, axis=-1) + a_inter * jnp.einsum('bhjd,bhd->bhj', qc, n)
        h = num / jnp.maximum(jnp.abs(den), jnp.exp(-m_t))[..., None]
        b_end = b[..., -1]
        log_u = b_end[..., None] - b + ic
        m_new = jnp.maximum(b_end + m, jnp.max(log_u, axis=-1))
        u = jnp.exp(log_u - m_new[..., None])
        decay = jnp.exp(b_end + m - m_new)
        uk = kc * u[..., None]
        C = decay[..., None, None] * C + jnp.einsum('bhsd,bhsv->bhdv', uk, vc)
        n = decay[..., None] * n + jnp.sum(uk, axis=2)
        return (C, n, m_new), h

    init = (jnp.zeros((B, H, dk, dv), F32), jnp.zeros((B, H, dk), F32), jnp.zeros((B, H), F32))
    _, hs = lax.scan(step, init, (chunks(q), chunks(k), chunks(v), chunks(log_i), chunks(log_f)))
    return jnp.moveaxis(hs.swapaxes(2, 3), 0, 1).reshape(B, S, H, dv)


def mlstm_mixer(h, mem_k, mem_v, w_in, b_igate, b_fgate, w_conv, w_hnorm, w_out):
    B, S, _ = h.shape
    qk_w = 2 * ML_HEADS * ML_QK_DIM
    cuts = [qk_w, qk_w + MIX_WIDTH, qk_w + 2 * MIX_WIDTH,
            qk_w + 2 * MIX_WIDTH + ML_HEADS, qk_w + 2 * MIX_WIDTH + 2 * ML_HEADS]
    qk, v, o_pre, i_pre, f_pre, q_mem = jnp.split(h @ w_in, cuts, axis=-1)
    qk = jax.nn.silu(causal_short_conv(qk, w_conv))
    q, k = jnp.split(qk, 2, axis=-1)
    q = q.reshape(B, S, ML_HEADS, ML_QK_DIM)
    k = k.reshape(B, S, ML_HEADS, ML_QK_DIM) * (ML_QK_DIM ** -0.5)
    v = v.reshape(B, S, ML_HEADS, ML_V_DIM)
    log_i = (i_pre + b_igate).astype(F32)
    log_f = jax.nn.log_sigmoid((f_pre + b_fgate).astype(F32))
    h_cell = mlstm_chunkwise(q, k, v, log_i, log_f).astype(h.dtype)
    h_cell = rms_norm(h_cell, w_hnorm)
    y_ml = h_cell.reshape(B, S, MIX_WIDTH) * jax.nn.sigmoid(o_pre)
    y_mem = memory_cross_attention(q_mem, mem_k, mem_v)
    return jnp.concatenate([y_ml, y_mem], axis=-1) @ w_out


def rope_cos_sin(positions):
    inv = ROPE_THETA ** (-jnp.arange(0, MLA_ROPE, 2, dtype=F32) / MLA_ROPE)
    ang = positions.astype(F32)[..., None] * inv
    return jnp.cos(ang), jnp.sin(ang)


def apply_rope(x, cos, sin):
    xf = x.astype(F32)
    x1, x2 = jnp.split(xf, 2, axis=-1)
    return jnp.concatenate([x1 * cos - x2 * sin, x2 * cos + x1 * sin], axis=-1).astype(x.dtype)


def mla_causal_attention(q_nope, q_rope, k_nope, k_rope, v):
    B, S, H, _ = q_nope.shape
    nb = S // Q_BLOCK
    scale = (MLA_NOPE + MLA_ROPE) ** -0.5
    k_pos = jnp.arange(S)

    def block(i):
        start = i * Q_BLOCK
        qn = lax.dynamic_slice_in_dim(q_nope, start, Q_BLOCK, axis=1)
        qr = lax.dynamic_slice_in_dim(q_rope, start, Q_BLOCK, axis=1)
        s = (jnp.einsum('bqhd,bkhd->bhqk', qn, k_nope)
             + jnp.einsum('bqhd,bkd->bhqk', qr, k_rope)).astype(F32) * scale
        q_pos = start + jnp.arange(Q_BLOCK)
        s = jnp.where(k_pos[None, :] <= q_pos[:, None], s, -jnp.inf)
        p = jax.nn.softmax(s, axis=-1).astype(v.dtype)
        return jnp.einsum('bhqk,bkhd->bqhd', p, v)

    out = lax.map(block, jnp.arange(nb))
    return jnp.moveaxis(out, 0, 1).reshape(B, S, H * MLA_V)


def mla_mixer(h, cos, sin, mem_k, mem_v, w_in, w_qnorm, w_uq, w_kvnorm, w_ukv, w_out):
    B, S, _ = h.shape
    cuts = [MLA_Q_RANK, MLA_Q_RANK + MLA_KV_RANK, MLA_Q_RANK + MLA_KV_RANK + MLA_ROPE]
    c_q, c_kv, k_rope, q_mem = jnp.split(h @ w_in, cuts, axis=-1)
    q = (rms_norm(c_q, w_qnorm) @ w_uq).reshape(B, S, MLA_HEADS, MLA_NOPE + MLA_ROPE)
    q_nope, q_rope = q[..., :MLA_NOPE], q[..., MLA_NOPE:]
    kv = (rms_norm(c_kv, w_kvnorm) @ w_ukv).reshape(B, S, MLA_HEADS, MLA_NOPE + MLA_V)
    k_nope, v = kv[..., :MLA_NOPE], kv[..., MLA_NOPE:]
    q_rope = apply_rope(q_rope, cos[:, :, None, :], sin[:, :, None, :])
    k_rope = apply_rope(k_rope, cos, sin)
    y_mla = mla_causal_attention(q_nope, q_rope, k_nope, k_rope, v)
    y_mem = memory_cross_attention(q_mem, mem_k, mem_v)
    return jnp.concatenate([y_mla, y_mem], axis=-1) @ w_out


def setup_inputs(seed: int = 0) -> dict:
    key = jax.random.key(seed)
    ks = jax.random.split(key, 32)

    def dense(k, fi, fo):
        return jax.random.normal(k, (fi, fo), F32) * (fi ** -0.5)

    def gain(k, *shape):
        return 1.0 + 0.02 * jax.random.normal(k, shape, F32)

    x = jax.random.normal(ks[0], (BATCH, SEQ, D_MODEL), F32)
    mem = jax.random.normal(ks[1], (BATCH, N_MEM, D_MODEL), F32)
    positions = (jax.random.randint(ks[2], (BATCH, 1), 0, 1024, dtype=jnp.int32)
                 + jnp.arange(SEQ, dtype=jnp.int32)[None, :])
    return {
        'x': x,
        'mem': mem,
        'positions': positions,
        'mem_norm': gain(ks[3], D_MODEL),
        'w_mem_kv': dense(ks[4], D_MODEL, 2 * MEM_WIDTH),
        'norm_mix0': gain(ks[5], D_MODEL),
        'w_in0': dense(ks[6], D_MODEL, ML_IN_WIDTH),
        'b_igate0': 0.1 * jax.random.normal(ks[7], (ML_HEADS,), F32),
        'b_fgate0': jnp.linspace(3.0, 6.0, ML_HEADS, dtype=F32) + 0.1 * jax.random.normal(ks[8], (ML_HEADS,), F32),
        'w_conv0': jax.random.normal(ks[9], (ML_CONV, 2 * ML_HEADS * ML_QK_DIM), F32) * (ML_CONV ** -0.5),
        'w_hnorm0': gain(ks[10], ML_HEADS, ML_V_DIM),
        'w_out0': dense(ks[11], D_MODEL, D_MODEL),
        'norm_ffn0': gain(ks[12], D_MODEL),
        'w_ff1_0': dense(ks[13], D_MODEL, D_FF),
        'w_ff2_0': dense(ks[14], D_FF, D_MODEL),
        'norm_mix1': gain(ks[15], D_MODEL),
        'w_in1': dense(ks[16], D_MODEL, MLA_IN_WIDTH),
        'w_qnorm1': gain(ks[17], MLA_Q_RANK),
        'w_uq1': dense(ks[18], MLA_Q_RANK, MLA_HEADS * (MLA_NOPE + MLA_ROPE)),
        'w_kvnorm1': gain(ks[19], MLA_KV_RANK),
        'w_ukv1': dense(ks[20], MLA_KV_RANK, MLA_HEADS * (MLA_NOPE + MLA_V)),
        'w_out1': dense(ks[21], D_MODEL, D_MODEL),
        'norm_ffn1': gain(ks[22], D_MODEL),
        'w_ff1_1': dense(ks[23], D_MODEL, D_FF),
        'w_ff2_1': dense(ks[24], D_FF, D_MODEL),
        'final_norm': gain(ks[25], D_MODEL),
    }


def reference(x, mem, positions, mem_norm, w_mem_kv,
              norm_mix0, w_in0, b_igate0, b_fgate0, w_conv0, w_hnorm0, w_out0,
              norm_ffn0, w_ff1_0, w_ff2_0,
              norm_mix1, w_in1, w_qnorm1, w_uq1, w_kvnorm1, w_ukv1, w_out1,
              norm_ffn1, w_ff1_1, w_ff2_1, final_norm):
    Bm = mem.shape[0]
    mem_k, mem_v = jnp.split(rms_norm(mem, mem_norm) @ w_mem_kv, 2, axis=-1)
    mem_k = mem_k.reshape(Bm, N_MEM, MEM_HEADS, MEM_HEAD_DIM)
    mem_v = mem_v.reshape(Bm, N_MEM, MEM_HEADS, MEM_HEAD_DIM)
    cos, sin = rope_cos_sin(positions)
    ffn_params = [(norm_ffn0, w_ff1_0, w_ff2_0), (norm_ffn1, w_ff1_1, w_ff2_1)]
    for layer in range(DEPTH):
        if layer % N_MIXERS == 0:
            x = x + mlstm_mixer(rms_norm(x, norm_mix0), mem_k, mem_v,
                                w_in0, b_igate0, b_fgate0, w_conv0, w_hnorm0, w_out0)
        else:
            x = x + mla_mixer(rms_norm(x, norm_mix1), cos, sin, mem_k, mem_v,
                              w_in1, w_qnorm1, w_uq1, w_kvnorm1, w_ukv1, w_out1)
        g, w1, w2 = ffn_params[layer]
        x = x + squared_relu_mlp(rms_norm(x, g), w1, w2)
    return rms_norm(x, final_norm)
```

```python
import functools
import math

import numpy as np
import jax
import jax.numpy as jnp
from jax import lax
from jax.experimental import pallas as pl
from jax.experimental.pallas import tpu as pltpu

F32 = jnp.float32
BF16 = jnp.bfloat16
EPS = 1e-6

D_MODEL = 1024
N_MEM = 256
MEM_HEADS = 4
MEM_HEAD_DIM = 64
MEM_WIDTH = MEM_HEADS * MEM_HEAD_DIM
MIX_WIDTH = D_MODEL - MEM_WIDTH

ML_HEADS = 4
ML_V_DIM = MIX_WIDTH // ML_HEADS
ML_QK_DIM = ML_V_DIM // 2
ML_CONV = 4
ML_CHUNK = 128
ML_QK_PAD = 128
ML_V_PAD = 256
ML_ONES_COL = ML_V_DIM

MLA_HEADS = 12
MLA_NOPE = 64
MLA_ROPE = 32
MLA_V = 64
MLA_Q_RANK = 384
MLA_KV_RANK = 256
MLA_QK_PAD = 128
ROPE_THETA = 10000.0
D_FF = 4 * D_MODEL

LANES = 128
VMEM_LIMIT = 56 * 1024 * 1024

TM_PROJ = 512
FF_CHUNK = 1024
TQ = 512


def _params(*sem):
    return pltpu.CompilerParams(dimension_semantics=sem, vmem_limit_bytes=VMEM_LIMIT)


def _rms(x, g):
    return x * lax.rsqrt(jnp.mean(x * x, axis=-1, keepdims=True) + EPS) * g


def _const_spec(shape):
    nd = len(shape)
    return pl.BlockSpec(shape, lambda *_: (0,) * nd)


def _mem_kv_kernel(mem_ref, g_ref, wkt_ref, wv_ref, kt_ref, v_ref):
    xn = _rms(mem_ref[0], g_ref[...]).astype(BF16)
    kt = lax.dot_general(wkt_ref[...], xn, (((1,), (1,)), ((), ())),
                         preferred_element_type=F32)
    v = jnp.dot(xn, wv_ref[...], preferred_element_type=F32)
    kt = kt * (MEM_HEAD_DIM ** -0.5)
    row_head = lax.broadcasted_iota(jnp.int32, kt.shape, 0) // MEM_HEAD_DIM
    col_head = lax.broadcasted_iota(jnp.int32, v.shape, 1) // MEM_HEAD_DIM
    for h in range(MEM_HEADS):
        kt_ref[0, h] = jnp.where(row_head == h, kt, 0.0).astype(BF16)
        v_ref[0, h] = jnp.where(col_head == h, v, 0.0).astype(BF16)


def _mem_kv(mem, g, wkt, wv):
    B = mem.shape[0]
    out = jax.ShapeDtypeStruct((B, MEM_HEADS, MEM_WIDTH, N_MEM), BF16)
    return pl.pallas_call(
        _mem_kv_kernel,
        grid=(B,),
        in_specs=[pl.BlockSpec((1, N_MEM, D_MODEL), lambda b: (b, 0, 0)),
                  _const_spec(g.shape), _const_spec(wkt.shape), _const_spec(wv.shape)],
        out_specs=[pl.BlockSpec((1, MEM_HEADS, MEM_WIDTH, N_MEM), lambda b: (b, 0, 0, 0)),
                   pl.BlockSpec((1, MEM_HEADS, N_MEM, MEM_WIDTH), lambda b: (b, 0, 0, 0))],
        out_shape=[out, out],
        compiler_params=_params("arbitrary"),
        name="mem_kv",
    )(mem, g, wkt, wv)


def _inproj0_kernel(x_ref, g_ref, wqk_ref, wv_ref, wo_ref, wqm_ref, wg_ref,
                    qk_ref, v_ref, o_ref, qm_ref, gate_ref):
    xn = _rms(x_ref[0], g_ref[...]).astype(BF16)
    qk_ref[0] = jnp.dot(xn, wqk_ref[...], preferred_element_type=F32).astype(BF16)
    v = jnp.dot(xn, wv_ref[...], preferred_element_type=F32)
    lane = lax.broadcasted_iota(jnp.int32, v.shape, 1) % ML_V_PAD
    v_ref[0] = jnp.where(lane == ML_ONES_COL, 1.0, v).astype(BF16)
    o_ref[0] = jnp.dot(xn, wo_ref[...], preferred_element_type=F32).astype(BF16)
    qm_ref[0] = jnp.dot(xn, wqm_ref[...], preferred_element_type=F32).astype(BF16)
    gate_ref[0] = jnp.dot(xn, wg_ref[...], preferred_element_type=F32)


def _inproj0(x, g, wqk, wv, wo, wqm, wg):
    B, S, D = x.shape
    tm = min(TM_PROJ, S)
    tok = lambda w: pl.BlockSpec((1, tm, w), lambda b, i: (b, i, 0))
    widths = (wqk.shape[1], wv.shape[1], wo.shape[1], wqm.shape[1], wg.shape[1])
    dtypes = (BF16, BF16, BF16, BF16, F32)
    return pl.pallas_call(
        _inproj0_kernel,
        grid=(B, S // tm),
        in_specs=[tok(D), _const_spec(g.shape)] + [_const_spec(w.shape) for w in (wqk, wv, wo, wqm, wg)],
        out_specs=[tok(w) for w in widths],
        out_shape=[jax.ShapeDtypeStruct((B, S, w), dt) for w, dt in zip(widths, dtypes)],
        compiler_params=_params("arbitrary", "arbitrary"),
        name="inproj0",
    )(x, g, wqk, wv, wo, wqm, wg)


def _split3(x):
    hi = x.astype(BF16)
    r = x - hi.astype(F32)
    mid = r.astype(BF16)
    lo = (r - mid.astype(F32)).astype(BF16)
    return hi, mid, lo


def _mlstm_kernel(qk_ref, v_ref, o_ref, gate_ref, wconv_ref, bias_ref, hnorm_ref,
                  y_ref, c_ref, m_ref, tail_ref):
    L = ML_CHUNK
    HQ = ML_HEADS * ML_QK_PAD

    @pl.when(pl.program_id(1) == 0)
    def _():
        c_ref[...] = jnp.zeros_like(c_ref)
        m_ref[...] = jnp.zeros_like(m_ref)
        tail_ref[...] = jnp.zeros_like(tail_ref)

    x = qk_ref[0].astype(F32)
    tail = tail_ref[...]
    wc = wconv_ref[...]
    row8 = lax.broadcasted_iota(jnp.int32, tail.shape, 0)
    conv = x * wc[ML_CONV - 1:ML_CONV]
    for s in range(1, ML_CONV):
        xs = pltpu.roll(x, s, axis=0)
        head = jnp.where(row8 < s, pltpu.roll(tail, s, axis=0), xs[0:8])
        xs = jnp.concatenate([head, xs[8:]], axis=0)
        conv = conv + xs * wc[ML_CONV - 1 - s:ML_CONV - s]
    tail_ref[...] = x[L - 8:L]
    act = conv * jax.nn.sigmoid(conv)
    q = act[:, :HQ].astype(BF16)
    k32 = act[:, HQ:] * (ML_QK_DIM ** -0.5)
    k = k32.astype(BF16)
    v = v_ref[0]

    gates = gate_ref[0] + bias_ref[...]
    lane = lax.broadcasted_iota(jnp.int32, gates.shape, 1)
    log_f = jnp.minimum(gates, 0.0) - jnp.log1p(jnp.exp(-jnp.abs(gates)))
    log_f = jnp.where((lane >= ML_HEADS) & (lane < 2 * ML_HEADS), log_f, 0.0)
    r_i = lax.broadcasted_iota(jnp.int32, (L, L), 0)
    c_i = lax.broadcasted_iota(jnp.int32, (L, L), 1)
    causal = c_i <= r_i
    tril = jnp.where(causal, 1.0, 0.0).astype(BF16)
    b_cols = sum(jnp.dot(tril, part, preferred_element_type=F32) for part in _split3(log_f))
    b_rows = b_cols.T
    g_rows = gates.T

    hn_all = hnorm_ref[...]
    vlane = lax.broadcasted_iota(jnp.int32, (L, ML_V_PAD), 1)
    for h in range(ML_HEADS):
        b_col = b_cols[:, ML_HEADS + h:ML_HEADS + h + 1]
        b_row = b_rows[ML_HEADS + h:ML_HEADS + h + 1, :]
        i_row = g_rows[h:h + 1, :]
        i_col = gates[:, h:h + 1]
        m_prev = m_ref[h][:, 0:1]
        qh = q[:, h * ML_QK_PAD:(h + 1) * ML_QK_PAD]
        kh = k[:, h * ML_QK_PAD:(h + 1) * ML_QK_PAD]
        vh = v[:, h * ML_V_PAD:(h + 1) * ML_V_PAD]
        c_prev = c_ref[h]

        log_w = jnp.where(causal, b_col - b_row + i_row, -jnp.inf)
        log_inter = b_col + m_prev
        m_t = jnp.maximum(log_inter, jnp.max(log_w, axis=-1, keepdims=True))
        w = jnp.exp(log_w - m_t)
        a_inter = jnp.exp(log_inter - m_t)
        s = lax.dot_general(qh, kh, (((1,), (1,)), ((), ())), preferred_element_type=F32) * w
        num = (jnp.dot(s.astype(BF16), vh, preferred_element_type=F32)
               + a_inter * jnp.dot(qh, c_prev.astype(BF16), preferred_element_type=F32))
        den = num[:, ML_ONES_COL:ML_ONES_COL + 1]
        inv = 1.0 / jnp.maximum(jnp.abs(den), jnp.exp(-m_t))
        hc = jnp.where(vlane < ML_V_DIM, num * inv, 0.0)
        ms = jnp.sum(hc * hc, axis=-1, keepdims=True) * (1.0 / ML_V_DIM)
        hn = hc * lax.rsqrt(ms + EPS) * hn_all[:, h * ML_V_PAD:(h + 1) * ML_V_PAD]
        og = o_ref[0, :, h * ML_V_PAD:(h + 1) * ML_V_PAD].astype(F32)
        y_ref[0, :, h * ML_V_PAD:(h + 1) * ML_V_PAD] = (hn * jax.nn.sigmoid(og)).astype(BF16)

        b_end = b_row[:, L - 1:L]
        m_new = jnp.maximum(b_end + m_prev, jnp.max(b_end - b_row + i_row, axis=-1, keepdims=True))
        u_col = jnp.exp(b_end - b_col + i_col - m_new)
        decay = jnp.exp(b_end + m_prev - m_new)
        uk = (k32[:, h * ML_QK_PAD:(h + 1) * ML_QK_PAD] * u_col).astype(BF16)
        c_ref[h] = decay * c_prev + lax.dot_general(uk, vh, (((0,), (0,)), ((), ())),
                                                    preferred_element_type=F32)
        m_ref[h] = jnp.broadcast_to(m_new, (1, LANES))


def _mlstm(qk, v, o, gates, wconv, bias, hnorm):
    B, S, _ = qk.shape
    L = ML_CHUNK
    tok = lambda w: pl.BlockSpec((1, L, w), lambda b, c: (b, c, 0))
    return pl.pallas_call(
        _mlstm_kernel,
        grid=(B, S // L),
        in_specs=[tok(qk.shape[2]), tok(v.shape[2]), tok(o.shape[2]), tok(gates.shape[2]),
                  _const_spec(wconv.shape), _const_spec(bias.shape), _const_spec(hnorm.shape)],
        out_specs=tok(v.shape[2]),
        out_shape=jax.ShapeDtypeStruct(v.shape, BF16),
        scratch_shapes=[pltpu.VMEM((ML_HEADS, ML_QK_PAD, ML_V_PAD), F32),
                        pltpu.VMEM((ML_HEADS, 1, LANES), F32),
                        pltpu.VMEM((8, qk.shape[2]), F32)],
        compiler_params=_params("arbitrary", "arbitrary"),
        name="mlstm",
    )(qk, v, o, gates, wconv, bias, hnorm)


def _outproj_kernel(x_ref, y_ref, qm_ref, kt_ref, vm_ref, wa_ref, wb_ref, out_ref):
    qm = qm_ref[0]
    ymem = None
    for h in range(MEM_HEADS):
        s = jnp.dot(qm, kt_ref[0, h], preferred_element_type=F32)
        e = jnp.exp(s - jnp.max(s, axis=-1, keepdims=True))
        p = (e * (1.0 / jnp.sum(e, axis=-1, keepdims=True))).astype(BF16)
        oh = jnp.dot(p, vm_ref[0, h], preferred_element_type=F32)
        ymem = oh if ymem is None else ymem + oh
    out = x_ref[0] + jnp.dot(y_ref[0], wa_ref[...], preferred_element_type=F32)
    out_ref[0] = out + jnp.dot(ymem.astype(BF16), wb_ref[...], preferred_element_type=F32)


def _outproj(x, y, qm, kt, vm, wa, wb):
    B, S, D = x.shape
    tm = min(TM_PROJ, S)
    tok = lambda w: pl.BlockSpec((1, tm, w), lambda b, i: (b, i, 0))
    per_b = pl.BlockSpec((1,) + kt.shape[1:], lambda b, i: (b, 0, 0, 0))
    return pl.pallas_call(
        _outproj_kernel,
        grid=(B, S // tm),
        in_specs=[tok(D), tok(y.shape[2]), tok(qm.shape[2]), per_b, per_b,
                  _const_spec(wa.shape), _const_spec(wb.shape)],
        out_specs=tok(D),
        out_shape=jax.ShapeDtypeStruct(x.shape, F32),
        compiler_params=_params("arbitrary", "arbitrary"),
        name="outproj",
    )(x, y, qm, kt, vm, wa, wb)


def _ffn_kernel(x_ref, g_ref, w1_ref, w2_ref, gf_ref, out_ref, *, final_norm):
    x = x_ref[0]
    hn = _rms(x, g_ref[...]).astype(BF16)
    acc = x
    for c in range(D_FF // FF_CHUNK):
        a = jnp.dot(hn, w1_ref[:, c * FF_CHUNK:(c + 1) * FF_CHUNK], preferred_element_type=F32)
        a = jnp.square(jnp.maximum(a, 0.0)).astype(BF16)
        acc = acc + jnp.dot(a, w2_ref[c * FF_CHUNK:(c + 1) * FF_CHUNK, :], preferred_element_type=F32)
    if final_norm:
        acc = _rms(acc, gf_ref[...])
    out_ref[0] = acc


def _ffn(x, g, w1, w2, gf, final_norm):
    B, S, D = x.shape
    tm = min(TM_PROJ, S)
    tok = pl.BlockSpec((1, tm, D), lambda b, i: (b, i, 0))
    return pl.pallas_call(
        functools.partial(_ffn_kernel, final_norm=final_norm),
        grid=(B, S // tm),
        in_specs=[tok, _const_spec(g.shape), _const_spec(w1.shape), _const_spec(w2.shape),
                  _const_spec(gf.shape)],
        out_specs=tok,
        out_shape=jax.ShapeDtypeStruct(x.shape, F32),
        compiler_params=_params("arbitrary", "arbitrary"),
        name="ffn",
    )(x, g, w1, w2, gf)


def _rope_kernel(pos_ref, inv_ref, cos_ref, sin_ref, nsin_ref):
    ang = pos_ref[...].astype(F32) * inv_ref[...]
    sn = jnp.sin(ang)
    cos_ref[...] = jnp.cos(ang)
    sin_ref[...] = sn
    nsin_ref[...] = -sn


def _rope_tables(positions):
    T = positions.size
    nf = MLA_ROPE // 2
    per_row = LANES // nf
    rows = T // per_row
    pos = jnp.repeat(positions.reshape(rows, per_row), nf, axis=1)
    inv = ROPE_THETA ** (-jnp.arange(0, MLA_ROPE, 2, dtype=F32) / MLA_ROPE)
    inv = jnp.tile(inv, per_row)[None, :]
    tr = min(rows, 1024)
    spec = pl.BlockSpec((tr, LANES), lambda i: (i, 0))
    out = jax.ShapeDtypeStruct((rows, LANES), F32)
    cos, sin, nsin = pl.pallas_call(
        _rope_kernel,
        grid=(rows // tr,),
        in_specs=[spec, _const_spec(inv.shape)],
        out_specs=[spec, spec, spec],
        out_shape=[out, out, out],
        compiler_params=_params("arbitrary"),
        name="rope_tables",
    )(pos, inv)
    return cos.reshape(T, nf), sin.reshape(T, nf), nsin.reshape(T, nf)


def _lane_pattern(parts, T):
    cols = [p if not isinstance(p, tuple) else jnp.full((T, p[1]), p[0], F32) for p in parts]
    out = jnp.concatenate(cols, axis=1)
    assert out.shape == (T, LANES)
    return out


def _rope_rotate(x, cos, sin_from_hi, sin_from_lo):
    half = MLA_ROPE // 2
    return (x * cos + pltpu.roll(x, LANES - half, axis=1) * sin_from_hi
            + pltpu.roll(x, half, axis=1) * sin_from_lo)


def _inproj1_kernel(x_ref, g_ref, win_ref, gq_ref, gkv_ref, wuq_ref, wuk_ref, wuv_ref, place_ref,
                    qcos_ref, qsa_ref, qsb_ref, kcos_ref, ksa_ref, ksb_ref,
                    q_ref, k_ref, v_ref, qm_ref):
    xn = _rms(x_ref[0], g_ref[...]).astype(BF16)
    c = jnp.dot(xn, win_ref[...], preferred_element_type=F32)
    o_kv = MLA_Q_RANK
    o_qm = o_kv + MLA_KV_RANK
    o_kr = o_qm + MEM_WIDTH
    qm_ref[0] = c[:, o_qm:o_kr].astype(BF16)
    cq = _rms(c[:, :o_kv], gq_ref[...]).astype(BF16)
    ckv = _rms(c[:, o_kv:o_qm], gkv_ref[...]).astype(BF16)

    qscale = ((MLA_NOPE + MLA_ROPE) ** -0.5) * math.log2(math.e)
    q = jnp.dot(cq, wuq_ref[...], preferred_element_type=F32)
    qcos, qsa, qsb = qcos_ref[0], qsa_ref[0], qsb_ref[0]
    for h in range(MLA_HEADS):
        sl = slice(h * MLA_QK_PAD, (h + 1) * MLA_QK_PAD)
        q_ref[0, :, sl] = (_rope_rotate(q[:, sl], qcos, qsa, qsb) * qscale).astype(BF16)

    kr = _rope_rotate(c[:, o_kr:], kcos_ref[0], ksa_ref[0], ksb_ref[0]).astype(BF16)
    k = (jnp.dot(ckv, wuk_ref[...], preferred_element_type=F32)
         + jnp.dot(kr, place_ref[...], preferred_element_type=F32))
    k_ref[0] = k.astype(BF16)
    v_ref[0] = jnp.dot(ckv, wuv_ref[...], preferred_element_type=F32).astype(BF16)


def _inproj1(x, g, win, gq, gkv, wuq, wuk, wuv, place, tables):
    B, S, D = x.shape
    tm = min(TM_PROJ, S)
    tok = lambda w: pl.BlockSpec((1, tm, w), lambda b, i: (b, i, 0))
    consts = (g, win, gq, gkv, wuq, wuk, wuv, place)
    widths = (wuq.shape[1], wuk.shape[1], wuv.shape[1], MEM_WIDTH)
    return pl.pallas_call(
        _inproj1_kernel,
        grid=(B, S // tm),
        in_specs=[tok(D)] + [_const_spec(a.shape) for a in consts] + [tok(LANES)] * 6,
        out_specs=[tok(w) for w in widths],
        out_shape=[jax.ShapeDtypeStruct((B, S, w), BF16) for w in widths],
        compiler_params=_params("arbitrary", "arbitrary"),
        name="inproj1",
    )(x, *consts, *tables)


def _flash_kernel(q_ref, k_ref, v_ref, y_ref, acc_ref, m_ref, l_ref):
    i = pl.program_id(2)
    nt = (((1,), (1,)), ((), ()))
    r_i = lax.broadcasted_iota(jnp.int32, (TQ, TQ), 0)
    c_i = lax.broadcasted_iota(jnp.int32, (TQ, TQ), 1)
    diag_mask = c_i <= r_i

    def scores(hh, start):
        qh = q_ref[0, :, hh * MLA_QK_PAD:(hh + 1) * MLA_QK_PAD]
        kh = k_ref[0, pl.ds(start, TQ), hh * MLA_QK_PAD:(hh + 1) * MLA_QK_PAD]
        return lax.dot_general(qh, kh, nt, preferred_element_type=F32)

    d0 = pl.multiple_of(i * TQ, TQ)
    v_d = v_ref[0, pl.ds(d0, TQ), :]
    for hh in range(2):
        s = jnp.where(diag_mask, scores(hh, d0), -jnp.inf)
        m = jnp.max(s, axis=-1, keepdims=True)
        p = jnp.exp2(s - m)
        m_ref[hh] = m
        l_ref[hh] = jnp.sum(p, axis=-1, keepdims=True)
        acc_ref[hh] = jnp.dot(p.astype(BF16), v_d, preferred_element_type=F32)

    def body(j, carry):
        start = pl.multiple_of(j * TQ, TQ)
        v_j = v_ref[0, pl.ds(start, TQ), :]
        for hh in range(2):
            s = scores(hh, start)
            m_prev = m_ref[hh]
            m_new = jnp.maximum(m_prev, jnp.max(s, axis=-1, keepdims=True))
            alpha = jnp.exp2(m_prev - m_new)
            p = jnp.exp2(s - m_new)
            l_ref[hh] = alpha * l_ref[hh] + jnp.sum(p, axis=-1, keepdims=True)
            acc_ref[hh] = alpha * acc_ref[hh] + jnp.dot(p.astype(BF16), v_j, preferred_element_type=F32)
            m_ref[hh] = m_new
        return carry

    lax.fori_loop(0, i, body, 0)

    lane = lax.broadcasted_iota(jnp.int32, (TQ, 2 * MLA_V), 1)
    out = jnp.where(lane < MLA_V, acc_ref[0] * (1.0 / l_ref[0]), acc_ref[1] * (1.0 / l_ref[1]))
    y_ref[0] = out.astype(BF16)


def _flash(q, k, v):
    B, S, _ = q.shape
    assert S % TQ == 0
    pairs = MLA_HEADS // 2
    return pl.pallas_call(
        _flash_kernel,
        grid=(B, pairs, S // TQ),
        in_specs=[pl.BlockSpec((1, TQ, 2 * MLA_QK_PAD), lambda b, p, i: (b, i, p)),
                  pl.BlockSpec((1, S, 2 * MLA_QK_PAD), lambda b, p, i: (b, 0, p)),
                  pl.BlockSpec((1, S, 2 * MLA_V), lambda b, p, i: (b, 0, p))],
        out_specs=pl.BlockSpec((1, TQ, 2 * MLA_V), lambda b, p, i: (b, i, p)),
        out_shape=jax.ShapeDtypeStruct(v.shape, BF16),
        scratch_shapes=[pltpu.VMEM((2, TQ, 2 * MLA_V), F32),
                        pltpu.VMEM((2, TQ, 1), F32),
                        pltpu.VMEM((2, TQ, 1), F32)],
        compiler_params=_params("arbitrary", "arbitrary", "arbitrary"),
        name="mla_flash",
    )(q, k, v)


def _pad_heads(w, heads, d, dp):
    lead = w.shape[:-1]
    w = w.reshape(lead + (heads, d))
    w = jnp.pad(w, [(0, 0)] * len(lead) + [(0, 0), (0, dp - d)])
    return w.reshape(lead + (heads * dp,))


def _pad_head_rows(w, heads, d, dp):
    return _pad_heads(w.T, heads, d, dp).T


def kernel(x, mem, positions, mem_norm, w_mem_kv, norm_mix0, w_in0, b_igate0, b_fgate0, w_conv0, w_hnorm0, w_out0, norm_ffn0, w_ff1_0, w_ff2_0, norm_mix1, w_in1, w_qnorm1, w_uq1, w_kvnorm1, w_ukv1, w_out1, norm_ffn1, w_ff1_1, w_ff2_1, final_norm):
    B, S, D = x.shape
    T = B * S
    row = lambda g: g.reshape(1, -1).astype(F32)

    kt, vm = _mem_kv(mem, row(mem_norm), w_mem_kv[:, :MEM_WIDTH].T.astype(BF16),
                     w_mem_kv[:, MEM_WIDTH:].astype(BF16))

    nq = ML_HEADS * ML_QK_DIM
    o_v, o_o, o_g = 2 * nq, 2 * nq + MIX_WIDTH, 2 * nq + 2 * MIX_WIDTH
    o_qm = o_g + 2 * ML_HEADS
    wqk = jnp.concatenate([_pad_heads(w_in0[:, :nq], ML_HEADS, ML_QK_DIM, ML_QK_PAD),
                           _pad_heads(w_in0[:, nq:o_v], ML_HEADS, ML_QK_DIM, ML_QK_PAD)], axis=1)
    wv0 = _pad_heads(w_in0[:, o_v:o_o], ML_HEADS, ML_V_DIM, ML_V_PAD)
    wo0 = _pad_heads(w_in0[:, o_o:o_g], ML_HEADS, ML_V_DIM, ML_V_PAD)
    wg0 = jnp.pad(w_in0[:, o_g:o_qm], ((0, 0), (0, LANES - 2 * ML_HEADS)))
    wqm0 = w_in0[:, o_qm:]
    wconv = jnp.concatenate([_pad_heads(w_conv0[:, :nq], ML_HEADS, ML_QK_DIM, ML_QK_PAD),
                             _pad_heads(w_conv0[:, nq:], ML_HEADS, ML_QK_DIM, ML_QK_PAD)], axis=1).astype(F32)
    gate_bias = jnp.pad(jnp.concatenate([b_igate0, b_fgate0]), (0, LANES - 2 * ML_HEADS)).reshape(1, LANES).astype(F32)
    hnorm = _pad_heads(w_hnorm0.reshape(1, -1), ML_HEADS, ML_V_DIM, ML_V_PAD).astype(F32)
    wa0 = _pad_head_rows(w_out0[:MIX_WIDTH], ML_HEADS, ML_V_DIM, ML_V_PAD).astype(BF16)
    wb0 = w_out0[MIX_WIDTH:].astype(BF16)

    qk, v0, o0, qm0, gates = _inproj0(x, row(norm_mix0), wqk.astype(BF16), wv0.astype(BF16),
                                      wo0.astype(BF16), wqm0.astype(BF16), wg0.astype(BF16))
    y0 = _mlstm(qk, v0, o0, gates, wconv, gate_bias, hnorm)
    x = _outproj(x, y0, qm0, kt, vm, wa0, wb0)
    x = _ffn(x, row(norm_ffn0), w_ff1_0.astype(BF16), w_ff2_0.astype(BF16), row(final_norm), False)

    o_kv, o_kr = MLA_Q_RANK, MLA_Q_RANK + MLA_KV_RANK
    o_qm1 = o_kr + MLA_ROPE
    win1 = jnp.concatenate([w_in1[:, :o_kr], w_in1[:, o_qm1:],
                            jnp.pad(w_in1[:, o_kr:o_qm1], ((0, 0), (0, LANES - MLA_ROPE)))], axis=1)
    wuq = _pad_heads(w_uq1, MLA_HEADS, MLA_NOPE + MLA_ROPE, MLA_QK_PAD)
    ukv = w_ukv1.reshape(MLA_KV_RANK, MLA_HEADS, MLA_NOPE + MLA_V)
    wuk = _pad_heads(ukv[:, :, :MLA_NOPE].reshape(MLA_KV_RANK, -1), MLA_HEADS, MLA_NOPE, MLA_QK_PAD)
    wuv = ukv[:, :, MLA_NOPE:].reshape(MLA_KV_RANK, -1)
    place = np.zeros((LANES, MLA_HEADS * MLA_QK_PAD), np.float32)
    for h in range(MLA_HEADS):
        place[np.arange(MLA_ROPE), h * MLA_QK_PAD + MLA_NOPE + np.arange(MLA_ROPE)] = 1.0
    place = jnp.asarray(place, BF16)

    cos, sin, nsin = _rope_tables(positions)
    half = MLA_ROPE // 2
    pad_q = LANES - MLA_NOPE - MLA_ROPE
    tables = [
        _lane_pattern([(1.0, MLA_NOPE), cos, cos, (0.0, pad_q)], T),
        _lane_pattern([(0.0, MLA_NOPE), nsin, (0.0, half), (0.0, pad_q)], T),
        _lane_pattern([(0.0, MLA_NOPE + half), sin, (0.0, pad_q)], T),
        _lane_pattern([cos, cos, (0.0, LANES - MLA_ROPE)], T),
        _lane_pattern([nsin, (0.0, half), (0.0, LANES - MLA_ROPE)], T),
        _lane_pattern([(0.0, half), sin, (0.0, LANES - MLA_ROPE)], T),
    ]
    tables = [t.reshape(B, S, LANES) for t in tables]

    q1, k1, v1, qm1 = _inproj1(x, row(norm_mix1), win1.astype(BF16), row(w_qnorm1), row(w_kvnorm1),
                               wuq.astype(BF16), wuk.astype(BF16), wuv.astype(BF16), place, tables)
    y1 = _flash(q1, k1, v1)
    x = _outproj(x, y1, qm1, kt, vm, w_out1[:MIX_WIDTH].astype(BF16), w_out1[MIX_WIDTH:].astype(BF16))
    return _ffn(x, row(norm_ffn1), w_ff1_1.astype(BF16), w_ff2_1.astype(BF16), row(final_norm), True)
```

```python
import functools
import math

import numpy as np
import jax
import jax.numpy as jnp
from jax import lax
from jax.experimental import pallas as pl
from jax.experimental.pallas import tpu as pltpu

F32 = jnp.float32
BF16 = jnp.bfloat16
EPS = 1e-6

D_MODEL = 1024
N_MEM = 256
MEM_HEADS = 4
MEM_HEAD_DIM = 64
MEM_WIDTH = MEM_HEADS * MEM_HEAD_DIM
MIX_WIDTH = D_MODEL - MEM_WIDTH

ML_HEADS = 4
ML_V_DIM = MIX_WIDTH // ML_HEADS
ML_QK_DIM = ML_V_DIM // 2
ML_CONV = 4
ML_CHUNK = 128
ML_QK_PAD = 128
ML_V_PAD = 256
ML_ONES_COL = ML_V_DIM

MLA_HEADS = 12
MLA_NOPE = 64
MLA_ROPE = 32
MLA_V = 64
MLA_Q_RANK = 384
MLA_KV_RANK = 256
MLA_QK_PAD = 128
MLA_VT_ROWS = 80
ROPE_THETA = 10000.0
D_FF = 4 * D_MODEL

LANES = 128
VMEM_LIMIT = 56 * 1024 * 1024

TM_PROJ = 512
FF_CHUNK = 1024
TQ = 512

_NT = (((1,), (1,)), ((), ()))


def _params(*sem):
    return pltpu.CompilerParams(dimension_semantics=sem, vmem_limit_bytes=VMEM_LIMIT)


def _rms(x, g):
    return x * lax.rsqrt(jnp.mean(x * x, axis=-1, keepdims=True) + EPS) * g


def _const_spec(shape):
    nd = len(shape)
    return pl.BlockSpec(shape, lambda *_: (0,) * nd)


def _mem_kv_kernel(mem_ref, g_ref, wkt_ref, wv_ref, kt_ref, v_ref):
    xn = _rms(mem_ref[0], g_ref[...]).astype(BF16)
    kt = lax.dot_general(wkt_ref[...], xn, _NT, preferred_element_type=F32)
    v = jnp.dot(xn, wv_ref[...], preferred_element_type=F32)
    kt = kt * (MEM_HEAD_DIM ** -0.5)
    row_head = lax.broadcasted_iota(jnp.int32, kt.shape, 0) // MEM_HEAD_DIM
    col_head = lax.broadcasted_iota(jnp.int32, v.shape, 1) // MEM_HEAD_DIM
    for h in range(MEM_HEADS):
        kt_ref[0, h] = jnp.where(row_head == h, kt, 0.0).astype(BF16)
        v_ref[0, h] = jnp.where(col_head == h, v, 0.0).astype(BF16)


def _mem_kv(mem, g, wkt, wv):
    B = mem.shape[0]
    out = jax.ShapeDtypeStruct((B, MEM_HEADS, MEM_WIDTH, N_MEM), BF16)
    return pl.pallas_call(
        _mem_kv_kernel,
        grid=(B,),
        in_specs=[pl.BlockSpec((1, N_MEM, D_MODEL), lambda b: (b, 0, 0)),
                  _const_spec(g.shape), _const_spec(wkt.shape), _const_spec(wv.shape)],
        out_specs=[pl.BlockSpec((1, MEM_HEADS, MEM_WIDTH, N_MEM), lambda b: (b, 0, 0, 0)),
                   pl.BlockSpec((1, MEM_HEADS, N_MEM, MEM_WIDTH), lambda b: (b, 0, 0, 0))],
        out_shape=[out, out],
        compiler_params=_params("arbitrary"),
        name="mem_kv",
    )(mem, g, wkt, wv)


def _inproj0_kernel(x_ref, g_ref, wqk_ref, wv_ref, wo_ref, wqm_ref, wg_ref,
                    qk_ref, v_ref, o_ref, qm_ref, gate_ref):
    xn = _rms(x_ref[0], g_ref[...]).astype(BF16)
    qk_ref[0] = jnp.dot(xn, wqk_ref[...], preferred_element_type=F32).astype(BF16)
    v = jnp.dot(xn, wv_ref[...], preferred_element_type=F32)
    lane = lax.broadcasted_iota(jnp.int32, v.shape, 1) % ML_V_PAD
    v_ref[0] = jnp.where(lane == ML_ONES_COL, 1.0, v).astype(BF16)
    o_ref[0] = jnp.dot(xn, wo_ref[...], preferred_element_type=F32).astype(BF16)
    qm_ref[0] = jnp.dot(xn, wqm_ref[...], preferred_element_type=F32).astype(BF16)
    gate_ref[0] = jnp.dot(xn, wg_ref[...], preferred_element_type=F32)


def _inproj0(x, g, wqk, wv, wo, wqm, wg):
    B, S, D = x.shape
    tm = min(TM_PROJ, S)
    tok = lambda w: pl.BlockSpec((1, tm, w), lambda b, i: (b, i, 0))
    widths = (wqk.shape[1], wv.shape[1], wo.shape[1], wqm.shape[1], wg.shape[1])
    dtypes = (BF16, BF16, BF16, BF16, F32)
    return pl.pallas_call(
        _inproj0_kernel,
        grid=(B, S // tm),
        in_specs=[tok(D), _const_spec(g.shape)] + [_const_spec(w.shape) for w in (wqk, wv, wo, wqm, wg)],
        out_specs=[tok(w) for w in widths],
        out_shape=[jax.ShapeDtypeStruct((B, S, w), dt) for w, dt in zip(widths, dtypes)],
        compiler_params=_params("arbitrary", "arbitrary"),
        name="inproj0",
    )(x, g, wqk, wv, wo, wqm, wg)


def _split3(x):
    hi = x.astype(BF16)
    r = x - hi.astype(F32)
    mid = r.astype(BF16)
    lo = (r - mid.astype(F32)).astype(BF16)
    return hi, mid, lo


def _mlstm_kernel(qk_ref, v_ref, o_ref, gate_ref, wconv_ref, bias_ref, hnorm_ref,
                  y_ref, c_ref, m_ref, tail_ref):
    L = ML_CHUNK
    HQ = ML_HEADS * ML_QK_PAD

    @pl.when(pl.program_id(1) == 0)
    def _():
        c_ref[...] = jnp.zeros_like(c_ref)
        m_ref[...] = jnp.zeros_like(m_ref)
        tail_ref[...] = jnp.zeros_like(tail_ref)

    x = qk_ref[0].astype(F32)
    tail = tail_ref[...]
    wc = wconv_ref[...]
    row8 = lax.broadcasted_iota(jnp.int32, tail.shape, 0)
    conv = x * wc[ML_CONV - 1:ML_CONV]
    for s in range(1, ML_CONV):
        xs = pltpu.roll(x, s, axis=0)
        head = jnp.where(row8 < s, pltpu.roll(tail, s, axis=0), xs[0:8])
        xs = jnp.concatenate([head, xs[8:]], axis=0)
        conv = conv + xs * wc[ML_CONV - 1 - s:ML_CONV - s]
    tail_ref[...] = x[L - 8:L]
    act = conv * jax.nn.sigmoid(conv)
    q = act[:, :HQ].astype(BF16)
    k32 = act[:, HQ:] * (ML_QK_DIM ** -0.5)
    k = k32.astype(BF16)
    v = v_ref[0]

    gates = gate_ref[0] + bias_ref[...]
    lane = lax.broadcasted_iota(jnp.int32, gates.shape, 1)
    log_f = jnp.minimum(gates, 0.0) - jnp.log1p(jnp.exp(-jnp.abs(gates)))
    log_f = jnp.where((lane >= ML_HEADS) & (lane < 2 * ML_HEADS), log_f, 0.0)
    r_i = lax.broadcasted_iota(jnp.int32, (L, L), 0)
    c_i = lax.broadcasted_iota(jnp.int32, (L, L), 1)
    causal = c_i <= r_i
    tril = jnp.where(causal, 1.0, 0.0).astype(BF16)
    b_cols = sum(jnp.dot(tril, part, preferred_element_type=F32) for part in _split3(log_f))
    b_rows = b_cols.T
    g_rows = gates.T

    hn_all = hnorm_ref[...]
    vlane = lax.broadcasted_iota(jnp.int32, (L, ML_V_PAD), 1)
    for h in range(ML_HEADS):
        b_col = b_cols[:, ML_HEADS + h:ML_HEADS + h + 1]
        b_row = b_rows[ML_HEADS + h:ML_HEADS + h + 1, :]
        i_row = g_rows[h:h + 1, :]
        i_col = gates[:, h:h + 1]
        m_prev = m_ref[h][:, 0:1]
        qh = q[:, h * ML_QK_PAD:(h + 1) * ML_QK_PAD]
        kh = k[:, h * ML_QK_PAD:(h + 1) * ML_QK_PAD]
        vh = v[:, h * ML_V_PAD:(h + 1) * ML_V_PAD]
        c_prev = c_ref[h]

        log_w = jnp.where(causal, b_col - b_row + i_row, -jnp.inf)
        log_inter = b_col + m_prev
        m_t = jnp.maximum(log_inter, jnp.max(log_w, axis=-1, keepdims=True))
        w = jnp.exp(log_w - m_t)
        a_inter = jnp.exp(log_inter - m_t)
        s = lax.dot_general(qh, kh, _NT, preferred_element_type=F32) * w
        num = (jnp.dot(s.astype(BF16), vh, preferred_element_type=F32)
               + a_inter * jnp.dot(qh, c_prev.astype(BF16), preferred_element_type=F32))
        den = num[:, ML_ONES_COL:ML_ONES_COL + 1]
        inv = 1.0 / jnp.maximum(jnp.abs(den), jnp.exp(-m_t))
        hc = jnp.where(vlane < ML_V_DIM, num * inv, 0.0)
        ms = jnp.sum(hc * hc, axis=-1, keepdims=True) * (1.0 / ML_V_DIM)
        hn = hc * lax.rsqrt(ms + EPS) * hn_all[:, h * ML_V_PAD:(h + 1) * ML_V_PAD]
        og = o_ref[0, :, h * ML_V_PAD:(h + 1) * ML_V_PAD].astype(F32)
        y_ref[0, :, h * ML_V_PAD:(h + 1) * ML_V_PAD] = (hn * jax.nn.sigmoid(og)).astype(BF16)

        b_end = b_row[:, L - 1:L]
        m_new = jnp.maximum(b_end + m_prev, jnp.max(b_end - b_row + i_row, axis=-1, keepdims=True))
        u_col = jnp.exp(b_end - b_col + i_col - m_new)
        decay = jnp.exp(b_end + m_prev - m_new)
        uk = (k32[:, h * ML_QK_PAD:(h + 1) * ML_QK_PAD] * u_col).astype(BF16)
        c_ref[h] = decay * c_prev + lax.dot_general(uk, vh, (((0,), (0,)), ((), ())),
                                                    preferred_element_type=F32)
        m_ref[h] = jnp.broadcast_to(m_new, (1, LANES))


def _mlstm(qk, v, o, gates, wconv, bias, hnorm):
    B, S, _ = qk.shape
    L = ML_CHUNK
    tok = lambda w: pl.BlockSpec((1, L, w), lambda b, c: (b, c, 0))
    return pl.pallas_call(
        _mlstm_kernel,
        grid=(B, S // L),
        in_specs=[tok(qk.shape[2]), tok(v.shape[2]), tok(o.shape[2]), tok(gates.shape[2]),
                  _const_spec(wconv.shape), _const_spec(bias.shape), _const_spec(hnorm.shape)],
        out_specs=tok(v.shape[2]),
        out_shape=jax.ShapeDtypeStruct(v.shape, BF16),
        scratch_shapes=[pltpu.VMEM((ML_HEADS, ML_QK_PAD, ML_V_PAD), F32),
                        pltpu.VMEM((ML_HEADS, 1, LANES), F32),
                        pltpu.VMEM((8, qk.shape[2]), F32)],
        compiler_params=_params("arbitrary", "arbitrary"),
        name="mlstm",
    )(qk, v, o, gates, wconv, bias, hnorm)


def _outproj_kernel(x_ref, y_ref, qm_ref, kt_ref, vm_ref, wa_ref, wb_ref, out_ref):
    qm = qm_ref[0]
    ymem = None
    for h in range(MEM_HEADS):
        s = jnp.dot(qm, kt_ref[0, h], preferred_element_type=F32)
        e = jnp.exp(s - jnp.max(s, axis=-1, keepdims=True))
        p = (e * (1.0 / jnp.sum(e, axis=-1, keepdims=True))).astype(BF16)
        oh = jnp.dot(p, vm_ref[0, h], preferred_element_type=F32)
        ymem = oh if ymem is None else ymem + oh
    out = x_ref[0] + jnp.dot(y_ref[0], wa_ref[...], preferred_element_type=F32)
    out_ref[0] = out + jnp.dot(ymem.astype(BF16), wb_ref[...], preferred_element_type=F32)


def _outproj(x, y, qm, kt, vm, wa, wb):
    B, S, D = x.shape
    tm = min(TM_PROJ, S)
    tok = lambda w: pl.BlockSpec((1, tm, w), lambda b, i: (b, i, 0))
    per_b = pl.BlockSpec((1,) + kt.shape[1:], lambda b, i: (b, 0, 0, 0))
    return pl.pallas_call(
        _outproj_kernel,
        grid=(B, S // tm),
        in_specs=[tok(D), tok(y.shape[2]), tok(qm.shape[2]), per_b, per_b,
                  _const_spec(wa.shape), _const_spec(wb.shape)],
        out_specs=tok(D),
        out_shape=jax.ShapeDtypeStruct(x.shape, F32),
        compiler_params=_params("arbitrary", "arbitrary"),
        name="outproj",
    )(x, y, qm, kt, vm, wa, wb)


def _ffn_kernel(x_ref, g_ref, w1_ref, w2_ref, gf_ref, out_ref, *, final_norm):
    x = x_ref[0]
    hn = _rms(x, g_ref[...]).astype(BF16)
    acc = x
    for c in range(D_FF // FF_CHUNK):
        a = jnp.dot(hn, w1_ref[:, c * FF_CHUNK:(c + 1) * FF_CHUNK], preferred_element_type=F32)
        a = jnp.square(jnp.maximum(a, 0.0)).astype(BF16)
        acc = acc + jnp.dot(a, w2_ref[c * FF_CHUNK:(c + 1) * FF_CHUNK, :], preferred_element_type=F32)
    if final_norm:
        acc = _rms(acc, gf_ref[...])
    out_ref[0] = acc


def _ffn(x, g, w1, w2, gf, final_norm):
    B, S, D = x.shape
    tm = min(TM_PROJ, S)
    tok = pl.BlockSpec((1, tm, D), lambda b, i: (b, i, 0))
    return pl.pallas_call(
        functools.partial(_ffn_kernel, final_norm=final_norm),
        grid=(B, S // tm),
        in_specs=[tok, _const_spec(g.shape), _const_spec(w1.shape), _const_spec(w2.shape),
                  _const_spec(gf.shape)],
        out_specs=tok,
        out_shape=jax.ShapeDtypeStruct(x.shape, F32),
        compiler_params=_params("arbitrary", "arbitrary"),
        name="ffn",
    )(x, g, w1, w2, gf)


def _rope_kernel(pos_ref, inv_ref, cos_ref, sin_ref):
    ang = pos_ref[0].astype(F32) * inv_ref[...]
    cos_ref[0] = jnp.cos(ang)
    sin_ref[0] = jnp.sin(ang)


def _rope_tables(positions):
    B, S = positions.shape
    nf = MLA_ROPE // 2
    inv = (ROPE_THETA ** (-jnp.arange(0, MLA_ROPE, 2, dtype=F32) / MLA_ROPE)).reshape(nf, 1)
    out = jax.ShapeDtypeStruct((B, nf, S), F32)
    spec = pl.BlockSpec((1, nf, S), lambda b: (b, 0, 0))
    return pl.pallas_call(
        _rope_kernel,
        grid=(B,),
        in_specs=[pl.BlockSpec((1, 1, S), lambda b: (b, 0, 0)), _const_spec(inv.shape)],
        out_specs=[spec, spec],
        out_shape=[out, out],
        compiler_params=_params("arbitrary"),
        name="rope_tables",
    )(positions.reshape(B, 1, S), inv)


def _inproj1_kernel(x_ref, g_ref, win_ref, wkrt_ref, gq_ref, gkv_ref, wuqt_ref, wuk_ref, wuvt_ref,
                    place_ref, cos_ref, sin_ref, qt_ref, k_ref, vt_ref, qm_ref):
    half = MLA_ROPE // 2
    xn = _rms(x_ref[0], g_ref[...]).astype(BF16)
    c = jnp.dot(xn, win_ref[...], preferred_element_type=F32)
    o_kv = MLA_Q_RANK
    o_qm = o_kv + MLA_KV_RANK
    qm_ref[0] = c[:, o_qm:].astype(BF16)
    cq = _rms(c[:, :o_kv], gq_ref[...]).astype(BF16)
    ckv = _rms(c[:, o_kv:o_qm], gkv_ref[...]).astype(BF16)
    cos_t, sin_t = cos_ref[0], sin_ref[0]

    def rope_t(x1, x2):
        return x1 * cos_t - x2 * sin_t, x2 * cos_t + x1 * sin_t

    qscale = ((MLA_NOPE + MLA_ROPE) ** -0.5) * math.log2(math.e)
    qt = lax.dot_general(wuqt_ref[...], cq, _NT, preferred_element_type=F32)
    zpad = jnp.zeros((MLA_QK_PAD - MLA_NOPE - MLA_ROPE, qt.shape[1]), F32)
    for h in range(MLA_HEADS):
        b0 = h * MLA_QK_PAD
        r1, r2 = rope_t(qt[b0 + MLA_NOPE:b0 + MLA_NOPE + half],
                        qt[b0 + MLA_NOPE + half:b0 + MLA_NOPE + MLA_ROPE])
        blk = jnp.concatenate([qt[b0:b0 + MLA_NOPE], r1, r2, zpad], axis=0) * qscale
        qt_ref[0, b0:b0 + MLA_QK_PAD] = blk.astype(BF16)

    krt = lax.dot_general(wkrt_ref[...], xn, _NT, preferred_element_type=F32)
    r1, r2 = rope_t(krt[:half], krt[half:MLA_ROPE])
    kr = jnp.concatenate([r1, r2, krt[MLA_ROPE:]], axis=0).T.astype(BF16)
    k = (jnp.dot(ckv, wuk_ref[...], preferred_element_type=F32)
         + jnp.dot(kr, place_ref[...], preferred_element_type=F32))
    k_ref[0] = k.astype(BF16)

    vt = lax.dot_general(wuvt_ref[...], ckv, _NT, preferred_element_type=F32)
    rowid = lax.broadcasted_iota(jnp.int32, vt.shape, 0) % MLA_VT_ROWS
    vt_ref[0, 0] = jnp.where(rowid == MLA_V, 1.0, vt).astype(BF16)


def _inproj1(x, g, win, wkrt, gq, gkv, wuqt, wuk, wuvt, place, cos_t, sin_t):
    B, S, D = x.shape
    tm = TQ
    assert S % tm == 0
    tok = lambda w: pl.BlockSpec((1, tm, w), lambda b, i: (b, i, 0))
    tok_t = lambda r: pl.BlockSpec((1, r, tm), lambda b, i: (b, 0, i))
    consts = (g, win, wkrt, gq, gkv, wuqt, wuk, wuvt, place)
    nvt = wuvt.shape[0]
    return pl.pallas_call(
        _inproj1_kernel,
        grid=(B, S // tm),
        in_specs=[tok(D)] + [_const_spec(a.shape) for a in consts] + [tok_t(cos_t.shape[1])] * 2,
        out_specs=[tok_t(wuqt.shape[0]), tok(wuk.shape[1]),
                   pl.BlockSpec((1, 1, nvt, tm), lambda b, i: (b, i, 0, 0)), tok(MEM_WIDTH)],
        out_shape=[jax.ShapeDtypeStruct((B, wuqt.shape[0], S), BF16),
                   jax.ShapeDtypeStruct((B, S, wuk.shape[1]), BF16),
                   jax.ShapeDtypeStruct((B, S // tm, nvt, tm), BF16),
                   jax.ShapeDtypeStruct((B, S, MEM_WIDTH), BF16)],
        compiler_params=_params("arbitrary", "arbitrary"),
        name="inproj1",
    )(x, *consts, cos_t, sin_t)


def _flash_kernel(qt_ref, k_ref, vt_ref, y_ref, acc_ref, m_ref):
    i = pl.program_id(2)
    key_i = lax.broadcasted_iota(jnp.int32, (TQ, TQ), 0)
    qry_i = lax.broadcasted_iota(jnp.int32, (TQ, TQ), 1)
    diag_mask = key_i <= qry_i

    def scores_t(hh, blk):
        kh = k_ref[0, pl.ds(pl.multiple_of(blk * TQ, TQ), TQ), hh * MLA_QK_PAD:(hh + 1) * MLA_QK_PAD]
        qh = qt_ref[0, hh * MLA_QK_PAD:(hh + 1) * MLA_QK_PAD, :]
        return jnp.dot(kh, qh, preferred_element_type=F32)

    def v_t(hh, blk):
        return vt_ref[0, blk, hh * MLA_VT_ROWS:(hh + 1) * MLA_VT_ROWS, :]

    for hh in range(2):
        s = jnp.where(diag_mask, scores_t(hh, i), -jnp.inf)
        m = jnp.max(s, axis=0, keepdims=True)
        p = jnp.exp2(s - m).astype(BF16)
        acc_ref[hh] = jnp.dot(v_t(hh, i), p, preferred_element_type=F32)
        m_ref[hh] = m

    def body(j, carry):
        for hh in range(2):
            s = scores_t(hh, j)
            m_prev = m_ref[hh]
            m_new = jnp.maximum(m_prev, jnp.max(s, axis=0, keepdims=True))
            alpha = jnp.exp2(m_prev - m_new)
            p = jnp.exp2(s - m_new).astype(BF16)
            acc_ref[hh] = alpha * acc_ref[hh] + jnp.dot(v_t(hh, j), p, preferred_element_type=F32)
            m_ref[hh] = m_new
        return carry

    lax.fori_loop(0, i, body, 0)

    outs = []
    for hh in range(2):
        acc = acc_ref[hh]
        outs.append(acc[:MLA_V] * (1.0 / acc[MLA_V:MLA_V + 1]))
    y_ref[0] = jnp.concatenate(outs, axis=0).T.astype(BF16)


def _flash(qt, k, vt):
    B, S, _ = k.shape
    assert S % TQ == 0
    pairs = MLA_HEADS // 2
    return pl.pallas_call(
        _flash_kernel,
        grid=(B, pairs, S // TQ),
        in_specs=[pl.BlockSpec((1, 2 * MLA_QK_PAD, TQ), lambda b, p, i: (b, p, i)),
                  pl.BlockSpec((1, S, 2 * MLA_QK_PAD), lambda b, p, i: (b, 0, p)),
                  pl.BlockSpec((1, S // TQ, 2 * MLA_VT_ROWS, TQ), lambda b, p, i: (b, 0, p, 0))],
        out_specs=pl.BlockSpec((1, TQ, 2 * MLA_V), lambda b, p, i: (b, i, p)),
        out_shape=jax.ShapeDtypeStruct((B, S, MLA_HEADS * MLA_V), BF16),
        scratch_shapes=[pltpu.VMEM((2, MLA_VT_ROWS, TQ), F32),
                        pltpu.VMEM((2, 1, TQ), F32)],
        compiler_params=_params("arbitrary", "arbitrary", "arbitrary"),
        name="mla_flash",
    )(qt, k, vt)


def _pad_heads(w, heads, d, dp):
    lead = w.shape[:-1]
    w = w.reshape(lead + (heads, d))
    w = jnp.pad(w, [(0, 0)] * len(lead) + [(0, 0), (0, dp - d)])
    return w.reshape(lead + (heads * dp,))


def _pad_head_rows(w, heads, d, dp):
    return _pad_heads(w.T, heads, d, dp).T


def kernel(x, mem, positions, mem_norm, w_mem_kv, norm_mix0, w_in0, b_igate0, b_fgate0, w_conv0, w_hnorm0, w_out0, norm_ffn0, w_ff1_0, w_ff2_0, norm_mix1, w_in1, w_qnorm1, w_uq1, w_kvnorm1, w_ukv1, w_out1, norm_ffn1, w_ff1_1, w_ff2_1, final_norm):
    row = lambda g: g.reshape(1, -1).astype(F32)

    kt, vm = _mem_kv(mem, row(mem_norm), w_mem_kv[:, :MEM_WIDTH].T.astype(BF16),
                     w_mem_kv[:, MEM_WIDTH:].astype(BF16))

    nq = ML_HEADS * ML_QK_DIM
    o_v, o_o, o_g = 2 * nq, 2 * nq + MIX_WIDTH, 2 * nq + 2 * MIX_WIDTH
    o_qm = o_g + 2 * ML_HEADS
    wqk = jnp.concatenate([_pad_heads(w_in0[:, :nq], ML_HEADS, ML_QK_DIM, ML_QK_PAD),
                           _pad_heads(w_in0[:, nq:o_v], ML_HEADS, ML_QK_DIM, ML_QK_PAD)], axis=1)
    wv0 = _pad_heads(w_in0[:, o_v:o_o], ML_HEADS, ML_V_DIM, ML_V_PAD)
    wo0 = _pad_heads(w_in0[:, o_o:o_g], ML_HEADS, ML_V_DIM, ML_V_PAD)
    wg0 = jnp.pad(w_in0[:, o_g:o_qm], ((0, 0), (0, LANES - 2 * ML_HEADS)))
    wqm0 = w_in0[:, o_qm:]
    wconv = jnp.concatenate([_pad_heads(w_conv0[:, :nq], ML_HEADS, ML_QK_DIM, ML_QK_PAD),
                             _pad_heads(w_conv0[:, nq:], ML_HEADS, ML_QK_DIM, ML_QK_PAD)], axis=1).astype(F32)
    gate_bias = jnp.pad(jnp.concatenate([b_igate0, b_fgate0]), (0, LANES - 2 * ML_HEADS)).reshape(1, LANES).astype(F32)
    hnorm = _pad_heads(w_hnorm0.reshape(1, -1), ML_HEADS, ML_V_DIM, ML_V_PAD).astype(F32)
    wa0 = _pad_head_rows(w_out0[:MIX_WIDTH], ML_HEADS, ML_V_DIM, ML_V_PAD).astype(BF16)
    wb0 = w_out0[MIX_WIDTH:].astype(BF16)

    qk, v0, o0, qm0, gates = _inproj0(x, row(norm_mix0), wqk.astype(BF16), wv0.astype(BF16),
                                      wo0.astype(BF16), wqm0.astype(BF16), wg0.astype(BF16))
    y0 = _mlstm(qk, v0, o0, gates, wconv, gate_bias, hnorm)
    x = _outproj(x, y0, qm0, kt, vm, wa0, wb0)
    x = _ffn(x, row(norm_ffn0), w_ff1_0.astype(BF16), w_ff2_0.astype(BF16), row(final_norm), False)

    o_kr = MLA_Q_RANK + MLA_KV_RANK
    o_qm1 = o_kr + MLA_ROPE
    win1 = jnp.concatenate([w_in1[:, :o_kr], w_in1[:, o_qm1:]], axis=1)
    wkrt = jnp.pad(w_in1[:, o_kr:o_qm1].T, ((0, LANES - MLA_ROPE), (0, 0)))
    wuqt = _pad_heads(w_uq1, MLA_HEADS, MLA_NOPE + MLA_ROPE, MLA_QK_PAD).T
    ukv = w_ukv1.reshape(MLA_KV_RANK, MLA_HEADS, MLA_NOPE + MLA_V)
    wuk = _pad_heads(ukv[:, :, :MLA_NOPE].reshape(MLA_KV_RANK, -1), MLA_HEADS, MLA_NOPE, MLA_QK_PAD)
    wuvt = _pad_heads(ukv[:, :, MLA_NOPE:].reshape(MLA_KV_RANK, -1), MLA_HEADS, MLA_V, MLA_VT_ROWS).T
    place = np.zeros((LANES, MLA_HEADS * MLA_QK_PAD), np.float32)
    for h in range(MLA_HEADS):
        place[np.arange(MLA_ROPE), h * MLA_QK_PAD + MLA_NOPE + np.arange(MLA_ROPE)] = 1.0
    place = jnp.asarray(place, BF16)

    cos_t, sin_t = _rope_tables(positions)
    qt1, k1, vt1, qm1 = _inproj1(x, row(norm_mix1), win1.astype(BF16), wkrt.astype(BF16),
                                 row(w_qnorm1), row(w_kvnorm1), wuqt.astype(BF16), wuk.astype(BF16),
                                 wuvt.astype(BF16), place, cos_t, sin_t)
    y1 = _flash(qt1, k1, vt1)
    x = _outproj(x, y1, qm1, kt, vm, w_out1[:MIX_WIDTH].astype(BF16), w_out1[MIX_WIDTH:].astype(BF16))
    return _ffn(x, row(norm_ffn1), w_ff1_1.astype(BF16), w_ff2_1.astype(BF16), row(final_norm), True)
```

```python
import functools
import math

import numpy as np
import jax
import jax.numpy as jnp
from jax import lax
from jax.experimental import pallas as pl
from jax.experimental.pallas import tpu as pltpu

F32 = jnp.float32
BF16 = jnp.bfloat16
EPS = 1e-6

D_MODEL = 1024
N_MEM = 256
MEM_HEADS = 4
MEM_HEAD_DIM = 64
MEM_WIDTH = MEM_HEADS * MEM_HEAD_DIM
MIX_WIDTH = D_MODEL - MEM_WIDTH

ML_HEADS = 4
ML_V_DIM = MIX_WIDTH // ML_HEADS
ML_QK_DIM = ML_V_DIM // 2
ML_CONV = 4
ML_CHUNK = 128
ML_QK_PAD = 128
ML_V_PAD = 256
ML_ONES_COL = ML_V_DIM

MLA_HEADS = 12
MLA_NOPE = 64
MLA_ROPE = 32
MLA_V = 64
MLA_Q_RANK = 384
MLA_KV_RANK = 256
MLA_QK_PAD = 128
MLA_VT_ROWS = 80
ROPE_THETA = 10000.0
D_FF = 4 * D_MODEL

LANES = 128
VMEM_LIMIT = 56 * 1024 * 1024

TM_PROJ = 512
FF_CHUNK = 1024
TQ = 512

_NT = (((1,), (1,)), ((), ()))


def _params(*sem):
    return pltpu.CompilerParams(dimension_semantics=sem, vmem_limit_bytes=VMEM_LIMIT)


def _rms(x, g):
    return x * lax.rsqrt(jnp.mean(x * x, axis=-1, keepdims=True) + EPS) * g


def _const_spec(shape):
    nd = len(shape)
    return pl.BlockSpec(shape, lambda *_: (0,) * nd)


def _mem_kv_kernel(mem_ref, g_ref, wkt_ref, wv_ref, kt_ref, v_ref):
    xn = _rms(mem_ref[0], g_ref[...]).astype(BF16)
    kt = lax.dot_general(wkt_ref[...], xn, _NT, preferred_element_type=F32)
    v = jnp.dot(xn, wv_ref[...], preferred_element_type=F32)
    kt = kt * (MEM_HEAD_DIM ** -0.5)
    row_head = lax.broadcasted_iota(jnp.int32, kt.shape, 0) // MEM_HEAD_DIM
    col_head = lax.broadcasted_iota(jnp.int32, v.shape, 1) // MEM_HEAD_DIM
    for h in range(MEM_HEADS):
        kt_ref[0, h] = jnp.where(row_head == h, kt, 0.0).astype(BF16)
        v_ref[0, h] = jnp.where(col_head == h, v, 0.0).astype(BF16)


def _mem_kv(mem, g, wkt, wv):
    B = mem.shape[0]
    out = jax.ShapeDtypeStruct((B, MEM_HEADS, MEM_WIDTH, N_MEM), BF16)
    return pl.pallas_call(
        _mem_kv_kernel,
        grid=(B,),
        in_specs=[pl.BlockSpec((1, N_MEM, D_MODEL), lambda b: (b, 0, 0)),
                  _const_spec(g.shape), _const_spec(wkt.shape), _const_spec(wv.shape)],
        out_specs=[pl.BlockSpec((1, MEM_HEADS, MEM_WIDTH, N_MEM), lambda b: (b, 0, 0, 0)),
                   pl.BlockSpec((1, MEM_HEADS, N_MEM, MEM_WIDTH), lambda b: (b, 0, 0, 0))],
        out_shape=[out, out],
        compiler_params=_params("arbitrary"),
        name="mem_kv",
    )(mem, g, wkt, wv)


def _inproj0_kernel(x_ref, g_ref, wqk_ref, wv_ref, wo_ref, wqm_ref, wg_ref,
                    qk_ref, v_ref, o_ref, qm_ref, gate_ref):
    xn = _rms(x_ref[0], g_ref[...]).astype(BF16)
    qk_ref[0] = jnp.dot(xn, wqk_ref[...], preferred_element_type=F32).astype(BF16)
    v = jnp.dot(xn, wv_ref[...], preferred_element_type=F32)
    lane = lax.broadcasted_iota(jnp.int32, v.shape, 1) % ML_V_PAD
    v_ref[0] = jnp.where(lane == ML_ONES_COL, 1.0, v).astype(BF16)
    o_ref[0] = jnp.dot(xn, wo_ref[...], preferred_element_type=F32).astype(BF16)
    qm_ref[0] = jnp.dot(xn, wqm_ref[...], preferred_element_type=F32).astype(BF16)
    gate_ref[0] = jnp.dot(xn, wg_ref[...], preferred_element_type=F32)


def _inproj0(x, g, wqk, wv, wo, wqm, wg):
    B, S, D = x.shape
    tm = min(TM_PROJ, S)
    tok = lambda w: pl.BlockSpec((1, tm, w), lambda b, i: (b, i, 0))
    widths = (wqk.shape[1], wv.shape[1], wo.shape[1], wqm.shape[1], wg.shape[1])
    dtypes = (BF16, BF16, BF16, BF16, F32)
    return pl.pallas_call(
        _inproj0_kernel,
        grid=(B, S // tm),
        in_specs=[tok(D), _const_spec(g.shape)] + [_const_spec(w.shape) for w in (wqk, wv, wo, wqm, wg)],
        out_specs=[tok(w) for w in widths],
        out_shape=[jax.ShapeDtypeStruct((B, S, w), dt) for w, dt in zip(widths, dtypes)],
        compiler_params=_params("arbitrary", "arbitrary"),
        name="inproj0",
    )(x, g, wqk, wv, wo, wqm, wg)


def _split3(x):
    hi = x.astype(BF16)
    r = x - hi.astype(F32)
    mid = r.astype(BF16)
    lo = (r - mid.astype(F32)).astype(BF16)
    return hi, mid, lo


def _mlstm_kernel(qk_ref, v_ref, o_ref, gate_ref, wconv_ref, bias_ref, hnorm_ref,
                  y_ref, c_ref, m_ref, tail_ref):
    L = ML_CHUNK
    HQ = ML_HEADS * ML_QK_PAD

    @pl.when(pl.program_id(1) == 0)
    def _():
        c_ref[...] = jnp.zeros_like(c_ref)
        m_ref[...] = jnp.zeros_like(m_ref)
        tail_ref[...] = jnp.zeros_like(tail_ref)

    x = qk_ref[0].astype(F32)
    tail = tail_ref[...]
    wc = wconv_ref[...]
    row8 = lax.broadcasted_iota(jnp.int32, tail.shape, 0)
    conv = x * wc[ML_CONV - 1:ML_CONV]
    for s in range(1, ML_CONV):
        xs = pltpu.roll(x, s, axis=0)
        head = jnp.where(row8 < s, pltpu.roll(tail, s, axis=0), xs[0:8])
        xs = jnp.concatenate([head, xs[8:]], axis=0)
        conv = conv + xs * wc[ML_CONV - 1 - s:ML_CONV - s]
    tail_ref[...] = x[L - 8:L]
    act = conv * jax.nn.sigmoid(conv)
    q = act[:, :HQ].astype(BF16)
    k32 = act[:, HQ:] * (ML_QK_DIM ** -0.5)
    k = k32.astype(BF16)
    v = v_ref[0]

    gates = gate_ref[0] + bias_ref[...]
    lane = lax.broadcasted_iota(jnp.int32, gates.shape, 1)
    log_f = jnp.minimum(gates, 0.0) - jnp.log1p(jnp.exp(-jnp.abs(gates)))
    log_f = jnp.where((lane >= ML_HEADS) & (lane < 2 * ML_HEADS), log_f, 0.0)
    r_i = lax.broadcasted_iota(jnp.int32, (L, L), 0)
    c_i = lax.broadcasted_iota(jnp.int32, (L, L), 1)
    causal = c_i <= r_i
    tril = jnp.where(causal, 1.0, 0.0).astype(BF16)
    b_cols = sum(jnp.dot(tril, part, preferred_element_type=F32) for part in _split3(log_f))
    b_rows = b_cols.T
    g_rows = gates.T

    hn_all = hnorm_ref[...]
    vlane = lax.broadcasted_iota(jnp.int32, (L, ML_V_PAD), 1)
    for h in range(ML_HEADS):
        b_col = b_cols[:, ML_HEADS + h:ML_HEADS + h + 1]
        b_row = b_rows[ML_HEADS + h:ML_HEADS + h + 1, :]
        i_row = g_rows[h:h + 1, :]
        i_col = gates[:, h:h + 1]
        m_prev = m_ref[h][:, 0:1]
        qh = q[:, h * ML_QK_PAD:(h + 1) * ML_QK_PAD]
        kh = k[:, h * ML_QK_PAD:(h + 1) * ML_QK_PAD]
        vh = v[:, h * ML_V_PAD:(h + 1) * ML_V_PAD]
        c_prev = c_ref[h]

        log_w = jnp.where(causal, b_col - b_row + i_row, -jnp.inf)
        log_inter = b_col + m_prev
        m_t = jnp.maximum(log_inter, jnp.max(log_w, axis=-1, keepdims=True))
        w = jnp.exp(log_w - m_t)
        a_inter = jnp.exp(log_inter - m_t)
        s = lax.dot_general(qh, kh, _NT, preferred_element_type=F32) * w
        num = (jnp.dot(s.astype(BF16), vh, preferred_element_type=F32)
               + a_inter * jnp.dot(qh, c_prev.astype(BF16), preferred_element_type=F32))
        den = num[:, ML_ONES_COL:ML_ONES_COL + 1]
        inv = 1.0 / jnp.maximum(jnp.abs(den), jnp.exp(-m_t))
        hc = jnp.where(vlane < ML_V_DIM, num * inv, 0.0)
        ms = jnp.sum(hc * hc, axis=-1, keepdims=True) * (1.0 / ML_V_DIM)
        hn = hc * lax.rsqrt(ms + EPS) * hn_all[:, h * ML_V_PAD:(h + 1) * ML_V_PAD]
        og = o_ref[0, :, h * ML_V_PAD:(h + 1) * ML_V_PAD].astype(F32)
        y_ref[0, :, h * ML_V_PAD:(h + 1) * ML_V_PAD] = (hn * jax.nn.sigmoid(og)).astype(BF16)

        b_end = b_row[:, L - 1:L]
        m_new = jnp.maximum(b_end + m_prev, jnp.max(b_end - b_row + i_row, axis=-1, keepdims=True))
        u_col = jnp.exp(b_end - b_col + i_col - m_new)
        decay = jnp.exp(b_end + m_prev - m_new)
        uk = (k32[:, h * ML_QK_PAD:(h + 1) * ML_QK_PAD] * u_col).astype(BF16)
        c_ref[h] = decay * c_prev + lax.dot_general(uk, vh, (((0,), (0,)), ((), ())),
                                                    preferred_element_type=F32)
        m_ref[h] = jnp.broadcast_to(m_new, (1, LANES))


def _mlstm(qk, v, o, gates, wconv, bias, hnorm):
    B, S, _ = qk.shape
    L = ML_CHUNK
    tok = lambda w: pl.BlockSpec((1, L, w), lambda b, c: (b, c, 0))
    return pl.pallas_call(
        _mlstm_kernel,
        grid=(B, S // L),
        in_specs=[tok(qk.shape[2]), tok(v.shape[2]), tok(o.shape[2]), tok(gates.shape[2]),
                  _const_spec(wconv.shape), _const_spec(bias.shape), _const_spec(hnorm.shape)],
        out_specs=tok(v.shape[2]),
        out_shape=jax.ShapeDtypeStruct(v.shape, BF16),
        scratch_shapes=[pltpu.VMEM((ML_HEADS, ML_QK_PAD, ML_V_PAD), F32),
                        pltpu.VMEM((ML_HEADS, 1, LANES), F32),
                        pltpu.VMEM((8, qk.shape[2]), F32)],
        compiler_params=_params("arbitrary", "arbitrary"),
        name="mlstm",
    )(qk, v, o, gates, wconv, bias, hnorm)


def _outproj_kernel(x_ref, y_ref, qm_ref, kt_ref, vm_ref, wa_ref, wb_ref, out_ref):
    qm = qm_ref[0]
    ymem = None
    for h in range(MEM_HEADS):
        s = jnp.dot(qm, kt_ref[0, h], preferred_element_type=F32)
        e = jnp.exp(s - jnp.max(s, axis=-1, keepdims=True))
        p = (e * (1.0 / jnp.sum(e, axis=-1, keepdims=True))).astype(BF16)
        oh = jnp.dot(p, vm_ref[0, h], preferred_element_type=F32)
        ymem = oh if ymem is None else ymem + oh
    out = x_ref[0] + jnp.dot(y_ref[0], wa_ref[...], preferred_element_type=F32)
    out_ref[0] = out + jnp.dot(ymem.astype(BF16), wb_ref[...], preferred_element_type=F32)


def _outproj(x, y, qm, kt, vm, wa, wb):
    B, S, D = x.shape
    tm = min(TM_PROJ, S)
    tok = lambda w: pl.BlockSpec((1, tm, w), lambda b, i: (b, i, 0))
    per_b = pl.BlockSpec((1,) + kt.shape[1:], lambda b, i: (b, 0, 0, 0))
    return pl.pallas_call(
        _outproj_kernel,
        grid=(B, S // tm),
        in_specs=[tok(D), tok(y.shape[2]), tok(qm.shape[2]), per_b, per_b,
                  _const_spec(wa.shape), _const_spec(wb.shape)],
        out_specs=tok(D),
        out_shape=jax.ShapeDtypeStruct(x.shape, F32),
        compiler_params=_params("arbitrary", "arbitrary"),
        name="outproj",
    )(x, y, qm, kt, vm, wa, wb)


def _ffn_kernel(x_ref, g_ref, w1_ref, w2_ref, gf_ref, out_ref, *, final_norm):
    x = x_ref[0]
    hn = _rms(x, g_ref[...]).astype(BF16)
    acc = x
    for c in range(D_FF // FF_CHUNK):
        a = jnp.dot(hn, w1_ref[:, c * FF_CHUNK:(c + 1) * FF_CHUNK], preferred_element_type=F32)
        a = jnp.square(jnp.maximum(a, 0.0)).astype(BF16)
        acc = acc + jnp.dot(a, w2_ref[c * FF_CHUNK:(c + 1) * FF_CHUNK, :], preferred_element_type=F32)
    if final_norm:
        acc = _rms(acc, gf_ref[...])
    out_ref[0] = acc


def _ffn(x, g, w1, w2, gf, final_norm):
    B, S, D = x.shape
    tm = min(TM_PROJ, S)
    tok = pl.BlockSpec((1, tm, D), lambda b, i: (b, i, 0))
    return pl.pallas_call(
        functools.partial(_ffn_kernel, final_norm=final_norm),
        grid=(B, S // tm),
        in_specs=[tok, _const_spec(g.shape), _const_spec(w1.shape), _const_spec(w2.shape),
                  _const_spec(gf.shape)],
        out_specs=tok,
        out_shape=jax.ShapeDtypeStruct(x.shape, F32),
        compiler_params=_params("arbitrary", "arbitrary"),
        name="ffn",
    )(x, g, w1, w2, gf)


def _rope_kernel(pos_ref, inv_ref, cos_ref, sin_ref):
    ang = pos_ref[0].astype(F32) * inv_ref[...]
    cos_ref[0] = jnp.cos(ang)
    sin_ref[0] = jnp.sin(ang)


def _rope_tables(positions):
    B, S = positions.shape
    nf = MLA_ROPE // 2
    inv = (ROPE_THETA ** (-jnp.arange(0, MLA_ROPE, 2, dtype=F32) / MLA_ROPE)).reshape(nf, 1)
    out = jax.ShapeDtypeStruct((B, nf, S), F32)
    spec = pl.BlockSpec((1, nf, S), lambda b: (b, 0, 0))
    return pl.pallas_call(
        _rope_kernel,
        grid=(B,),
        in_specs=[pl.BlockSpec((1, 1, S), lambda b: (b, 0, 0)), _const_spec(inv.shape)],
        out_specs=[spec, spec],
        out_shape=[out, out],
        compiler_params=_params("arbitrary"),
        name="rope_tables",
    )(positions.reshape(B, 1, S), inv)


def _inproj1_kernel(x_ref, g_ref, win_ref, wkrt_ref, gq_ref, gkv_ref, wuqt_ref, wuk_ref, wuvt_ref,
                    place_ref, cos_ref, sin_ref, qt_ref, k_ref, vt_ref, qm_ref):
    half = MLA_ROPE // 2
    xn = _rms(x_ref[0], g_ref[...]).astype(BF16)
    c = jnp.dot(xn, win_ref[...], preferred_element_type=F32)
    o_kv = MLA_Q_RANK
    o_qm = o_kv + MLA_KV_RANK
    qm_ref[0] = c[:, o_qm:].astype(BF16)
    cq = _rms(c[:, :o_kv], gq_ref[...]).astype(BF16)
    ckv = _rms(c[:, o_kv:o_qm], gkv_ref[...]).astype(BF16)
    cos_t, sin_t = cos_ref[0], sin_ref[0]

    def rope_t(x1, x2):
        return x1 * cos_t - x2 * sin_t, x2 * cos_t + x1 * sin_t

    qscale = ((MLA_NOPE + MLA_ROPE) ** -0.5) * math.log2(math.e)
    qt = lax.dot_general(wuqt_ref[...], cq, _NT, preferred_element_type=F32)
    zpad = jnp.zeros((MLA_QK_PAD - MLA_NOPE - MLA_ROPE, qt.shape[1]), F32)
    for h in range(MLA_HEADS):
        b0 = h * MLA_QK_PAD
        r1, r2 = rope_t(qt[b0 + MLA_NOPE:b0 + MLA_NOPE + half],
                        qt[b0 + MLA_NOPE + half:b0 + MLA_NOPE + MLA_ROPE])
        blk = jnp.concatenate([qt[b0:b0 + MLA_NOPE], r1, r2, zpad], axis=0) * qscale
        qt_ref[0, b0:b0 + MLA_QK_PAD] = blk.astype(BF16)

    krt = lax.dot_general(wkrt_ref[...], xn, _NT, preferred_element_type=F32)
    r1, r2 = rope_t(krt[:half], krt[half:MLA_ROPE])
    kr = jnp.concatenate([r1, r2, krt[MLA_ROPE:]], axis=0).T.astype(BF16)
    k = (jnp.dot(ckv, wuk_ref[...], preferred_element_type=F32)
         + jnp.dot(kr, place_ref[...], preferred_element_type=F32))
    k_ref[0] = k.astype(BF16)

    vt = lax.dot_general(wuvt_ref[...], ckv, _NT, preferred_element_type=F32)
    rowid = lax.broadcasted_iota(jnp.int32, vt.shape, 0) % MLA_VT_ROWS
    vt_ref[0, 0] = jnp.where(rowid == MLA_V, 1.0, vt).astype(BF16)


def _inproj1(x, g, win, wkrt, gq, gkv, wuqt, wuk, wuvt, place, cos_t, sin_t):
    B, S, D = x.shape
    tm = TQ
    assert S % tm == 0
    tok = lambda w: pl.BlockSpec((1, tm, w), lambda b, i: (b, i, 0))
    tok_t = lambda r: pl.BlockSpec((1, r, tm), lambda b, i: (b, 0, i))
    consts = (g, win, wkrt, gq, gkv, wuqt, wuk, wuvt, place)
    nvt = wuvt.shape[0]
    return pl.pallas_call(
        _inproj1_kernel,
        grid=(B, S // tm),
        in_specs=[tok(D)] + [_const_spec(a.shape) for a in consts] + [tok_t(cos_t.shape[1])] * 2,
        out_specs=[tok_t(wuqt.shape[0]), tok(wuk.shape[1]),
                   pl.BlockSpec((1, 1, nvt, tm), lambda b, i: (b, i, 0, 0)), tok(MEM_WIDTH)],
        out_shape=[jax.ShapeDtypeStruct((B, wuqt.shape[0], S), BF16),
                   jax.ShapeDtypeStruct((B, S, wuk.shape[1]), BF16),
                   jax.ShapeDtypeStruct((B, S // tm, nvt, tm), BF16),
                   jax.ShapeDtypeStruct((B, S, MEM_WIDTH), BF16)],
        compiler_params=_params("arbitrary", "arbitrary"),
        name="inproj1",
    )(x, *consts, cos_t, sin_t)


def _flash_kernel(qt_ref, k_ref, vt_ref, y_ref, sa_ref, sb_ref, smax_ref, acc_ref, m_ref):
    i = pl.program_id(2)
    slots = (sa_ref, sb_ref)

    def scores_t(hh, blk):
        kh = k_ref[0, pl.ds(pl.multiple_of(blk * TQ, TQ), TQ), hh * MLA_QK_PAD:(hh + 1) * MLA_QK_PAD]
        qh = qt_ref[0, hh * MLA_QK_PAD:(hh + 1) * MLA_QK_PAD, :]
        return jnp.dot(kh, qh, preferred_element_type=F32)

    def v_t(hh, blk):
        return vt_ref[0, blk, hh * MLA_VT_ROWS:(hh + 1) * MLA_VT_ROWS, :]

    def col_max(s):
        parts = [jnp.max(s[r:r + 128], axis=0, keepdims=True) for r in range(0, TQ, 128)]
        return jnp.maximum(jnp.maximum(parts[0], parts[1]), jnp.maximum(parts[2], parts[3]))

    def qk_stage(slot, blk):
        for hh in range(2):
            s = scores_t(hh, blk)
            slots[slot][hh] = s
            smax_ref[slot, hh] = col_max(s)

    def pv_stage(slot, blk):
        for hh in range(2):
            m_prev = m_ref[hh]
            m_new = jnp.maximum(m_prev, smax_ref[slot, hh])
            alpha = jnp.exp2(m_prev - m_new)
            p = jnp.exp2(slots[slot][hh] - m_new).astype(BF16)
            acc_ref[hh] = alpha * acc_ref[hh] + jnp.dot(v_t(hh, blk), p, preferred_element_type=F32)
            m_ref[hh] = m_new

    @pl.when(i >= 1)
    def _():
        qk_stage(0, 0)

    key_i = lax.broadcasted_iota(jnp.int32, (TQ, TQ), 0)
    qry_i = lax.broadcasted_iota(jnp.int32, (TQ, TQ), 1)
    for hh in range(2):
        s = jnp.where(key_i <= qry_i, scores_t(hh, i), -jnp.inf)
        m = jnp.max(s, axis=0, keepdims=True)
        p = jnp.exp2(s - m).astype(BF16)
        acc_ref[hh] = jnp.dot(v_t(hh, i), p, preferred_element_type=F32)
        m_ref[hh] = m

    def body(t, carry):
        qk_stage(1, 2 * t + 1)
        pv_stage(0, 2 * t)
        qk_stage(0, 2 * t + 2)
        pv_stage(1, 2 * t + 1)
        return carry

    lax.fori_loop(0, (i - 1) // 2, body, 0)

    @pl.when(i % 2 == 1)
    def _():
        pv_stage(0, i - 1)

    @pl.when((i % 2 == 0) & (i >= 2))
    def _():
        qk_stage(1, i - 1)
        pv_stage(0, i - 2)
        pv_stage(1, i - 1)

    outs = []
    for hh in range(2):
        acc = acc_ref[hh]
        outs.append(acc[:MLA_V] * (1.0 / acc[MLA_V:MLA_V + 1]))
    y_ref[0] = jnp.concatenate(outs, axis=0).T.astype(BF16)


def _flash(qt, k, vt):
    B, S, _ = k.shape
    assert S % TQ == 0
    pairs = MLA_HEADS // 2
    return pl.pallas_call(
        _flash_kernel,
        grid=(B, pairs, S // TQ),
        in_specs=[pl.BlockSpec((1, 2 * MLA_QK_PAD, TQ), lambda b, p, i: (b, p, i)),
                  pl.BlockSpec((1, S, 2 * MLA_QK_PAD), lambda b, p, i: (b, 0, p)),
                  pl.BlockSpec((1, S // TQ, 2 * MLA_VT_ROWS, TQ), lambda b, p, i: (b, 0, p, 0))],
        out_specs=pl.BlockSpec((1, TQ, 2 * MLA_V), lambda b, p, i: (b, i, p)),
        out_shape=jax.ShapeDtypeStruct((B, S, MLA_HEADS * MLA_V), BF16),
        scratch_shapes=[pltpu.VMEM((2, TQ, TQ), F32),
                        pltpu.VMEM((2, TQ, TQ), F32),
                        pltpu.VMEM((2, 2, 1, TQ), F32),
                        pltpu.VMEM((2, MLA_VT_ROWS, TQ), F32),
                        pltpu.VMEM((2, 1, TQ), F32)],
        compiler_params=_params("arbitrary", "arbitrary", "arbitrary"),
        name="mla_flash",
    )(qt, k, vt)


def _pad_heads(w, heads, d, dp):
    lead = w.shape[:-1]
    w = w.reshape(lead + (heads, d))
    w = jnp.pad(w, [(0, 0)] * len(lead) + [(0, 0), (0, dp - d)])
    return w.reshape(lead + (heads * dp,))


def _pad_head_rows(w, heads, d, dp):
    return _pad_heads(w.T, heads, d, dp).T


def kernel(x, mem, positions, mem_norm, w_mem_kv, norm_mix0, w_in0, b_igate0, b_fgate0, w_conv0, w_hnorm0, w_out0, norm_ffn0, w_ff1_0, w_ff2_0, norm_mix1, w_in1, w_qnorm1, w_uq1, w_kvnorm1, w_ukv1, w_out1, norm_ffn1, w_ff1_1, w_ff2_1, final_norm):
    row = lambda g: g.reshape(1, -1).astype(F32)

    kt, vm = _mem_kv(mem, row(mem_norm), w_mem_kv[:, :MEM_WIDTH].T.astype(BF16),
                     w_mem_kv[:, MEM_WIDTH:].astype(BF16))

    nq = ML_HEADS * ML_QK_DIM
    o_v, o_o, o_g = 2 * nq, 2 * nq + MIX_WIDTH, 2 * nq + 2 * MIX_WIDTH
    o_qm = o_g + 2 * ML_HEADS
    wqk = jnp.concatenate([_pad_heads(w_in0[:, :nq], ML_HEADS, ML_QK_DIM, ML_QK_PAD),
                           _pad_heads(w_in0[:, nq:o_v], ML_HEADS, ML_QK_DIM, ML_QK_PAD)], axis=1)
    wv0 = _pad_heads(w_in0[:, o_v:o_o], ML_HEADS, ML_V_DIM, ML_V_PAD)
    wo0 = _pad_heads(w_in0[:, o_o:o_g], ML_HEADS, ML_V_DIM, ML_V_PAD)
    wg0 = jnp.pad(w_in0[:, o_g:o_qm], ((0, 0), (0, LANES - 2 * ML_HEADS)))
    wqm0 = w_in0[:, o_qm:]
    wconv = jnp.concatenate([_pad_heads(w_conv0[:, :nq], ML_HEADS, ML_QK_DIM, ML_QK_PAD),
                             _pad_heads(w_conv0[:, nq:], ML_HEADS, ML_QK_DIM, ML_QK_PAD)], axis=1).astype(F32)
    gate_bias = jnp.pad(jnp.concatenate([b_igate0, b_fgate0]), (0, LANES - 2 * ML_HEADS)).reshape(1, LANES).astype(F32)
    hnorm = _pad_heads(w_hnorm0.reshape(1, -1), ML_HEADS, ML_V_DIM, ML_V_PAD).astype(F32)
    wa0 = _pad_head_rows(w_out0[:MIX_WIDTH], ML_HEADS, ML_V_DIM, ML_V_PAD).astype(BF16)
    wb0 = w_out0[MIX_WIDTH:].astype(BF16)

    qk, v0, o0, qm0, gates = _inproj0(x, row(norm_mix0), wqk.astype(BF16), wv0.astype(BF16),
                                      wo0.astype(BF16), wqm0.astype(BF16), wg0.astype(BF16))
    y0 = _mlstm(qk, v0, o0, gates, wconv, gate_bias, hnorm)
    x = _outproj(x, y0, qm0, kt, vm, wa0, wb0)
    x = _ffn(x, row(norm_ffn0), w_ff1_0.astype(BF16), w_ff2_0.astype(BF16), row(final_norm), False)

    o_kr = MLA_Q_RANK + MLA_KV_RANK
    o_qm1 = o_kr + MLA_ROPE
    win1 = jnp.concatenate([w_in1[:, :o_kr], w_in1[:, o_qm1:]], axis=1)
    wkrt = jnp.pad(w_in1[:, o_kr:o_qm1].T, ((0, LANES - MLA_ROPE), (0, 0)))
    wuqt = _pad_heads(w_uq1, MLA_HEADS, MLA_NOPE + MLA_ROPE, MLA_QK_PAD).T
    ukv = w_ukv1.reshape(MLA_KV_RANK, MLA_HEADS, MLA_NOPE + MLA_V)
    wuk = _pad_heads(ukv[:, :, :MLA_NOPE].reshape(MLA_KV_RANK, -1), MLA_HEADS, MLA_NOPE, MLA_QK_PAD)
    wuvt = _pad_heads(ukv[:, :, MLA_NOPE:].reshape(MLA_KV_RANK, -1), MLA_HEADS, MLA_V, MLA_VT_ROWS).T
    place = np.zeros((LANES, MLA_HEADS * MLA_QK_PAD), np.float32)
    for h in range(MLA_HEADS):
        place[np.arange(MLA_ROPE), h * MLA_QK_PAD + MLA_NOPE + np.arange(MLA_ROPE)] = 1.0
    place = jnp.asarray(place, BF16)

    cos_t, sin_t = _rope_tables(positions)
    qt1, k1, vt1, qm1 = _inproj1(x, row(norm_mix1), win1.astype(BF16), wkrt.astype(BF16),
                                 row(w_qnorm1), row(w_kvnorm1), wuqt.astype(BF16), wuk.astype(BF16),
                                 wuvt.astype(BF16), place, cos_t, sin_t)
    y1 = _flash(qt1, k1, vt1)
    x = _outproj(x, y1, qm1, kt, vm, w_out1[:MIX_WIDTH].astype(BF16), w_out1[MIX_WIDTH:].astype(BF16))
    return _ffn(x, row(norm_ffn1), w_ff1_1.astype(BF16), w_ff2_1.astype(BF16), row(final_norm), True)
```

```python
import functools
import math

import numpy as np
import jax
import jax.numpy as jnp
from jax import lax
from jax.experimental import pallas as pl
from jax.experimental.pallas import tpu as pltpu

F32 = jnp.float32
BF16 = jnp.bfloat16
EPS = 1e-6

D_MODEL = 1024
N_MEM = 256
MEM_HEADS = 4
MEM_HEAD_DIM = 64
MEM_WIDTH = MEM_HEADS * MEM_HEAD_DIM
MIX_WIDTH = D_MODEL - MEM_WIDTH

ML_HEADS = 4
ML_V_DIM = MIX_WIDTH // ML_HEADS
ML_QK_DIM = ML_V_DIM // 2
ML_CONV = 4
ML_CHUNK = 128
ML_QK_PAD = 128
ML_V_PAD = 256
ML_ONES_COL = ML_V_DIM

MLA_HEADS = 12
MLA_NOPE = 64
MLA_ROPE = 32
MLA_V = 64
MLA_Q_RANK = 384
MLA_KV_RANK = 256
MLA_QK_PAD = 128
MLA_VT_ROWS = 80
ROPE_THETA = 10000.0
D_FF = 4 * D_MODEL

LANES = 128
VMEM_LIMIT = 56 * 1024 * 1024

TM_PROJ = 512
FF_CHUNK = 1024
TQ = 512

_NT = (((1,), (1,)), ((), ()))


def _params(*sem):
    return pltpu.CompilerParams(dimension_semantics=sem, vmem_limit_bytes=VMEM_LIMIT)


def _rms(x, g):
    return x * lax.rsqrt(jnp.mean(x * x, axis=-1, keepdims=True) + EPS) * g


def _const_spec(shape):
    nd = len(shape)
    return pl.BlockSpec(shape, lambda *_: (0,) * nd)


def _mem_kv_kernel(mem_ref, g_ref, wkt_ref, wv_ref, kt_ref, v_ref):
    xn = _rms(mem_ref[0], g_ref[...]).astype(BF16)
    kt = lax.dot_general(wkt_ref[...], xn, _NT, preferred_element_type=F32)
    v = jnp.dot(xn, wv_ref[...], preferred_element_type=F32)
    kt = kt * (MEM_HEAD_DIM ** -0.5)
    row_head = lax.broadcasted_iota(jnp.int32, kt.shape, 0) // MEM_HEAD_DIM
    col_head = lax.broadcasted_iota(jnp.int32, v.shape, 1) // MEM_HEAD_DIM
    for h in range(MEM_HEADS):
        kt_ref[0, h] = jnp.where(row_head == h, kt, 0.0).astype(BF16)
        v_ref[0, h] = jnp.where(col_head == h, v, 0.0).astype(BF16)


def _mem_kv(mem, g, wkt, wv):
    B = mem.shape[0]
    out = jax.ShapeDtypeStruct((B, MEM_HEADS, MEM_WIDTH, N_MEM), BF16)
    return pl.pallas_call(
        _mem_kv_kernel,
        grid=(B,),
        in_specs=[pl.BlockSpec((1, N_MEM, D_MODEL), lambda b: (b, 0, 0)),
                  _const_spec(g.shape), _const_spec(wkt.shape), _const_spec(wv.shape)],
        out_specs=[pl.BlockSpec((1, MEM_HEADS, MEM_WIDTH, N_MEM), lambda b: (b, 0, 0, 0)),
                   pl.BlockSpec((1, MEM_HEADS, N_MEM, MEM_WIDTH), lambda b: (b, 0, 0, 0))],
        out_shape=[out, out],
        compiler_params=_params("arbitrary"),
        name="mem_kv",
    )(mem, g, wkt, wv)


def _inproj0_kernel(x_ref, g_ref, wqk_ref, wv_ref, wo_ref, wqm_ref, wg_ref,
                    qk_ref, v_ref, o_ref, qm_ref, gate_ref):
    xn = _rms(x_ref[0], g_ref[...]).astype(BF16)
    qk_ref[0] = jnp.dot(xn, wqk_ref[...], preferred_element_type=F32).astype(BF16)
    v = jnp.dot(xn, wv_ref[...], preferred_element_type=F32)
    lane = lax.broadcasted_iota(jnp.int32, v.shape, 1) % ML_V_PAD
    v_ref[0] = jnp.where(lane == ML_ONES_COL, 1.0, v).astype(BF16)
    o_ref[0] = jnp.dot(xn, wo_ref[...], preferred_element_type=F32).astype(BF16)
    qm_ref[0] = jnp.dot(xn, wqm_ref[...], preferred_element_type=F32).astype(BF16)
    gate_ref[0] = jnp.dot(xn, wg_ref[...], preferred_element_type=F32)


def _inproj0(x, g, wqk, wv, wo, wqm, wg):
    B, S, D = x.shape
    tm = min(TM_PROJ, S)
    tok = lambda w: pl.BlockSpec((1, tm, w), lambda b, i: (b, i, 0))
    widths = (wqk.shape[1], wv.shape[1], wo.shape[1], wqm.shape[1], wg.shape[1])
    dtypes = (BF16, BF16, BF16, BF16, F32)
    return pl.pallas_call(
        _inproj0_kernel,
        grid=(B, S // tm),
        in_specs=[tok(D), _const_spec(g.shape)] + [_const_spec(w.shape) for w in (wqk, wv, wo, wqm, wg)],
        out_specs=[tok(w) for w in widths],
        out_shape=[jax.ShapeDtypeStruct((B, S, w), dt) for w, dt in zip(widths, dtypes)],
        compiler_params=_params("arbitrary", "arbitrary"),
        name="inproj0",
    )(x, g, wqk, wv, wo, wqm, wg)


def _split3(x):
    hi = x.astype(BF16)
    r = x - hi.astype(F32)
    mid = r.astype(BF16)
    lo = (r - mid.astype(F32)).astype(BF16)
    return hi, mid, lo


def _mlstm_kernel(qk_ref, v_ref, o_ref, gate_ref, wconv_ref, bias_ref, hnorm_ref,
                  y_ref, c_ref, m_ref, tail_ref):
    L = ML_CHUNK
    HQ = ML_HEADS * ML_QK_PAD

    @pl.when(pl.program_id(1) == 0)
    def _():
        c_ref[...] = jnp.zeros_like(c_ref)
        m_ref[...] = jnp.zeros_like(m_ref)
        tail_ref[...] = jnp.zeros_like(tail_ref)

    x = qk_ref[0].astype(F32)
    tail = tail_ref[...]
    wc = wconv_ref[...]
    row8 = lax.broadcasted_iota(jnp.int32, tail.shape, 0)
    conv = x * wc[ML_CONV - 1:ML_CONV]
    for s in range(1, ML_CONV):
        xs = pltpu.roll(x, s, axis=0)
        head = jnp.where(row8 < s, pltpu.roll(tail, s, axis=0), xs[0:8])
        xs = jnp.concatenate([head, xs[8:]], axis=0)
        conv = conv + xs * wc[ML_CONV - 1 - s:ML_CONV - s]
    tail_ref[...] = x[L - 8:L]
    act = conv * jax.nn.sigmoid(conv)
    q = act[:, :HQ].astype(BF16)
    k32 = act[:, HQ:] * (ML_QK_DIM ** -0.5)
    k = k32.astype(BF16)
    v = v_ref[0]

    gates = gate_ref[0] + bias_ref[...]
    lane = lax.broadcasted_iota(jnp.int32, gates.shape, 1)
    log_f = jnp.minimum(gates, 0.0) - jnp.log1p(jnp.exp(-jnp.abs(gates)))
    log_f = jnp.where((lane >= ML_HEADS) & (lane < 2 * ML_HEADS), log_f, 0.0)
    r_i = lax.broadcasted_iota(jnp.int32, (L, L), 0)
    c_i = lax.broadcasted_iota(jnp.int32, (L, L), 1)
    causal = c_i <= r_i
    tril = jnp.where(causal, 1.0, 0.0).astype(BF16)
    b_cols = sum(jnp.dot(tril, part, preferred_element_type=F32) for part in _split3(log_f))
    b_rows = b_cols.T
    g_rows = gates.T

    hn_all = hnorm_ref[...]
    vlane = lax.broadcasted_iota(jnp.int32, (L, ML_V_PAD), 1)
    for h in range(ML_HEADS):
        b_col = b_cols[:, ML_HEADS + h:ML_HEADS + h + 1]
        b_row = b_rows[ML_HEADS + h:ML_HEADS + h + 1, :]
        i_row = g_rows[h:h + 1, :]
        i_col = gates[:, h:h + 1]
        m_prev = m_ref[h][:, 0:1]
        qh = q[:, h * ML_QK_PAD:(h + 1) * ML_QK_PAD]
        kh = k[:, h * ML_QK_PAD:(h + 1) * ML_QK_PAD]
        vh = v[:, h * ML_V_PAD:(h + 1) * ML_V_PAD]
        c_prev = c_ref[h]

        log_w = jnp.where(causal, b_col - b_row + i_row, -jnp.inf)
        log_inter = b_col + m_prev
        m_t = jnp.maximum(log_inter, jnp.max(log_w, axis=-1, keepdims=True))
        w = jnp.exp(log_w - m_t)
        a_inter = jnp.exp(log_inter - m_t)
        s = lax.dot_general(qh, kh, _NT, preferred_element_type=F32) * w
        num = (jnp.dot(s.astype(BF16), vh, preferred_element_type=F32)
               + a_inter * jnp.dot(qh, c_prev.astype(BF16), preferred_element_type=F32))
        den = num[:, ML_ONES_COL:ML_ONES_COL + 1]
        inv = 1.0 / jnp.maximum(jnp.abs(den), jnp.exp(-m_t))
        hc = jnp.where(vlane < ML_V_DIM, num * inv, 0.0)
        ms = jnp.sum(hc * hc, axis=-1, keepdims=True) * (1.0 / ML_V_DIM)
        hn = hc * lax.rsqrt(ms + EPS) * hn_all[:, h * ML_V_PAD:(h + 1) * ML_V_PAD]
        og = o_ref[0, :, h * ML_V_PAD:(h + 1) * ML_V_PAD].astype(F32)
        y_ref[0, :, h * ML_V_PAD:(h + 1) * ML_V_PAD] = (hn * jax.nn.sigmoid(og)).astype(BF16)

        b_end = b_row[:, L - 1:L]
        m_new = jnp.maximum(b_end + m_prev, jnp.max(b_end - b_row + i_row, axis=-1, keepdims=True))
        u_col = jnp.exp(b_end - b_col + i_col - m_new)
        decay = jnp.exp(b_end + m_prev - m_new)
        uk = (k32[:, h * ML_QK_PAD:(h + 1) * ML_QK_PAD] * u_col).astype(BF16)
        c_ref[h] = decay * c_prev + lax.dot_general(uk, vh, (((0,), (0,)), ((), ())),
                                                    preferred_element_type=F32)
        m_ref[h] = jnp.broadcast_to(m_new, (1, LANES))


def _mlstm(qk, v, o, gates, wconv, bias, hnorm):
    B, S, _ = qk.shape
    L = ML_CHUNK
    tok = lambda w: pl.BlockSpec((1, L, w), lambda b, c: (b, c, 0))
    return pl.pallas_call(
        _mlstm_kernel,
        grid=(B, S // L),
        in_specs=[tok(qk.shape[2]), tok(v.shape[2]), tok(o.shape[2]), tok(gates.shape[2]),
                  _const_spec(wconv.shape), _const_spec(bias.shape), _const_spec(hnorm.shape)],
        out_specs=tok(v.shape[2]),
        out_shape=jax.ShapeDtypeStruct(v.shape, BF16),
        scratch_shapes=[pltpu.VMEM((ML_HEADS, ML_QK_PAD, ML_V_PAD), F32),
                        pltpu.VMEM((ML_HEADS, 1, LANES), F32),
                        pltpu.VMEM((8, qk.shape[2]), F32)],
        compiler_params=_params("arbitrary", "arbitrary"),
        name="mlstm",
    )(qk, v, o, gates, wconv, bias, hnorm)


def _outproj_kernel(x_ref, y_ref, qm_ref, kt_ref, vm_ref, wa_ref, wb_ref, out_ref):
    qm = qm_ref[0]
    ymem = None
    for h in range(MEM_HEADS):
        s = jnp.dot(qm, kt_ref[0, h], preferred_element_type=F32)
        e = jnp.exp(s - jnp.max(s, axis=-1, keepdims=True))
        p = (e * (1.0 / jnp.sum(e, axis=-1, keepdims=True))).astype(BF16)
        oh = jnp.dot(p, vm_ref[0, h], preferred_element_type=F32)
        ymem = oh if ymem is None else ymem + oh
    out = x_ref[0] + jnp.dot(y_ref[0], wa_ref[...], preferred_element_type=F32)
    out_ref[0] = out + jnp.dot(ymem.astype(BF16), wb_ref[...], preferred_element_type=F32)


def _outproj(x, y, qm, kt, vm, wa, wb):
    B, S, D = x.shape
    tm = min(TM_PROJ, S)
    tok = lambda w: pl.BlockSpec((1, tm, w), lambda b, i: (b, i, 0))
    per_b = pl.BlockSpec((1,) + kt.shape[1:], lambda b, i: (b, 0, 0, 0))
    return pl.pallas_call(
        _outproj_kernel,
        grid=(B, S // tm),
        in_specs=[tok(D), tok(y.shape[2]), tok(qm.shape[2]), per_b, per_b,
                  _const_spec(wa.shape), _const_spec(wb.shape)],
        out_specs=tok(D),
        out_shape=jax.ShapeDtypeStruct(x.shape, F32),
        compiler_params=_params("arbitrary", "arbitrary"),
        name="outproj",
    )(x, y, qm, kt, vm, wa, wb)


def _ffn_kernel(x_ref, g_ref, w1_ref, w2_ref, gf_ref, out_ref, *, final_norm):
    x = x_ref[0]
    hn = _rms(x, g_ref[...]).astype(BF16)
    acc = x
    for c in range(D_FF // FF_CHUNK):
        a = jnp.dot(hn, w1_ref[:, c * FF_CHUNK:(c + 1) * FF_CHUNK], preferred_element_type=F32)
        a = jnp.square(jnp.maximum(a, 0.0)).astype(BF16)
        acc = acc + jnp.dot(a, w2_ref[c * FF_CHUNK:(c + 1) * FF_CHUNK, :], preferred_element_type=F32)
    if final_norm:
        acc = _rms(acc, gf_ref[...])
    out_ref[0] = acc


def _ffn(x, g, w1, w2, gf, final_norm):
    B, S, D = x.shape
    tm = min(TM_PROJ, S)
    tok = pl.BlockSpec((1, tm, D), lambda b, i: (b, i, 0))
    return pl.pallas_call(
        functools.partial(_ffn_kernel, final_norm=final_norm),
        grid=(B, S // tm),
        in_specs=[tok, _const_spec(g.shape), _const_spec(w1.shape), _const_spec(w2.shape),
                  _const_spec(gf.shape)],
        out_specs=tok,
        out_shape=jax.ShapeDtypeStruct(x.shape, F32),
        compiler_params=_params("arbitrary", "arbitrary"),
        name="ffn",
    )(x, g, w1, w2, gf)


def _rope_kernel(pos_ref, inv_ref, cos_ref, sin_ref):
    ang = pos_ref[0].astype(F32) * inv_ref[...]
    cos_ref[0] = jnp.cos(ang)
    sin_ref[0] = jnp.sin(ang)


def _rope_tables(positions):
    B, S = positions.shape
    nf = MLA_ROPE // 2
    inv = (ROPE_THETA ** (-jnp.arange(0, MLA_ROPE, 2, dtype=F32) / MLA_ROPE)).reshape(nf, 1)
    out = jax.ShapeDtypeStruct((B, nf, S), F32)
    spec = pl.BlockSpec((1, nf, S), lambda b: (b, 0, 0))
    return pl.pallas_call(
        _rope_kernel,
        grid=(B,),
        in_specs=[pl.BlockSpec((1, 1, S), lambda b: (b, 0, 0)), _const_spec(inv.shape)],
        out_specs=[spec, spec],
        out_shape=[out, out],
        compiler_params=_params("arbitrary"),
        name="rope_tables",
    )(positions.reshape(B, 1, S), inv)


def _inproj1_kernel(x_ref, g_ref, win_ref, wkrt_ref, gq_ref, gkv_ref, wuqt_ref, wuk_ref, wuvt_ref,
                    place_ref, cos_ref, sin_ref, qt_ref, k_ref, vt_ref, qm_ref):
    half = MLA_ROPE // 2
    xn = _rms(x_ref[0], g_ref[...]).astype(BF16)
    c = jnp.dot(xn, win_ref[...], preferred_element_type=F32)
    o_kv = MLA_Q_RANK
    o_qm = o_kv + MLA_KV_RANK
    qm_ref[0] = c[:, o_qm:].astype(BF16)
    cq = _rms(c[:, :o_kv], gq_ref[...]).astype(BF16)
    ckv = _rms(c[:, o_kv:o_qm], gkv_ref[...]).astype(BF16)
    cos_t, sin_t = cos_ref[0], sin_ref[0]

    def rope_t(x1, x2):
        return x1 * cos_t - x2 * sin_t, x2 * cos_t + x1 * sin_t

    qscale = ((MLA_NOPE + MLA_ROPE) ** -0.5) * math.log2(math.e)
    qt = lax.dot_general(wuqt_ref[...], cq, _NT, preferred_element_type=F32)
    zpad = jnp.zeros((MLA_QK_PAD - MLA_NOPE - MLA_ROPE, qt.shape[1]), F32)
    for h in range(MLA_HEADS):
        b0 = h * MLA_QK_PAD
        r1, r2 = rope_t(qt[b0 + MLA_NOPE:b0 + MLA_NOPE + half],
                        qt[b0 + MLA_NOPE + half:b0 + MLA_NOPE + MLA_ROPE])
        blk = jnp.concatenate([qt[b0:b0 + MLA_NOPE], r1, r2, zpad], axis=0) * qscale
        qt_ref[0, b0:b0 + MLA_QK_PAD] = blk.astype(BF16)

    krt = lax.dot_general(wkrt_ref[...], xn, _NT, preferred_element_type=F32)
    r1, r2 = rope_t(krt[:half], krt[half:MLA_ROPE])
    kr = jnp.concatenate([r1, r2, krt[MLA_ROPE:]], axis=0).T.astype(BF16)
    k = (jnp.dot(ckv, wuk_ref[...], preferred_element_type=F32)
         + jnp.dot(kr, place_ref[...], preferred_element_type=F32))
    k_ref[0] = k.astype(BF16)

    vt = lax.dot_general(wuvt_ref[...], ckv, _NT, preferred_element_type=F32)
    rowid = lax.broadcasted_iota(jnp.int32, vt.shape, 0) % MLA_VT_ROWS
    vt_ref[0, 0] = jnp.where(rowid == MLA_V, 1.0, vt).astype(BF16)


def _inproj1(x, g, win, wkrt, gq, gkv, wuqt, wuk, wuvt, place, cos_t, sin_t):
    B, S, D = x.shape
    tm = TQ
    assert S % tm == 0
    tok = lambda w: pl.BlockSpec((1, tm, w), lambda b, i: (b, i, 0))
    tok_t = lambda r: pl.BlockSpec((1, r, tm), lambda b, i: (b, 0, i))
    consts = (g, win, wkrt, gq, gkv, wuqt, wuk, wuvt, place)
    nvt = wuvt.shape[0]
    return pl.pallas_call(
        _inproj1_kernel,
        grid=(B, S // tm),
        in_specs=[tok(D)] + [_const_spec(a.shape) for a in consts] + [tok_t(cos_t.shape[1])] * 2,
        out_specs=[tok_t(wuqt.shape[0]), tok(wuk.shape[1]),
                   pl.BlockSpec((1, 1, nvt, tm), lambda b, i: (b, i, 0, 0)), tok(MEM_WIDTH)],
        out_shape=[jax.ShapeDtypeStruct((B, wuqt.shape[0], S), BF16),
                   jax.ShapeDtypeStruct((B, S, wuk.shape[1]), BF16),
                   jax.ShapeDtypeStruct((B, S // tm, nvt, tm), BF16),
                   jax.ShapeDtypeStruct((B, S, MEM_WIDTH), BF16)],
        compiler_params=_params("arbitrary", "arbitrary"),
        name="inproj1",
    )(x, *consts, cos_t, sin_t)


def _flash_kernel(qt_ref, k_ref, vt_ref, y_ref, sa_ref, sb_ref, smax_ref, acc_ref, m_ref):
    i = pl.program_id(2)
    slots = (sa_ref, sb_ref)

    def scores_t(hh, blk):
        kh = k_ref[0, pl.ds(pl.multiple_of(blk * TQ, TQ), TQ), hh * MLA_QK_PAD:(hh + 1) * MLA_QK_PAD]
        qh = qt_ref[0, hh * MLA_QK_PAD:(hh + 1) * MLA_QK_PAD, :]
        return jnp.dot(kh, qh, preferred_element_type=F32)

    def v_t(hh, blk):
        return vt_ref[0, blk, hh * MLA_VT_ROWS:(hh + 1) * MLA_VT_ROWS, :]

    def col_max(s):
        parts = [jnp.max(s[r:r + 128], axis=0, keepdims=True) for r in range(0, TQ, 128)]
        return jnp.maximum(jnp.maximum(parts[0], parts[1]), jnp.maximum(parts[2], parts[3]))

    key_i = lax.broadcasted_iota(jnp.int32, (TQ, TQ), 0)
    qry_i = lax.broadcasted_iota(jnp.int32, (TQ, TQ), 1)

    def qk_stage(slot, blk, diagonal=False):
        for hh in range(2):
            s = scores_t(hh, blk)
            if diagonal:
                s = jnp.where(key_i <= qry_i, s, -jnp.inf)
            slots[slot][hh] = s
            smax_ref[slot, hh] = col_max(s)

    def pv_stage(slot, blk):
        for hh in range(2):
            m_prev = m_ref[hh]
            m_new = jnp.maximum(m_prev, smax_ref[slot, hh])
            alpha = jnp.exp2(m_prev - m_new)
            p = jnp.exp2(slots[slot][hh] - m_new).astype(BF16)
            acc_ref[hh] = alpha * acc_ref[hh] + jnp.dot(v_t(hh, blk), p, preferred_element_type=F32)
            m_ref[hh] = m_new

    m_ref[...] = jnp.full(m_ref.shape, -jnp.inf, F32)
    acc_ref[...] = jnp.zeros_like(acc_ref)

    @pl.when(i == 0)
    def _():
        qk_stage(0, 0, diagonal=True)
        pv_stage(0, 0)

    @pl.when(i >= 1)
    def _():
        qk_stage(0, 0)

    def body(t, carry):
        qk_stage(1, 2 * t + 1)
        pv_stage(0, 2 * t)
        qk_stage(0, 2 * t + 2)
        pv_stage(1, 2 * t + 1)
        return carry

    lax.fori_loop(0, (i - 1) // 2, body, 0)

    @pl.when(i % 2 == 1)
    def _():
        qk_stage(1, i, diagonal=True)
        pv_stage(0, i - 1)
        pv_stage(1, i)

    @pl.when((i % 2 == 0) & (i >= 2))
    def _():
        qk_stage(1, i - 1)
        pv_stage(0, i - 2)
        qk_stage(0, i, diagonal=True)
        pv_stage(1, i - 1)
        pv_stage(0, i)

    outs = []
    for hh in range(2):
        acc = acc_ref[hh]
        outs.append(acc[:MLA_V] * (1.0 / acc[MLA_V:MLA_V + 1]))
    y_ref[0] = jnp.concatenate(outs, axis=0).T.astype(BF16)


def _flash(qt, k, vt):
    B, S, _ = k.shape
    assert S % TQ == 0
    pairs = MLA_HEADS // 2
    return pl.pallas_call(
        _flash_kernel,
        grid=(B, pairs, S // TQ),
        in_specs=[pl.BlockSpec((1, 2 * MLA_QK_PAD, TQ), lambda b, p, i: (b, p, i)),
                  pl.BlockSpec((1, S, 2 * MLA_QK_PAD), lambda b, p, i: (b, 0, p)),
                  pl.BlockSpec((1, S // TQ, 2 * MLA_VT_ROWS, TQ), lambda b, p, i: (b, 0, p, 0))],
        out_specs=pl.BlockSpec((1, TQ, 2 * MLA_V), lambda b, p, i: (b, i, p)),
        out_shape=jax.ShapeDtypeStruct((B, S, MLA_HEADS * MLA_V), BF16),
        scratch_shapes=[pltpu.VMEM((2, TQ, TQ), F32),
                        pltpu.VMEM((2, TQ, TQ), F32),
                        pltpu.VMEM((2, 2, 1, TQ), F32),
                        pltpu.VMEM((2, MLA_VT_ROWS, TQ), F32),
                        pltpu.VMEM((2, 1, TQ), F32)],
        compiler_params=_params("arbitrary", "arbitrary", "arbitrary"),
        name="mla_flash",
    )(qt, k, vt)


def _pad_heads(w, heads, d, dp):
    lead = w.shape[:-1]
    w = w.reshape(lead + (heads, d))
    w = jnp.pad(w, [(0, 0)] * len(lead) + [(0, 0), (0, dp - d)])
    return w.reshape(lead + (heads * dp,))


def _pad_head_rows(w, heads, d, dp):
    return _pad_heads(w.T, heads, d, dp).T


def kernel(x, mem, positions, mem_norm, w_mem_kv, norm_mix0, w_in0, b_igate0, b_fgate0, w_conv0, w_hnorm0, w_out0, norm_ffn0, w_ff1_0, w_ff2_0, norm_mix1, w_in1, w_qnorm1, w_uq1, w_kvnorm1, w_ukv1, w_out1, norm_ffn1, w_ff1_1, w_ff2_1, final_norm):
    row = lambda g: g.reshape(1, -1).astype(F32)

    kt, vm = _mem_kv(mem, row(mem_norm), w_mem_kv[:, :MEM_WIDTH].T.astype(BF16),
                     w_mem_kv[:, MEM_WIDTH:].astype(BF16))

    nq = ML_HEADS * ML_QK_DIM
    o_v, o_o, o_g = 2 * nq, 2 * nq + MIX_WIDTH, 2 * nq + 2 * MIX_WIDTH
    o_qm = o_g + 2 * ML_HEADS
    wqk = jnp.concatenate([_pad_heads(w_in0[:, :nq], ML_HEADS, ML_QK_DIM, ML_QK_PAD),
                           _pad_heads(w_in0[:, nq:o_v], ML_HEADS, ML_QK_DIM, ML_QK_PAD)], axis=1)
    wv0 = _pad_heads(w_in0[:, o_v:o_o], ML_HEADS, ML_V_DIM, ML_V_PAD)
    wo0 = _pad_heads(w_in0[:, o_o:o_g], ML_HEADS, ML_V_DIM, ML_V_PAD)
    wg0 = jnp.pad(w_in0[:, o_g:o_qm], ((0, 0), (0, LANES - 2 * ML_HEADS)))
    wqm0 = w_in0[:, o_qm:]
    wconv = jnp.concatenate([_pad_heads(w_conv0[:, :nq], ML_HEADS, ML_QK_DIM, ML_QK_PAD),
                             _pad_heads(w_conv0[:, nq:], ML_HEADS, ML_QK_DIM, ML_QK_PAD)], axis=1).astype(F32)
    gate_bias = jnp.pad(jnp.concatenate([b_igate0, b_fgate0]), (0, LANES - 2 * ML_HEADS)).reshape(1, LANES).astype(F32)
    hnorm = _pad_heads(w_hnorm0.reshape(1, -1), ML_HEADS, ML_V_DIM, ML_V_PAD).astype(F32)
    wa0 = _pad_head_rows(w_out0[:MIX_WIDTH], ML_HEADS, ML_V_DIM, ML_V_PAD).astype(BF16)
    wb0 = w_out0[MIX_WIDTH:].astype(BF16)

    qk, v0, o0, qm0, gates = _inproj0(x, row(norm_mix0), wqk.astype(BF16), wv0.astype(BF16),
                                      wo0.astype(BF16), wqm0.astype(BF16), wg0.astype(BF16))
    y0 = _mlstm(qk, v0, o0, gates, wconv, gate_bias, hnorm)
    x = _outproj(x, y0, qm0, kt, vm, wa0, wb0)
    x = _ffn(x, row(norm_ffn0), w_ff1_0.astype(BF16), w_ff2_0.astype(BF16), row(final_norm), False)

    o_kr = MLA_Q_RANK + MLA_KV_RANK
    o_qm1 = o_kr + MLA_ROPE
    win1 = jnp.concatenate([w_in1[:, :o_kr], w_in1[:, o_qm1:]], axis=1)
    wkrt = jnp.pad(w_in1[:, o_kr:o_qm1].T, ((0, LANES - MLA_ROPE), (0, 0)))
    wuqt = _pad_heads(w_uq1, MLA_HEADS, MLA_NOPE + MLA_ROPE, MLA_QK_PAD).T
    ukv = w_ukv1.reshape(MLA_KV_RANK, MLA_HEADS, MLA_NOPE + MLA_V)
    wuk = _pad_heads(ukv[:, :, :MLA_NOPE].reshape(MLA_KV_RANK, -1), MLA_HEADS, MLA_NOPE, MLA_QK_PAD)
    wuvt = _pad_heads(ukv[:, :, MLA_NOPE:].reshape(MLA_KV_RANK, -1), MLA_HEADS, MLA_V, MLA_VT_ROWS).T
    place = np.zeros((LANES, MLA_HEADS * MLA_QK_PAD), np.float32)
    for h in range(MLA_HEADS):
        place[np.arange(MLA_ROPE), h * MLA_QK_PAD + MLA_NOPE + np.arange(MLA_ROPE)] = 1.0
    place = jnp.asarray(place, BF16)

    cos_t, sin_t = _rope_tables(positions)
    qt1, k1, vt1, qm1 = _inproj1(x, row(norm_mix1), win1.astype(BF16), wkrt.astype(BF16),
                                 row(w_qnorm1), row(w_kvnorm1), wuqt.astype(BF16), wuk.astype(BF16),
                                 wuvt.astype(BF16), place, cos_t, sin_t)
    y1 = _flash(qt1, k1, vt1)
    x = _outproj(x, y1, qm1, kt, vm, w_out1[:MIX_WIDTH].astype(BF16), w_out1[MIX_WIDTH:].astype(BF16))
    return _ffn(x, row(norm_ffn1), w_ff1_1.astype(BF16), w_ff2_1.astype(BF16), row(final_norm), True)
```

```python
import functools
import math

import numpy as np
import jax
import jax.numpy as jnp
from jax import lax
from jax.experimental import pallas as pl
from jax.experimental.pallas import tpu as pltpu

F32 = jnp.float32
BF16 = jnp.bfloat16
EPS = 1e-6

D_MODEL = 1024
N_MEM = 256
MEM_HEADS = 4
MEM_HEAD_DIM = 64
MEM_WIDTH = MEM_HEADS * MEM_HEAD_DIM
MIX_WIDTH = D_MODEL - MEM_WIDTH

ML_HEADS = 4
ML_V_DIM = MIX_WIDTH // ML_HEADS
ML_QK_DIM = ML_V_DIM // 2
ML_CONV = 4
ML_CHUNK = 128
ML_QK_PAD = 128
ML_V_PAD = 256
ML_ONES_ROW = ML_V_DIM

MLA_HEADS = 12
MLA_NOPE = 64
MLA_ROPE = 32
MLA_V = 64
MLA_Q_RANK = 384
MLA_KV_RANK = 256
MLA_QK_PAD = 128
MLA_VT_ROWS = 80
ROPE_THETA = 10000.0
D_FF = 4 * D_MODEL

LANES = 128
VMEM_LIMIT = 56 * 1024 * 1024

TM_PROJ = 512
FF_CHUNK = 1024
TQ = 512

_NT = (((1,), (1,)), ((), ()))


def _params(*sem):
    return pltpu.CompilerParams(dimension_semantics=sem, vmem_limit_bytes=VMEM_LIMIT)


def _rms(x, g):
    return x * lax.rsqrt(jnp.mean(x * x, axis=-1, keepdims=True) + EPS) * g


def _const_spec(shape):
    nd = len(shape)
    return pl.BlockSpec(shape, lambda *_: (0,) * nd)


def _mem_kv_kernel(mem_ref, g_ref, wkt_ref, wv_ref, kt_ref, v_ref):
    xn = _rms(mem_ref[0], g_ref[...]).astype(BF16)
    kt = lax.dot_general(wkt_ref[...], xn, _NT, preferred_element_type=F32)
    v = jnp.dot(xn, wv_ref[...], preferred_element_type=F32)
    kt = kt * (MEM_HEAD_DIM ** -0.5)
    row_head = lax.broadcasted_iota(jnp.int32, kt.shape, 0) // MEM_HEAD_DIM
    col_head = lax.broadcasted_iota(jnp.int32, v.shape, 1) // MEM_HEAD_DIM
    for h in range(MEM_HEADS):
        kt_ref[0, h] = jnp.where(row_head == h, kt, 0.0).astype(BF16)
        v_ref[0, h] = jnp.where(col_head == h, v, 0.0).astype(BF16)


def _mem_kv(mem, g, wkt, wv):
    B = mem.shape[0]
    out = jax.ShapeDtypeStruct((B, MEM_HEADS, MEM_WIDTH, N_MEM), BF16)
    return pl.pallas_call(
        _mem_kv_kernel,
        grid=(B,),
        in_specs=[pl.BlockSpec((1, N_MEM, D_MODEL), lambda b: (b, 0, 0)),
                  _const_spec(g.shape), _const_spec(wkt.shape), _const_spec(wv.shape)],
        out_specs=[pl.BlockSpec((1, MEM_HEADS, MEM_WIDTH, N_MEM), lambda b: (b, 0, 0, 0)),
                   pl.BlockSpec((1, MEM_HEADS, N_MEM, MEM_WIDTH), lambda b: (b, 0, 0, 0))],
        out_shape=[out, out],
        compiler_params=_params("arbitrary"),
        name="mem_kv",
    )(mem, g, wkt, wv)


def _inproj0_kernel(x_ref, g_ref, wqk_ref, wv_ref, wo_ref, wqm_ref, wg_ref,
                    qk_ref, v_ref, o_ref, qm_ref, gate_ref):
    xn = _rms(x_ref[0], g_ref[...]).astype(BF16)
    qk_ref[0] = jnp.dot(xn, wqk_ref[...], preferred_element_type=F32).astype(BF16)
    vt = lax.dot_general(wv_ref[...], xn, _NT, preferred_element_type=F32)
    rowid = lax.broadcasted_iota(jnp.int32, vt.shape, 0) % ML_V_PAD
    v_ref[0] = jnp.where(rowid == ML_ONES_ROW, 1.0, vt).astype(BF16)
    o_ref[0] = jnp.dot(xn, wo_ref[...], preferred_element_type=F32).astype(BF16)
    qm_ref[0] = jnp.dot(xn, wqm_ref[...], preferred_element_type=F32).astype(BF16)
    gate_ref[0] = jnp.dot(xn, wg_ref[...], preferred_element_type=F32)


def _inproj0(x, g, wqk, wvt, wo, wqm, wg):
    B, S, D = x.shape
    tm = min(TM_PROJ, S)
    tok = lambda w: pl.BlockSpec((1, tm, w), lambda b, i: (b, i, 0))
    tok_t = lambda r: pl.BlockSpec((1, r, tm), lambda b, i: (b, 0, i))
    nv = wvt.shape[0]
    consts = (g, wqk, wvt, wo, wqm, wg)
    return pl.pallas_call(
        _inproj0_kernel,
        grid=(B, S // tm),
        in_specs=[tok(D)] + [_const_spec(a.shape) for a in consts],
        out_specs=[tok(wqk.shape[1]), tok_t(nv), tok(wo.shape[1]), tok(wqm.shape[1]), tok(wg.shape[1])],
        out_shape=[jax.ShapeDtypeStruct((B, S, wqk.shape[1]), BF16),
                   jax.ShapeDtypeStruct((B, nv, S), BF16),
                   jax.ShapeDtypeStruct((B, S, wo.shape[1]), BF16),
                   jax.ShapeDtypeStruct((B, S, wqm.shape[1]), BF16),
                   jax.ShapeDtypeStruct((B, S, wg.shape[1]), F32)],
        compiler_params=_params("arbitrary", "arbitrary"),
        name="inproj0",
    )(x, *consts)


def _split3(x):
    hi = x.astype(BF16)
    r = x - hi.astype(F32)
    mid = r.astype(BF16)
    lo = (r - mid.astype(F32)).astype(BF16)
    return hi, mid, lo


def _mlstm_kernel(qk_ref, vt_ref, o_ref, gate_ref, wconv_ref, bias_ref, hnorm_ref,
                  y_ref, ct_ref, m_ref, tail_ref):
    L = ML_CHUNK
    HQ = ML_HEADS * ML_QK_PAD

    @pl.when(pl.program_id(1) == 0)
    def _():
        ct_ref[...] = jnp.zeros_like(ct_ref)
        m_ref[...] = jnp.zeros_like(m_ref)
        tail_ref[...] = jnp.zeros_like(tail_ref)

    x = qk_ref[0].astype(F32)
    tail = tail_ref[...]
    wc = wconv_ref[...]
    row8 = lax.broadcasted_iota(jnp.int32, tail.shape, 0)
    conv = x * wc[ML_CONV - 1:ML_CONV]
    for s in range(1, ML_CONV):
        xs = pltpu.roll(x, s, axis=0)
        head = jnp.where(row8 < s, pltpu.roll(tail, s, axis=0), xs[0:8])
        xs = jnp.concatenate([head, xs[8:]], axis=0)
        conv = conv + xs * wc[ML_CONV - 1 - s:ML_CONV - s]
    tail_ref[...] = x[L - 8:L]
    act = conv * jax.nn.sigmoid(conv)
    q = act[:, :HQ].astype(BF16)
    k = (act[:, HQ:] * (ML_QK_DIM ** -0.5)).astype(BF16)

    gates = gate_ref[0] + bias_ref[...]
    lane = lax.broadcasted_iota(jnp.int32, gates.shape, 1)
    log_f = jnp.minimum(gates, 0.0) - jnp.log1p(jnp.exp(-jnp.abs(gates)))
    log_f = jnp.where((lane >= ML_HEADS) & (lane < 2 * ML_HEADS), log_f, 0.0)
    key_i = lax.broadcasted_iota(jnp.int32, (L, L), 0)
    qry_i = lax.broadcasted_iota(jnp.int32, (L, L), 1)
    tril = jnp.where(qry_i <= key_i, 1.0, 0.0).astype(BF16)
    causal_t = key_i <= qry_i
    b_cols = sum(jnp.dot(tril, part, preferred_element_type=F32) for part in _split3(log_f))
    b_rows = b_cols.T
    g_rows = gates.T

    for h in range(ML_HEADS):
        b_row = b_rows[ML_HEADS + h:ML_HEADS + h + 1, :]
        c_row = g_rows[h:h + 1, :] - b_row
        c_col = gates[:, h:h + 1] - b_cols[:, ML_HEADS + h:ML_HEADS + h + 1]
        m_prev = m_ref[h][:, 0:1]
        qh = q[:, h * ML_QK_PAD:(h + 1) * ML_QK_PAD]
        kh = k[:, h * ML_QK_PAD:(h + 1) * ML_QK_PAD]
        vth = vt_ref[0, h * ML_V_PAD:(h + 1) * ML_V_PAD, :]
        ct_prev = ct_ref[h]

        cm = jnp.where(causal_t, c_col, -jnp.inf)
        m_row = jnp.maximum(m_prev, jnp.max(cm, axis=0, keepdims=True))
        w_t = jnp.exp(cm - m_row)
        a_row = jnp.exp(m_prev - m_row)
        s_t = (lax.dot_general(kh, qh, _NT, preferred_element_type=F32) * w_t).astype(BF16)
        num_t = (jnp.dot(vth, s_t, preferred_element_type=F32)
                 + a_row * lax.dot_general(ct_prev.astype(BF16), qh, _NT, preferred_element_type=F32))
        den = num_t[ML_ONES_ROW:ML_ONES_ROW + 1]
        inv = 1.0 / jnp.maximum(jnp.abs(den), jnp.exp(-(b_row + m_row)))
        hc_t = num_t[:ML_V_DIM] * inv
        ms = jnp.sum(hc_t * hc_t, axis=0, keepdims=True) * (1.0 / ML_V_DIM)
        hn_t = hc_t * lax.rsqrt(ms + EPS) * hnorm_ref[h]
        hn = jnp.concatenate([hn_t, jnp.zeros((ML_V_PAD - ML_V_DIM, L), F32)], axis=0).T
        og = o_ref[0, :, h * ML_V_PAD:(h + 1) * ML_V_PAD].astype(F32)
        y_ref[0, :, h * ML_V_PAD:(h + 1) * ML_V_PAD] = (hn * jax.nn.sigmoid(og)).astype(BF16)

        m_end = m_row[:, L - 1:L]
        u_row = jnp.exp(c_row - m_end)
        decay = jnp.exp(m_prev - m_end)
        uv_t = (vth.astype(F32) * u_row).astype(BF16)
        ct_ref[h] = decay * ct_prev + jnp.dot(uv_t, kh, preferred_element_type=F32)
        m_ref[h] = jnp.broadcast_to(b_row[:, L - 1:L] + m_end, (1, LANES))


def _mlstm(qk, vt, o, gates, wconv, bias, hnorm):
    B, S, _ = qk.shape
    L = ML_CHUNK
    tok = lambda w: pl.BlockSpec((1, L, w), lambda b, c: (b, c, 0))
    return pl.pallas_call(
        _mlstm_kernel,
        grid=(B, S // L),
        in_specs=[tok(qk.shape[2]), pl.BlockSpec((1, vt.shape[1], L), lambda b, c: (b, 0, c)),
                  tok(o.shape[2]), tok(gates.shape[2]),
                  _const_spec(wconv.shape), _const_spec(bias.shape), _const_spec(hnorm.shape)],
        out_specs=tok(o.shape[2]),
        out_shape=jax.ShapeDtypeStruct(o.shape, BF16),
        scratch_shapes=[pltpu.VMEM((ML_HEADS, ML_V_PAD, ML_QK_PAD), F32),
                        pltpu.VMEM((ML_HEADS, 1, LANES), F32),
                        pltpu.VMEM((8, qk.shape[2]), F32)],
        compiler_params=_params("arbitrary", "arbitrary"),
        name="mlstm",
    )(qk, vt, o, gates, wconv, bias, hnorm)


def _outproj_kernel(x_ref, y_ref, qm_ref, kt_ref, vm_ref, wa_ref, wb_ref, out_ref):
    qm = qm_ref[0]
    ymem = None
    for h in range(MEM_HEADS):
        s = jnp.dot(qm, kt_ref[0, h], preferred_element_type=F32)
        e = jnp.exp(s - jnp.max(s, axis=-1, keepdims=True))
        p = (e * (1.0 / jnp.sum(e, axis=-1, keepdims=True))).astype(BF16)
        oh = jnp.dot(p, vm_ref[0, h], preferred_element_type=F32)
        ymem = oh if ymem is None else ymem + oh
    out = x_ref[0] + jnp.dot(y_ref[0], wa_ref[...], preferred_element_type=F32)
    out_ref[0] = out + jnp.dot(ymem.astype(BF16), wb_ref[...], preferred_element_type=F32)


def _outproj(x, y, qm, kt, vm, wa, wb):
    B, S, D = x.shape
    tm = min(TM_PROJ, S)
    tok = lambda w: pl.BlockSpec((1, tm, w), lambda b, i: (b, i, 0))
    per_b = pl.BlockSpec((1,) + kt.shape[1:], lambda b, i: (b, 0, 0, 0))
    return pl.pallas_call(
        _outproj_kernel,
        grid=(B, S // tm),
        in_specs=[tok(D), tok(y.shape[2]), tok(qm.shape[2]), per_b, per_b,
                  _const_spec(wa.shape), _const_spec(wb.shape)],
        out_specs=tok(D),
        out_shape=jax.ShapeDtypeStruct(x.shape, F32),
        compiler_params=_params("arbitrary", "arbitrary"),
        name="outproj",
    )(x, y, qm, kt, vm, wa, wb)


def _ffn_kernel(x_ref, g_ref, w1_ref, w2_ref, gf_ref, out_ref, *, final_norm):
    x = x_ref[0]
    hn = _rms(x, g_ref[...]).astype(BF16)
    acc = x
    for c in range(D_FF // FF_CHUNK):
        a = jnp.dot(hn, w1_ref[:, c * FF_CHUNK:(c + 1) * FF_CHUNK], preferred_element_type=F32)
        a = jnp.square(jnp.maximum(a, 0.0)).astype(BF16)
        acc = acc + jnp.dot(a, w2_ref[c * FF_CHUNK:(c + 1) * FF_CHUNK, :], preferred_element_type=F32)
    if final_norm:
        acc = _rms(acc, gf_ref[...])
    out_ref[0] = acc


def _ffn(x, g, w1, w2, gf, final_norm):
    B, S, D = x.shape
    tm = min(TM_PROJ, S)
    tok = pl.BlockSpec((1, tm, D), lambda b, i: (b, i, 0))
    return pl.pallas_call(
        functools.partial(_ffn_kernel, final_norm=final_norm),
        grid=(B, S // tm),
        in_specs=[tok, _const_spec(g.shape), _const_spec(w1.shape), _const_spec(w2.shape),
                  _const_spec(gf.shape)],
        out_specs=tok,
        out_shape=jax.ShapeDtypeStruct(x.shape, F32),
        compiler_params=_params("arbitrary", "arbitrary"),
        name="ffn",
    )(x, g, w1, w2, gf)


def _rope_kernel(pos_ref, inv_ref, cos_ref, sin_ref):
    ang = pos_ref[0].astype(F32) * inv_ref[...]
    cos_ref[0] = jnp.cos(ang)
    sin_ref[0] = jnp.sin(ang)


def _rope_tables(positions):
    B, S = positions.shape
    nf = MLA_ROPE // 2
    inv = (ROPE_THETA ** (-jnp.arange(0, MLA_ROPE, 2, dtype=F32) / MLA_ROPE)).reshape(nf, 1)
    out = jax.ShapeDtypeStruct((B, nf, S), F32)
    spec = pl.BlockSpec((1, nf, S), lambda b: (b, 0, 0))
    return pl.pallas_call(
        _rope_kernel,
        grid=(B,),
        in_specs=[pl.BlockSpec((1, 1, S), lambda b: (b, 0, 0)), _const_spec(inv.shape)],
        out_specs=[spec, spec],
        out_shape=[out, out],
        compiler_params=_params("arbitrary"),
        name="rope_tables",
    )(positions.reshape(B, 1, S), inv)


def _inproj1_kernel(x_ref, g_ref, win_ref, wkrt_ref, gq_ref, gkv_ref, wuqt_ref, wuk_ref, wuvt_ref,
                    place_ref, cos_ref, sin_ref, qt_ref, k_ref, vt_ref, qm_ref):
    half = MLA_ROPE // 2
    xn = _rms(x_ref[0], g_ref[...]).astype(BF16)
    c = jnp.dot(xn, win_ref[...], preferred_element_type=F32)
    o_kv = MLA_Q_RANK
    o_qm = o_kv + MLA_KV_RANK
    qm_ref[0] = c[:, o_qm:].astype(BF16)
    cq = _rms(c[:, :o_kv], gq_ref[...]).astype(BF16)
    ckv = _rms(c[:, o_kv:o_qm], gkv_ref[...]).astype(BF16)
    cos_t, sin_t = cos_ref[0], sin_ref[0]

    def rope_t(x1, x2):
        return x1 * cos_t - x2 * sin_t, x2 * cos_t + x1 * sin_t

    qscale = ((MLA_NOPE + MLA_ROPE) ** -0.5) * math.log2(math.e)
    qt = lax.dot_general(wuqt_ref[...], cq, _NT, preferred_element_type=F32)
    zpad = jnp.zeros((MLA_QK_PAD - MLA_NOPE - MLA_ROPE, qt.shape[1]), F32)
    for h in range(MLA_HEADS):
        b0 = h * MLA_QK_PAD
        r1, r2 = rope_t(qt[b0 + MLA_NOPE:b0 + MLA_NOPE + half],
                        qt[b0 + MLA_NOPE + half:b0 + MLA_NOPE + MLA_ROPE])
        blk = jnp.concatenate([qt[b0:b0 + MLA_NOPE], r1, r2, zpad], axis=0) * qscale
        qt_ref[0, b0:b0 + MLA_QK_PAD] = blk.astype(BF16)

    krt = lax.dot_general(wkrt_ref[...], xn, _NT, preferred_element_type=F32)
    r1, r2 = rope_t(krt[:half], krt[half:MLA_ROPE])
    kr = jnp.concatenate([r1, r2, krt[MLA_ROPE:]], axis=0).T.astype(BF16)
    k = (jnp.dot(ckv, wuk_ref[...], preferred_element_type=F32)
         + jnp.dot(kr, place_ref[...], preferred_element_type=F32))
    k_ref[0] = k.astype(BF16)

    vt = lax.dot_general(wuvt_ref[...], ckv, _NT, preferred_element_type=F32)
    rowid = lax.broadcasted_iota(jnp.int32, vt.shape, 0) % MLA_VT_ROWS
    vt_ref[0, 0] = jnp.where(rowid == MLA_V, 1.0, vt).astype(BF16)


def _inproj1(x, g, win, wkrt, gq, gkv, wuqt, wuk, wuvt, place, cos_t, sin_t):
    B, S, D = x.shape
    tm = TQ
    assert S % tm == 0
    tok = lambda w: pl.BlockSpec((1, tm, w), lambda b, i: (b, i, 0))
    tok_t = lambda r: pl.BlockSpec((1, r, tm), lambda b, i: (b, 0, i))
    consts = (g, win, wkrt, gq, gkv, wuqt, wuk, wuvt, place)
    nvt = wuvt.shape[0]
    return pl.pallas_call(
        _inproj1_kernel,
        grid=(B, S // tm),
        in_specs=[tok(D)] + [_const_spec(a.shape) for a in consts] + [tok_t(cos_t.shape[1])] * 2,
        out_specs=[tok_t(wuqt.shape[0]), tok(wuk.shape[1]),
                   pl.BlockSpec((1, 1, nvt, tm), lambda b, i: (b, i, 0, 0)), tok(MEM_WIDTH)],
        out_shape=[jax.ShapeDtypeStruct((B, wuqt.shape[0], S), BF16),
                   jax.ShapeDtypeStruct((B, S, wuk.shape[1]), BF16),
                   jax.ShapeDtypeStruct((B, S // tm, nvt, tm), BF16),
                   jax.ShapeDtypeStruct((B, S, MEM_WIDTH), BF16)],
        compiler_params=_params("arbitrary", "arbitrary"),
        name="inproj1",
    )(x, *consts, cos_t, sin_t)


def _flash_kernel(qt_ref, k_ref, vt_ref, y_ref, sa_ref, sb_ref, smax_ref, acc_ref, m_ref):
    i = pl.program_id(2)
    slots = (sa_ref, sb_ref)

    def scores_t(hh, blk):
        kh = k_ref[0, pl.ds(pl.multiple_of(blk * TQ, TQ), TQ), hh * MLA_QK_PAD:(hh + 1) * MLA_QK_PAD]
        qh = qt_ref[0, hh * MLA_QK_PAD:(hh + 1) * MLA_QK_PAD, :]
        return jnp.dot(kh, qh, preferred_element_type=F32)

    def v_t(hh, blk):
        return vt_ref[0, blk, hh * MLA_VT_ROWS:(hh + 1) * MLA_VT_ROWS, :]

    def col_max(s):
        parts = [jnp.max(s[r:r + 128], axis=0, keepdims=True) for r in range(0, TQ, 128)]
        return jnp.maximum(jnp.maximum(parts[0], parts[1]), jnp.maximum(parts[2], parts[3]))

    key_i = lax.broadcasted_iota(jnp.int32, (TQ, TQ), 0)
    qry_i = lax.broadcasted_iota(jnp.int32, (TQ, TQ), 1)

    def qk_stage(slot, blk, diagonal=False):
        for hh in range(2):
            s = scores_t(hh, blk)
            if diagonal:
                s = jnp.where(key_i <= qry_i, s, -jnp.inf)
            slots[slot][hh] = s
            smax_ref[slot, hh] = col_max(s)

    def pv_stage(slot, blk):
        for hh in range(2):
            m_prev = m_ref[hh]
            m_new = jnp.maximum(m_prev, smax_ref[slot, hh])
            alpha = jnp.exp2(m_prev - m_new)
            p = jnp.exp2(slots[slot][hh] - m_new).astype(BF16)
            acc_ref[hh] = alpha * acc_ref[hh] + jnp.dot(v_t(hh, blk), p, preferred_element_type=F32)
            m_ref[hh] = m_new

    m_ref[...] = jnp.full(m_ref.shape, -jnp.inf, F32)
    acc_ref[...] = jnp.zeros_like(acc_ref)

    @pl.when(i == 0)
    def _():
        qk_stage(0, 0, diagonal=True)
        pv_stage(0, 0)

    @pl.when(i >= 1)
    def _():
        qk_stage(0, 0)

    def body(t, carry):
        qk_stage(1, 2 * t + 1)
        pv_stage(0, 2 * t)
        qk_stage(0, 2 * t + 2)
        pv_stage(1, 2 * t + 1)
        return carry

    lax.fori_loop(0, (i - 1) // 2, body, 0)

    @pl.when(i % 2 == 1)
    def _():
        qk_stage(1, i, diagonal=True)
        pv_stage(0, i - 1)
        pv_stage(1, i)

    @pl.when((i % 2 == 0) & (i >= 2))
    def _():
        qk_stage(1, i - 1)
        pv_stage(0, i - 2)
        qk_stage(0, i, diagonal=True)
        pv_stage(1, i - 1)
        pv_stage(0, i)

    outs = []
    for hh in range(2):
        acc = acc_ref[hh]
        outs.append(acc[:MLA_V] * (1.0 / acc[MLA_V:MLA_V + 1]))
    y_ref[0] = jnp.concatenate(outs, axis=0).T.astype(BF16)


def _flash(qt, k, vt):
    B, S, _ = k.shape
    assert S % TQ == 0
    pairs = MLA_HEADS // 2
    return pl.pallas_call(
        _flash_kernel,
        grid=(B, pairs, S // TQ),
        in_specs=[pl.BlockSpec((1, 2 * MLA_QK_PAD, TQ), lambda b, p, i: (b, p, i)),
                  pl.BlockSpec((1, S, 2 * MLA_QK_PAD), lambda b, p, i: (b, 0, p)),
                  pl.BlockSpec((1, S // TQ, 2 * MLA_VT_ROWS, TQ), lambda b, p, i: (b, 0, p, 0))],
        out_specs=pl.BlockSpec((1, TQ, 2 * MLA_V), lambda b, p, i: (b, i, p)),
        out_shape=jax.ShapeDtypeStruct((B, S, MLA_HEADS * MLA_V), BF16),
        scratch_shapes=[pltpu.VMEM((2, TQ, TQ), F32),
                        pltpu.VMEM((2, TQ, TQ), F32),
                        pltpu.VMEM((2, 2, 1, TQ), F32),
                        pltpu.VMEM((2, MLA_VT_ROWS, TQ), F32),
                        pltpu.VMEM((2, 1, TQ), F32)],
        compiler_params=_params("arbitrary", "arbitrary", "arbitrary"),
        name="mla_flash",
    )(qt, k, vt)


def _pad_heads(w, heads, d, dp):
    lead = w.shape[:-1]
    w = w.reshape(lead + (heads, d))
    w = jnp.pad(w, [(0, 0)] * len(lead) + [(0, 0), (0, dp - d)])
    return w.reshape(lead + (heads * dp,))


def _pad_head_rows(w, heads, d, dp):
    return _pad_heads(w.T, heads, d, dp).T


def kernel(x, mem, positions, mem_norm, w_mem_kv, norm_mix0, w_in0, b_igate0, b_fgate0, w_conv0, w_hnorm0, w_out0, norm_ffn0, w_ff1_0, w_ff2_0, norm_mix1, w_in1, w_qnorm1, w_uq1, w_kvnorm1, w_ukv1, w_out1, norm_ffn1, w_ff1_1, w_ff2_1, final_norm):
    row = lambda g: g.reshape(1, -1).astype(F32)

    kt, vm = _mem_kv(mem, row(mem_norm), w_mem_kv[:, :MEM_WIDTH].T.astype(BF16),
                     w_mem_kv[:, MEM_WIDTH:].astype(BF16))

    nq = ML_HEADS * ML_QK_DIM
    o_v, o_o, o_g = 2 * nq, 2 * nq + MIX_WIDTH, 2 * nq + 2 * MIX_WIDTH
    o_qm = o_g + 2 * ML_HEADS
    wqk = jnp.concatenate([_pad_heads(w_in0[:, :nq], ML_HEADS, ML_QK_DIM, ML_QK_PAD),
                           _pad_heads(w_in0[:, nq:o_v], ML_HEADS, ML_QK_DIM, ML_QK_PAD)], axis=1)
    wvt0 = _pad_heads(w_in0[:, o_v:o_o], ML_HEADS, ML_V_DIM, ML_V_PAD).T
    wo0 = _pad_heads(w_in0[:, o_o:o_g], ML_HEADS, ML_V_DIM, ML_V_PAD)
    wg0 = jnp.pad(w_in0[:, o_g:o_qm], ((0, 0), (0, LANES - 2 * ML_HEADS)))
    wqm0 = w_in0[:, o_qm:]
    wconv = jnp.concatenate([_pad_heads(w_conv0[:, :nq], ML_HEADS, ML_QK_DIM, ML_QK_PAD),
                             _pad_heads(w_conv0[:, nq:], ML_HEADS, ML_QK_DIM, ML_QK_PAD)], axis=1).astype(F32)
    gate_bias = jnp.pad(jnp.concatenate([b_igate0, b_fgate0]), (0, LANES - 2 * ML_HEADS)).reshape(1, LANES).astype(F32)
    hnorm = jnp.broadcast_to(w_hnorm0.astype(F32)[:, :, None], (ML_HEADS, ML_V_DIM, LANES))
    wa0 = _pad_head_rows(w_out0[:MIX_WIDTH], ML_HEADS, ML_V_DIM, ML_V_PAD).astype(BF16)
    wb0 = w_out0[MIX_WIDTH:].astype(BF16)

    qk, vt0, o0, qm0, gates = _inproj0(x, row(norm_mix0), wqk.astype(BF16), wvt0.astype(BF16),
                                       wo0.astype(BF16), wqm0.astype(BF16), wg0.astype(BF16))
    y0 = _mlstm(qk, vt0, o0, gates, wconv, gate_bias, hnorm)
    x = _outproj(x, y0, qm0, kt, vm, wa0, wb0)
    x = _ffn(x, row(norm_ffn0), w_ff1_0.astype(BF16), w_ff2_0.astype(BF16), row(final_norm), False)

    o_kr = MLA_Q_RANK + MLA_KV_RANK
    o_qm1 = o_kr + MLA_ROPE
    win1 = jnp.concatenate([w_in1[:, :o_kr], w_in1[:, o_qm1:]], axis=1)
    wkrt = jnp.pad(w_in1[:, o_kr:o_qm1].T, ((0, LANES - MLA_ROPE), (0, 0)))
    wuqt = _pad_heads(w_uq1, MLA_HEADS, MLA_NOPE + MLA_ROPE, MLA_QK_PAD).T
    ukv = w_ukv1.reshape(MLA_KV_RANK, MLA_HEADS, MLA_NOPE + MLA_V)
    wuk = _pad_heads(ukv[:, :, :MLA_NOPE].reshape(MLA_KV_RANK, -1), MLA_HEADS, MLA_NOPE, MLA_QK_PAD)
    wuvt = _pad_heads(ukv[:, :, MLA_NOPE:].reshape(MLA_KV_RANK, -1), MLA_HEADS, MLA_V, MLA_VT_ROWS).T
    place = np.zeros((LANES, MLA_HEADS * MLA_QK_PAD), np.float32)
    for h in range(MLA_HEADS):
        place[np.arange(MLA_ROPE), h * MLA_QK_PAD + MLA_NOPE + np.arange(MLA_ROPE)] = 1.0
    place = jnp.asarray(place, BF16)

    cos_t, sin_t = _rope_tables(positions)
    qt1, k1, vt1, qm1 = _inproj1(x, row(norm_mix1), win1.astype(BF16), wkrt.astype(BF16),
                                 row(w_qnorm1), row(w_kvnorm1), wuqt.astype(BF16), wuk.astype(BF16),
                                 wuvt.astype(BF16), place, cos_t, sin_t)
    y1 = _flash(qt1, k1, vt1)
    x = _outproj(x, y1, qm1, kt, vm, w_out1[:MIX_WIDTH].astype(BF16), w_out1[MIX_WIDTH:].astype(BF16))
    return _ffn(x, row(norm_ffn1), w_ff1_1.astype(BF16), w_ff2_1.astype(BF16), row(final_norm), True)
```

```python
import functools
import math

import jax
import jax.numpy as jnp
from jax import lax
from jax.experimental import pallas as pl
from jax.experimental.pallas import tpu as pltpu

F32 = jnp.float32
BF16 = jnp.bfloat16
EPS = 1e-6

D_MODEL = 1024
N_MEM = 256
MEM_HEADS = 4
MEM_HEAD_DIM = 64
MEM_WIDTH = MEM_HEADS * MEM_HEAD_DIM
MIX_WIDTH = D_MODEL - MEM_WIDTH

ML_HEADS = 4
ML_V_DIM = MIX_WIDTH // ML_HEADS
ML_QK_DIM = ML_V_DIM // 2
ML_CONV = 4
ML_CHUNK = 128
ML_QK_PAD = 128
ML_V_PAD = 256
ML_ONES_ROW = ML_V_DIM

MLA_HEADS = 12
MLA_NOPE = 64
MLA_ROPE = 32
MLA_V = 64
MLA_Q_RANK = 384
MLA_KV_RANK = 256
MLA_QK_PAD = 128
MLA_VT_ROWS = 80
ROPE_THETA = 10000.0
D_FF = 4 * D_MODEL

LANES = 128
VMEM_LIMIT = 56 * 1024 * 1024

TM_PROJ = 512
FF_CHUNK = 1024
TQ = 512

_NT = (((1,), (1,)), ((), ()))


def _params(*sem):
    return pltpu.CompilerParams(dimension_semantics=sem, vmem_limit_bytes=VMEM_LIMIT)


def _rms(x, g):
    return x * lax.rsqrt(jnp.mean(x * x, axis=-1, keepdims=True) + EPS) * g


def _const_spec(shape):
    nd = len(shape)
    return pl.BlockSpec(shape, lambda *_: (0,) * nd)


def _mem_kv_kernel(mem_ref, g_ref, wkt_ref, wv_ref, kt_ref, v_ref):
    xn = _rms(mem_ref[0], g_ref[...]).astype(BF16)
    kt = lax.dot_general(wkt_ref[...], xn, _NT, preferred_element_type=F32)
    v = jnp.dot(xn, wv_ref[...], preferred_element_type=F32)
    kt = kt * (MEM_HEAD_DIM ** -0.5)
    row_head = lax.broadcasted_iota(jnp.int32, kt.shape, 0) // MEM_HEAD_DIM
    col_head = lax.broadcasted_iota(jnp.int32, v.shape, 1) // MEM_HEAD_DIM
    for h in range(MEM_HEADS):
        kt_ref[0, h] = jnp.where(row_head == h, kt, 0.0).astype(BF16)
        v_ref[0, h] = jnp.where(col_head == h, v, 0.0).astype(BF16)


def _mem_kv(mem, g, wkt, wv):
    B = mem.shape[0]
    out = jax.ShapeDtypeStruct((B, MEM_HEADS, MEM_WIDTH, N_MEM), BF16)
    return pl.pallas_call(
        _mem_kv_kernel,
        grid=(B,),
        in_specs=[pl.BlockSpec((1, N_MEM, D_MODEL), lambda b: (b, 0, 0)),
                  _const_spec(g.shape), _const_spec(wkt.shape), _const_spec(wv.shape)],
        out_specs=[pl.BlockSpec((1, MEM_HEADS, MEM_WIDTH, N_MEM), lambda b: (b, 0, 0, 0)),
                   pl.BlockSpec((1, MEM_HEADS, N_MEM, MEM_WIDTH), lambda b: (b, 0, 0, 0))],
        out_shape=[out, out],
        compiler_params=_params("arbitrary"),
        name="mem_kv",
    )(mem, g, wkt, wv)


def _inproj0_kernel(x_ref, g_ref, wqk_ref, wv_ref, wo_ref, wqm_ref, wg_ref,
                    qk_ref, v_ref, o_ref, qm_ref, gate_ref):
    xn = _rms(x_ref[0], g_ref[...]).astype(BF16)
    qk_ref[0] = jnp.dot(xn, wqk_ref[...], preferred_element_type=F32).astype(BF16)
    vt = lax.dot_general(wv_ref[...], xn, _NT, preferred_element_type=F32)
    rowid = lax.broadcasted_iota(jnp.int32, vt.shape, 0) % ML_V_PAD
    v_ref[0] = jnp.where(rowid == ML_ONES_ROW, 1.0, vt).astype(BF16)
    o_ref[0] = jnp.dot(xn, wo_ref[...], preferred_element_type=F32).astype(BF16)
    qm_ref[0] = jnp.dot(xn, wqm_ref[...], preferred_element_type=F32).astype(BF16)
    gate_ref[0] = jnp.dot(xn, wg_ref[...], preferred_element_type=F32)


def _inproj0(x, g, wqk, wvt, wo, wqm, wg):
    B, S, D = x.shape
    tm = min(TM_PROJ, S)
    tok = lambda w: pl.BlockSpec((1, tm, w), lambda b, i: (b, i, 0))
    tok_t = lambda r: pl.BlockSpec((1, r, tm), lambda b, i: (b, 0, i))
    nv = wvt.shape[0]
    consts = (g, wqk, wvt, wo, wqm, wg)
    return pl.pallas_call(
        _inproj0_kernel,
        grid=(B, S // tm),
        in_specs=[tok(D)] + [_const_spec(a.shape) for a in consts],
        out_specs=[tok(wqk.shape[1]), tok_t(nv), tok(wo.shape[1]), tok(wqm.shape[1]), tok(wg.shape[1])],
        out_shape=[jax.ShapeDtypeStruct((B, S, wqk.shape[1]), BF16),
                   jax.ShapeDtypeStruct((B, nv, S), BF16),
                   jax.ShapeDtypeStruct((B, S, wo.shape[1]), BF16),
                   jax.ShapeDtypeStruct((B, S, wqm.shape[1]), BF16),
                   jax.ShapeDtypeStruct((B, S, wg.shape[1]), F32)],
        compiler_params=_params("arbitrary", "arbitrary"),
        name="inproj0",
    )(x, *consts)


def _split3(x):
    hi = x.astype(BF16)
    r = x - hi.astype(F32)
    mid = r.astype(BF16)
    lo = (r - mid.astype(F32)).astype(BF16)
    return hi, mid, lo


def _mlstm_kernel(qk_ref, vt_ref, o_ref, gate_ref, wconv_ref, bias_ref, hnorm_ref,
                  y_ref, ct_ref, m_ref, tail_ref):
    L = ML_CHUNK
    HQ = ML_HEADS * ML_QK_PAD

    @pl.when(pl.program_id(1) == 0)
    def _():
        ct_ref[...] = jnp.zeros_like(ct_ref)
        m_ref[...] = jnp.zeros_like(m_ref)
        tail_ref[...] = jnp.zeros_like(tail_ref)

    x = qk_ref[0].astype(F32)
    tail = tail_ref[...]
    wc = wconv_ref[...]
    row8 = lax.broadcasted_iota(jnp.int32, tail.shape, 0)
    conv = x * wc[ML_CONV - 1:ML_CONV]
    for s in range(1, ML_CONV):
        xs = pltpu.roll(x, s, axis=0)
        head = jnp.where(row8 < s, pltpu.roll(tail, s, axis=0), xs[0:8])
        xs = jnp.concatenate([head, xs[8:]], axis=0)
        conv = conv + xs * wc[ML_CONV - 1 - s:ML_CONV - s]
    tail_ref[...] = x[L - 8:L]
    act = conv * jax.nn.sigmoid(conv)
    q = act[:, :HQ].astype(BF16)
    k = (act[:, HQ:] * (ML_QK_DIM ** -0.5)).astype(BF16)

    gates = gate_ref[0] + bias_ref[...]
    lane = lax.broadcasted_iota(jnp.int32, gates.shape, 1)
    log_f = jnp.minimum(gates, 0.0) - jnp.log1p(jnp.exp(-jnp.abs(gates)))
    log_f = jnp.where((lane >= ML_HEADS) & (lane < 2 * ML_HEADS), log_f, 0.0)
    key_i = lax.broadcasted_iota(jnp.int32, (L, L), 0)
    qry_i = lax.broadcasted_iota(jnp.int32, (L, L), 1)
    tril = jnp.where(qry_i <= key_i, 1.0, 0.0).astype(BF16)
    causal_t = key_i <= qry_i
    b_cols = sum(jnp.dot(tril, part, preferred_element_type=F32) for part in _split3(log_f))
    b_rows = b_cols.T
    g_rows = gates.T

    for h in range(ML_HEADS):
        b_row = b_rows[ML_HEADS + h:ML_HEADS + h + 1, :]
        c_row = g_rows[h:h + 1, :] - b_row
        c_col = gates[:, h:h + 1] - b_cols[:, ML_HEADS + h:ML_HEADS + h + 1]
        m_prev = m_ref[h][:, 0:1]
        qh = q[:, h * ML_QK_PAD:(h + 1) * ML_QK_PAD]
        kh = k[:, h * ML_QK_PAD:(h + 1) * ML_QK_PAD]
        vth = vt_ref[0, h * ML_V_PAD:(h + 1) * ML_V_PAD, :]
        ct_prev = ct_ref[h]

        cm = jnp.where(causal_t, c_col, -jnp.inf)
        m_row = jnp.maximum(m_prev, jnp.max(cm, axis=0, keepdims=True))
        w_t = jnp.exp(cm - m_row)
        a_row = jnp.exp(m_prev - m_row)
        s_t = (lax.dot_general(kh, qh, _NT, preferred_element_type=F32) * w_t).astype(BF16)
        num_t = (jnp.dot(vth, s_t, preferred_element_type=F32)
                 + a_row * lax.dot_general(ct_prev.astype(BF16), qh, _NT, preferred_element_type=F32))
        den = num_t[ML_ONES_ROW:ML_ONES_ROW + 1]
        inv = 1.0 / jnp.maximum(jnp.abs(den), jnp.exp(-(b_row + m_row)))
        hc_t = num_t[:ML_V_DIM] * inv
        ms = jnp.sum(hc_t * hc_t, axis=0, keepdims=True) * (1.0 / ML_V_DIM)
        hn_t = hc_t * lax.rsqrt(ms + EPS) * hnorm_ref[h]
        hn = jnp.concatenate([hn_t, jnp.zeros((ML_V_PAD - ML_V_DIM, L), F32)], axis=0).T
        og = o_ref[0, :, h * ML_V_PAD:(h + 1) * ML_V_PAD].astype(F32)
        y_ref[0, :, h * ML_V_PAD:(h + 1) * ML_V_PAD] = (hn * jax.nn.sigmoid(og)).astype(BF16)

        m_end = m_row[:, L - 1:L]
        u_row = jnp.exp(c_row - m_end)
        decay = jnp.exp(m_prev - m_end)
        uv_t = (vth.astype(F32) * u_row).astype(BF16)
        ct_ref[h] = decay * ct_prev + jnp.dot(uv_t, kh, preferred_element_type=F32)
        m_ref[h] = jnp.broadcast_to(b_row[:, L - 1:L] + m_end, (1, LANES))


def _mlstm(qk, vt, o, gates, wconv, bias, hnorm):
    B, S, _ = qk.shape
    L = ML_CHUNK
    tok = lambda w: pl.BlockSpec((1, L, w), lambda b, c: (b, c, 0))
    return pl.pallas_call(
        _mlstm_kernel,
        grid=(B, S // L),
        in_specs=[tok(qk.shape[2]), pl.BlockSpec((1, vt.shape[1], L), lambda b, c: (b, 0, c)),
                  tok(o.shape[2]), tok(gates.shape[2]),
                  _const_spec(wconv.shape), _const_spec(bias.shape), _const_spec(hnorm.shape)],
        out_specs=tok(o.shape[2]),
        out_shape=jax.ShapeDtypeStruct(o.shape, BF16),
        scratch_shapes=[pltpu.VMEM((ML_HEADS, ML_V_PAD, ML_QK_PAD), F32),
                        pltpu.VMEM((ML_HEADS, 1, LANES), F32),
                        pltpu.VMEM((8, qk.shape[2]), F32)],
        compiler_params=_params("arbitrary", "arbitrary"),
        name="mlstm",
    )(qk, vt, o, gates, wconv, bias, hnorm)


def _outproj_kernel(x_ref, y_ref, qm_ref, kt_ref, vm_ref, w_ref, out_ref):
    tm = x_ref.shape[1]
    halves = [slice(r0, r0 + tm // 2) for r0 in range(0, tm, tm // 2)]
    scores = [[jnp.dot(qm_ref[0, rows, :], kt_ref[0, h], preferred_element_type=F32)
               for h in range(MEM_HEADS)] for rows in halves]
    for rows, s_heads in zip(halves, scores):
        ymem = None
        for h in range(MEM_HEADS):
            s = s_heads[h]
            e = jnp.exp(s - jnp.max(s, axis=-1, keepdims=True))
            p = (e * (1.0 / jnp.sum(e, axis=-1, keepdims=True))).astype(BF16)
            oh = jnp.dot(p, vm_ref[0, h], preferred_element_type=F32)
            ymem = oh if ymem is None else ymem + oh
        ycat = jnp.concatenate([y_ref[0, rows, :], ymem.astype(BF16)], axis=1)
        out_ref[0, rows, :] = x_ref[0, rows, :] + jnp.dot(ycat, w_ref[...], preferred_element_type=F32)


def _outproj(x, y, qm, kt, vm, w):
    B, S, D = x.shape
    tm = min(TM_PROJ, S)
    tok = lambda w: pl.BlockSpec((1, tm, w), lambda b, i: (b, i, 0))
    per_b = pl.BlockSpec((1,) + kt.shape[1:], lambda b, i: (b, 0, 0, 0))
    return pl.pallas_call(
        _outproj_kernel,
        grid=(B, S // tm),
        in_specs=[tok(D), tok(y.shape[2]), tok(qm.shape[2]), per_b, per_b,
                  _const_spec(w.shape)],
        out_specs=tok(D),
        out_shape=jax.ShapeDtypeStruct(x.shape, F32),
        compiler_params=_params("arbitrary", "arbitrary"),
        name="outproj",
    )(x, y, qm, kt, vm, w)


def _ffn_kernel(x_ref, g_ref, w1_ref, w2_ref, gf_ref, out_ref, *, final_norm):
    x = x_ref[0]
    hn = _rms(x, g_ref[...]).astype(BF16)
    acc = x
    for c in range(D_FF // FF_CHUNK):
        a = jnp.dot(hn, w1_ref[:, c * FF_CHUNK:(c + 1) * FF_CHUNK], preferred_element_type=F32)
        a = jnp.square(jnp.maximum(a, 0.0)).astype(BF16)
        acc = acc + jnp.dot(a, w2_ref[c * FF_CHUNK:(c + 1) * FF_CHUNK, :], preferred_element_type=F32)
    if final_norm:
        acc = _rms(acc, gf_ref[...])
    out_ref[0] = acc


def _ffn(x, g, w1, w2, gf, final_norm):
    B, S, D = x.shape
    tm = min(TM_PROJ, S)
    tok = pl.BlockSpec((1, tm, D), lambda b, i: (b, i, 0))
    return pl.pallas_call(
        functools.partial(_ffn_kernel, final_norm=final_norm),
        grid=(B, S // tm),
        in_specs=[tok, _const_spec(g.shape), _const_spec(w1.shape), _const_spec(w2.shape),
                  _const_spec(gf.shape)],
        out_specs=tok,
        out_shape=jax.ShapeDtypeStruct(x.shape, F32),
        compiler_params=_params("arbitrary", "arbitrary"),
        name="ffn",
    )(x, g, w1, w2, gf)


def _rope_kernel(pos_ref, inv_ref, cos_ref, sin_ref):
    ang = pos_ref[0].astype(F32) * inv_ref[...]
    cos_ref[0] = jnp.cos(ang)
    sin_ref[0] = jnp.sin(ang)


def _rope_tables(positions):
    B, S = positions.shape
    nf = MLA_ROPE // 2
    inv = (ROPE_THETA ** (-jnp.arange(0, MLA_ROPE, 2, dtype=F32) / MLA_ROPE)).reshape(nf, 1)
    out = jax.ShapeDtypeStruct((B, nf, S), F32)
    spec = pl.BlockSpec((1, nf, S), lambda b: (b, 0, 0))
    return pl.pallas_call(
        _rope_kernel,
        grid=(B,),
        in_specs=[pl.BlockSpec((1, 1, S), lambda b: (b, 0, 0)), _const_spec(inv.shape)],
        out_specs=[spec, spec],
        out_shape=[out, out],
        compiler_params=_params("arbitrary"),
        name="rope_tables",
    )(positions.reshape(B, 1, S), inv)


def _inproj1_kernel(x_ref, g_ref, win_ref, wkrt_ref, gq_ref, gkv_ref, wuqt_ref, wuk_ref, wuvt_ref,
                    cos_ref, sin_ref, qt_ref, k_ref, vt_ref, qm_ref):
    half = MLA_ROPE // 2
    xn = _rms(x_ref[0], g_ref[...]).astype(BF16)
    c = jnp.dot(xn, win_ref[...], preferred_element_type=F32)
    o_kv = MLA_Q_RANK
    o_qm = o_kv + MLA_KV_RANK
    qm_ref[0] = c[:, o_qm:].astype(BF16)
    cq = _rms(c[:, :o_kv], gq_ref[...]).astype(BF16)
    ckv = _rms(c[:, o_kv:o_qm], gkv_ref[...]).astype(BF16)
    cos_t, sin_t = cos_ref[0], sin_ref[0]

    def rope_t(x1, x2):
        return x1 * cos_t - x2 * sin_t, x2 * cos_t + x1 * sin_t

    qscale = ((MLA_NOPE + MLA_ROPE) ** -0.5) * math.log2(math.e)
    qt = lax.dot_general(wuqt_ref[...], cq, _NT, preferred_element_type=F32)
    zpad = jnp.zeros((MLA_QK_PAD - MLA_NOPE - MLA_ROPE, qt.shape[1]), F32)
    for h in range(MLA_HEADS):
        b0 = h * MLA_QK_PAD
        r1, r2 = rope_t(qt[b0 + MLA_NOPE:b0 + MLA_NOPE + half],
                        qt[b0 + MLA_NOPE + half:b0 + MLA_NOPE + MLA_ROPE])
        blk = jnp.concatenate([qt[b0:b0 + MLA_NOPE], r1, r2, zpad], axis=0) * qscale
        qt_ref[0, b0:b0 + MLA_QK_PAD] = blk.astype(BF16)

    krt = lax.dot_general(wkrt_ref[...], xn, _NT, preferred_element_type=F32)
    r1, r2 = rope_t(krt[MLA_NOPE:MLA_NOPE + half], krt[MLA_NOPE + half:MLA_NOPE + MLA_ROPE])
    kr = jnp.concatenate([krt[:MLA_NOPE], r1, r2, krt[MLA_NOPE + MLA_ROPE:]], axis=0).T
    k = jnp.dot(ckv, wuk_ref[...], preferred_element_type=F32)
    for h in range(MLA_HEADS):
        sl = slice(h * MLA_QK_PAD, (h + 1) * MLA_QK_PAD)
        k_ref[0, :, sl] = (k[:, sl] + kr).astype(BF16)

    vt = lax.dot_general(wuvt_ref[...], ckv, _NT, preferred_element_type=F32)
    rowid = lax.broadcasted_iota(jnp.int32, vt.shape, 0) % MLA_VT_ROWS
    vt_ref[0, 0] = jnp.where(rowid == MLA_V, 1.0, vt).astype(BF16)


def _inproj1(x, g, win, wkrt, gq, gkv, wuqt, wuk, wuvt, cos_t, sin_t):
    B, S, D = x.shape
    tm = TQ
    assert S % tm == 0
    tok = lambda w: pl.BlockSpec((1, tm, w), lambda b, i: (b, i, 0))
    tok_t = lambda r: pl.BlockSpec((1, r, tm), lambda b, i: (b, 0, i))
    consts = (g, win, wkrt, gq, gkv, wuqt, wuk, wuvt)
    nvt = wuvt.shape[0]
    return pl.pallas_call(
        _inproj1_kernel,
        grid=(B, S // tm),
        in_specs=[tok(D)] + [_const_spec(a.shape) for a in consts] + [tok_t(cos_t.shape[1])] * 2,
        out_specs=[tok_t(wuqt.shape[0]), tok(wuk.shape[1]),
                   pl.BlockSpec((1, 1, nvt, tm), lambda b, i: (b, i, 0, 0)), tok(MEM_WIDTH)],
        out_shape=[jax.ShapeDtypeStruct((B, wuqt.shape[0], S), BF16),
                   jax.ShapeDtypeStruct((B, S, wuk.shape[1]), BF16),
                   jax.ShapeDtypeStruct((B, S // tm, nvt, tm), BF16),
                   jax.ShapeDtypeStruct((B, S, MEM_WIDTH), BF16)],
        compiler_params=_params("arbitrary", "arbitrary"),
        name="inproj1",
    )(x, *consts, cos_t, sin_t)


def _flash_kernel(qt_ref, k_ref, vt_ref, y_ref, sa_ref, sb_ref, smax_ref, acc_ref, m_ref):
    i = pl.program_id(2)
    slots = (sa_ref, sb_ref)

    def scores_t(hh, blk):
        kh = k_ref[0, pl.ds(pl.multiple_of(blk * TQ, TQ), TQ), hh * MLA_QK_PAD:(hh + 1) * MLA_QK_PAD]
        qh = qt_ref[0, hh * MLA_QK_PAD:(hh + 1) * MLA_QK_PAD, :]
        return jnp.dot(kh, qh, preferred_element_type=F32)

    def v_t(hh, blk):
        return vt_ref[0, blk, hh * MLA_VT_ROWS:(hh + 1) * MLA_VT_ROWS, :]

    def col_max(s):
        parts = [jnp.max(s[r:r + 128], axis=0, keepdims=True) for r in range(0, TQ, 128)]
        return jnp.maximum(jnp.maximum(parts[0], parts[1]), jnp.maximum(parts[2], parts[3]))

    key_i = lax.broadcasted_iota(jnp.int32, (TQ, TQ), 0)
    qry_i = lax.broadcasted_iota(jnp.int32, (TQ, TQ), 1)

    def qk_stage(slot, blk, diagonal=False):
        for hh in range(2):
            s = scores_t(hh, blk)
            if diagonal:
                s = jnp.where(key_i <= qry_i, s, -jnp.inf)
            slots[slot][hh] = s
            smax_ref[slot, hh] = col_max(s)

    def pv_stage(slot, blk):
        for hh in range(2):
            m_prev = m_ref[hh]
            m_new = jnp.maximum(m_prev, smax_ref[slot, hh])
            alpha = jnp.exp2(m_prev - m_new)
            p = jnp.exp2(slots[slot][hh] - m_new).astype(BF16)
            acc_ref[hh] = alpha * acc_ref[hh] + jnp.dot(v_t(hh, blk), p, preferred_element_type=F32)
            m_ref[hh] = m_new

    m_ref[...] = jnp.full(m_ref.shape, -jnp.inf, F32)
    acc_ref[...] = jnp.zeros_like(acc_ref)

    @pl.when(i == 0)
    def _():
        qk_stage(0, 0, diagonal=True)
        pv_stage(0, 0)

    @pl.when(i >= 1)
    def _():
        qk_stage(0, 0)

    def body(t, carry):
        qk_stage(1, 2 * t + 1)
        pv_stage(0, 2 * t)
        qk_stage(0, 2 * t + 2)
        pv_stage(1, 2 * t + 1)
        return carry

    lax.fori_loop(0, (i - 1) // 2, body, 0)

    @pl.when(i % 2 == 1)
    def _():
        qk_stage(1, i, diagonal=True)
        pv_stage(0, i - 1)
        pv_stage(1, i)

    @pl.when((i % 2 == 0) & (i >= 2))
    def _():
        qk_stage(1, i - 1)
        pv_stage(0, i - 2)
        qk_stage(0, i, diagonal=True)
        pv_stage(1, i - 1)
        pv_stage(0, i)

    outs = []
    for hh in range(2):
        acc = acc_ref[hh]
        outs.append(acc[:MLA_V] * (1.0 / acc[MLA_V:MLA_V + 1]))
    y_ref[0] = jnp.concatenate(outs, axis=0).T.astype(BF16)


def _flash(qt, k, vt):
    B, S, _ = k.shape
    assert S % TQ == 0
    pairs = MLA_HEADS // 2
    return pl.pallas_call(
        _flash_kernel,
        grid=(B, pairs, S // TQ),
        in_specs=[pl.BlockSpec((1, 2 * MLA_QK_PAD, TQ), lambda b, p, i: (b, p, i)),
                  pl.BlockSpec((1, S, 2 * MLA_QK_PAD), lambda b, p, i: (b, 0, p)),
                  pl.BlockSpec((1, S // TQ, 2 * MLA_VT_ROWS, TQ), lambda b, p, i: (b, 0, p, 0))],
        out_specs=pl.BlockSpec((1, TQ, 2 * MLA_V), lambda b, p, i: (b, i, p)),
        out_shape=jax.ShapeDtypeStruct((B, S, MLA_HEADS * MLA_V), BF16),
        scratch_shapes=[pltpu.VMEM((2, TQ, TQ), F32),
                        pltpu.VMEM((2, TQ, TQ), F32),
                        pltpu.VMEM((2, 2, 1, TQ), F32),
                        pltpu.VMEM((2, MLA_VT_ROWS, TQ), F32),
                        pltpu.VMEM((2, 1, TQ), F32)],
        compiler_params=_params("arbitrary", "arbitrary", "arbitrary"),
        name="mla_flash",
    )(qt, k, vt)


def _pad_heads(w, heads, d, dp):
    lead = w.shape[:-1]
    w = w.reshape(lead + (heads, d))
    w = jnp.pad(w, [(0, 0)] * len(lead) + [(0, 0), (0, dp - d)])
    return w.reshape(lead + (heads * dp,))


def _pad_head_rows(w, heads, d, dp):
    return _pad_heads(w.T, heads, d, dp).T


def kernel(x, mem, positions, mem_norm, w_mem_kv, norm_mix0, w_in0, b_igate0, b_fgate0, w_conv0, w_hnorm0, w_out0, norm_ffn0, w_ff1_0, w_ff2_0, norm_mix1, w_in1, w_qnorm1, w_uq1, w_kvnorm1, w_ukv1, w_out1, norm_ffn1, w_ff1_1, w_ff2_1, final_norm):
    row = lambda g: g.reshape(1, -1).astype(F32)

    kt, vm = _mem_kv(mem, row(mem_norm), w_mem_kv[:, :MEM_WIDTH].T.astype(BF16),
                     w_mem_kv[:, MEM_WIDTH:].astype(BF16))

    nq = ML_HEADS * ML_QK_DIM
    o_v, o_o, o_g = 2 * nq, 2 * nq + MIX_WIDTH, 2 * nq + 2 * MIX_WIDTH
    o_qm = o_g + 2 * ML_HEADS
    wqk = jnp.concatenate([_pad_heads(w_in0[:, :nq], ML_HEADS, ML_QK_DIM, ML_QK_PAD),
                           _pad_heads(w_in0[:, nq:o_v], ML_HEADS, ML_QK_DIM, ML_QK_PAD)], axis=1)
    wvt0 = _pad_heads(w_in0[:, o_v:o_o], ML_HEADS, ML_V_DIM, ML_V_PAD).T
    wo0 = _pad_heads(w_in0[:, o_o:o_g], ML_HEADS, ML_V_DIM, ML_V_PAD)
    wg0 = jnp.pad(w_in0[:, o_g:o_qm], ((0, 0), (0, LANES - 2 * ML_HEADS)))
    wqm0 = w_in0[:, o_qm:]
    wconv = jnp.concatenate([_pad_heads(w_conv0[:, :nq], ML_HEADS, ML_QK_DIM, ML_QK_PAD),
                             _pad_heads(w_conv0[:, nq:], ML_HEADS, ML_QK_DIM, ML_QK_PAD)], axis=1).astype(F32)
    gate_bias = jnp.pad(jnp.concatenate([b_igate0, b_fgate0]), (0, LANES - 2 * ML_HEADS)).reshape(1, LANES).astype(F32)
    hnorm = jnp.broadcast_to(w_hnorm0.astype(F32)[:, :, None], (ML_HEADS, ML_V_DIM, LANES))
    wout0 = jnp.concatenate([_pad_head_rows(w_out0[:MIX_WIDTH], ML_HEADS, ML_V_DIM, ML_V_PAD),
                             w_out0[MIX_WIDTH:]], axis=0).astype(BF16)

    qk, vt0, o0, qm0, gates = _inproj0(x, row(norm_mix0), wqk.astype(BF16), wvt0.astype(BF16),
                                       wo0.astype(BF16), wqm0.astype(BF16), wg0.astype(BF16))
    y0 = _mlstm(qk, vt0, o0, gates, wconv, gate_bias, hnorm)
    x = _outproj(x, y0, qm0, kt, vm, wout0)
    x = _ffn(x, row(norm_ffn0), w_ff1_0.astype(BF16), w_ff2_0.astype(BF16), row(final_norm), False)

    o_kr = MLA_Q_RANK + MLA_KV_RANK
    o_qm1 = o_kr + MLA_ROPE
    win1 = jnp.concatenate([w_in1[:, :o_kr], w_in1[:, o_qm1:]], axis=1)
    wkrt = jnp.pad(w_in1[:, o_kr:o_qm1].T, ((MLA_NOPE, LANES - MLA_NOPE - MLA_ROPE), (0, 0)))
    wuqt = _pad_heads(w_uq1, MLA_HEADS, MLA_NOPE + MLA_ROPE, MLA_QK_PAD).T
    ukv = w_ukv1.reshape(MLA_KV_RANK, MLA_HEADS, MLA_NOPE + MLA_V)
    wuk = _pad_heads(ukv[:, :, :MLA_NOPE].reshape(MLA_KV_RANK, -1), MLA_HEADS, MLA_NOPE, MLA_QK_PAD)
    wuvt = _pad_heads(ukv[:, :, MLA_NOPE:].reshape(MLA_KV_RANK, -1), MLA_HEADS, MLA_V, MLA_VT_ROWS).T

    cos_t, sin_t = _rope_tables(positions)
    qt1, k1, vt1, qm1 = _inproj1(x, row(norm_mix1), win1.astype(BF16), wkrt.astype(BF16),
                                 row(w_qnorm1), row(w_kvnorm1), wuqt.astype(BF16), wuk.astype(BF16),
                                 wuvt.astype(BF16), cos_t, sin_t)
    y1 = _flash(qt1, k1, vt1)
    x = _outproj(x, y1, qm1, kt, vm, w_out1.astype(BF16))
    return _ffn(x, row(norm_ffn1), w_ff1_1.astype(BF16), w_ff2_1.astype(BF16), row(final_norm), True)
```

```python
import functools
import math

import jax
import jax.numpy as jnp
from jax import lax
from jax.experimental import pallas as pl
from jax.experimental.pallas import tpu as pltpu

F32 = jnp.float32
BF16 = jnp.bfloat16
EPS = 1e-6

D_MODEL = 1024
N_MEM = 256
MEM_HEADS = 4
MEM_HEAD_DIM = 64
MEM_WIDTH = MEM_HEADS * MEM_HEAD_DIM
MIX_WIDTH = D_MODEL - MEM_WIDTH

ML_HEADS = 4
ML_V_DIM = MIX_WIDTH // ML_HEADS
ML_QK_DIM = ML_V_DIM // 2
ML_CONV = 4
ML_CHUNK = 128
ML_QK_PAD = 128
ML_V_PAD = 256
ML_ONES_ROW = ML_V_DIM
ML_NB = 2

MLA_HEADS = 12
MLA_NOPE = 64
MLA_ROPE = 32
MLA_V = 64
MLA_Q_RANK = 384
MLA_KV_RANK = 256
MLA_QK_PAD = 128
MLA_VT_ROWS = 80
ROPE_THETA = 10000.0
D_FF = 4 * D_MODEL

LANES = 128
VMEM_LIMIT = 56 * 1024 * 1024

TM_PROJ = 512
FF_CHUNK = 1024
TQ = 512

_NT = (((1,), (1,)), ((), ()))


def _params(*sem):
    return pltpu.CompilerParams(dimension_semantics=sem, vmem_limit_bytes=VMEM_LIMIT)


def _rms(x, g):
    return x * lax.rsqrt(jnp.mean(x * x, axis=-1, keepdims=True) + EPS) * g


def _const_spec(shape):
    nd = len(shape)
    return pl.BlockSpec(shape, lambda *_: (0,) * nd)


def _mem_kv_kernel(mem_ref, g_ref, wkt_ref, wv_ref, kt_ref, v_ref):
    xn = _rms(mem_ref[0], g_ref[...]).astype(BF16)
    kt = lax.dot_general(wkt_ref[...], xn, _NT, preferred_element_type=F32)
    v = jnp.dot(xn, wv_ref[...], preferred_element_type=F32)
    kt = kt * (MEM_HEAD_DIM ** -0.5)
    row_head = lax.broadcasted_iota(jnp.int32, kt.shape, 0) // MEM_HEAD_DIM
    col_head = lax.broadcasted_iota(jnp.int32, v.shape, 1) // MEM_HEAD_DIM
    for h in range(MEM_HEADS):
        kt_ref[0, h] = jnp.where(row_head == h, kt, 0.0).astype(BF16)
        v_ref[0, h] = jnp.where(col_head == h, v, 0.0).astype(BF16)


def _mem_kv(mem, g, wkt, wv):
    B = mem.shape[0]
    out = jax.ShapeDtypeStruct((B, MEM_HEADS, MEM_WIDTH, N_MEM), BF16)
    return pl.pallas_call(
        _mem_kv_kernel,
        grid=(B,),
        in_specs=[pl.BlockSpec((1, N_MEM, D_MODEL), lambda b: (b, 0, 0)),
                  _const_spec(g.shape), _const_spec(wkt.shape), _const_spec(wv.shape)],
        out_specs=[pl.BlockSpec((1, MEM_HEADS, MEM_WIDTH, N_MEM), lambda b: (b, 0, 0, 0)),
                   pl.BlockSpec((1, MEM_HEADS, N_MEM, MEM_WIDTH), lambda b: (b, 0, 0, 0))],
        out_shape=[out, out],
        compiler_params=_params("arbitrary"),
        name="mem_kv",
    )(mem, g, wkt, wv)


def _inproj0_kernel(x_ref, g_ref, wqk_ref, wv_ref, wo_ref, wqm_ref, wg_ref,
                    qk_ref, v_ref, o_ref, qm_ref, gate_ref):
    xn = _rms(x_ref[0], g_ref[...]).astype(BF16)
    qk_ref[0] = jnp.dot(xn, wqk_ref[...], preferred_element_type=F32).astype(BF16)
    vt = lax.dot_general(wv_ref[...], xn, _NT, preferred_element_type=F32)
    rowid = lax.broadcasted_iota(jnp.int32, vt.shape, 0) % ML_V_PAD
    v_ref[0] = jnp.where(rowid == ML_ONES_ROW, 1.0, vt).astype(BF16)
    o_ref[0] = jnp.dot(xn, wo_ref[...], preferred_element_type=F32).astype(BF16)
    qm_ref[0] = jnp.dot(xn, wqm_ref[...], preferred_element_type=F32).astype(BF16)
    gate_ref[0] = jnp.dot(xn, wg_ref[...], preferred_element_type=F32)


def _inproj0(x, g, wqk, wvt, wo, wqm, wg):
    B, S, D = x.shape
    tm = min(TM_PROJ, S)
    tok = lambda w: pl.BlockSpec((1, tm, w), lambda b, i: (b, i, 0))
    tok_t = lambda r: pl.BlockSpec((1, r, tm), lambda b, i: (b, 0, i))
    nv = wvt.shape[0]
    consts = (g, wqk, wvt, wo, wqm, wg)
    return pl.pallas_call(
        _inproj0_kernel,
        grid=(B, S // tm),
        in_specs=[tok(D)] + [_const_spec(a.shape) for a in consts],
        out_specs=[tok(wqk.shape[1]), tok_t(nv), tok(wo.shape[1]), tok(wqm.shape[1]), tok(wg.shape[1])],
        out_shape=[jax.ShapeDtypeStruct((B, S, wqk.shape[1]), BF16),
                   jax.ShapeDtypeStruct((B, nv, S), BF16),
                   jax.ShapeDtypeStruct((B, S, wo.shape[1]), BF16),
                   jax.ShapeDtypeStruct((B, S, wqm.shape[1]), BF16),
                   jax.ShapeDtypeStruct((B, S, wg.shape[1]), F32)],
        compiler_params=_params("arbitrary", "arbitrary"),
        name="inproj0",
    )(x, *consts)


def _split3(x):
    hi = x.astype(BF16)
    r = x - hi.astype(F32)
    mid = r.astype(BF16)
    lo = (r - mid.astype(F32)).astype(BF16)
    return hi, mid, lo


def _mlstm_kernel(qk_ref, vt_ref, o_ref, gate_ref, wconv_ref, bias_ref, hnorm_ref,
                  y_ref, ct_ref, m_ref, tail_ref):
    L = ML_CHUNK
    HQ = ML_HEADS * ML_QK_PAD

    @pl.when(pl.program_id(1) == 0)
    def _():
        ct_ref[...] = jnp.zeros_like(ct_ref)
        m_ref[...] = jnp.zeros_like(m_ref)
        tail_ref[...] = jnp.zeros_like(tail_ref)

    key_i = lax.broadcasted_iota(jnp.int32, (L, L), 0)
    qry_i = lax.broadcasted_iota(jnp.int32, (L, L), 1)
    tril = jnp.where(qry_i <= key_i, 1.0, 0.0).astype(BF16)
    causal_t = key_i <= qry_i
    wc = wconv_ref[...]

    def prepare(nb):
        x = qk_ref[nb].astype(F32)
        tail = tail_ref[nb]
        row8 = lax.broadcasted_iota(jnp.int32, tail.shape, 0)
        conv = x * wc[ML_CONV - 1:ML_CONV]
        for s in range(1, ML_CONV):
            xs = pltpu.roll(x, s, axis=0)
            head = jnp.where(row8 < s, pltpu.roll(tail, s, axis=0), xs[0:8])
            xs = jnp.concatenate([head, xs[8:]], axis=0)
            conv = conv + xs * wc[ML_CONV - 1 - s:ML_CONV - s]
        tail_ref[nb] = x[L - 8:L]
        act = conv * jax.nn.sigmoid(conv)
        q = act[:, :HQ].astype(BF16)
        k = (act[:, HQ:] * (ML_QK_DIM ** -0.5)).astype(BF16)

        gates = gate_ref[nb] + bias_ref[...]
        lane = lax.broadcasted_iota(jnp.int32, gates.shape, 1)
        log_f = jnp.minimum(gates, 0.0) - jnp.log1p(jnp.exp(-jnp.abs(gates)))
        log_f = jnp.where((lane >= ML_HEADS) & (lane < 2 * ML_HEADS), log_f, 0.0)
        b_cols = sum(jnp.dot(tril, part, preferred_element_type=F32) for part in _split3(log_f))
        return q, k, gates, b_cols, b_cols.T, gates.T

    def head(nb, h, q, k, gates, b_cols, b_rows, g_rows):
        b_row = b_rows[ML_HEADS + h:ML_HEADS + h + 1, :]
        c_row = g_rows[h:h + 1, :] - b_row
        c_col = gates[:, h:h + 1] - b_cols[:, ML_HEADS + h:ML_HEADS + h + 1]
        m_prev = m_ref[nb, h][:, 0:1]
        qh = q[:, h * ML_QK_PAD:(h + 1) * ML_QK_PAD]
        kh = k[:, h * ML_QK_PAD:(h + 1) * ML_QK_PAD]
        vth = vt_ref[nb, h * ML_V_PAD:(h + 1) * ML_V_PAD, :]
        ct_prev = ct_ref[nb, h]

        cm = jnp.where(causal_t, c_col, -jnp.inf)
        m_row = jnp.maximum(m_prev, jnp.max(cm, axis=0, keepdims=True))
        w_t = jnp.exp(cm - m_row)
        a_row = jnp.exp(m_prev - m_row)
        s_t = (lax.dot_general(kh, qh, _NT, preferred_element_type=F32) * w_t).astype(BF16)
        num_t = (jnp.dot(vth, s_t, preferred_element_type=F32)
                 + a_row * lax.dot_general(ct_prev.astype(BF16), qh, _NT, preferred_element_type=F32))
        den = num_t[ML_ONES_ROW:ML_ONES_ROW + 1]
        inv = 1.0 / jnp.maximum(jnp.abs(den), jnp.exp(-(b_row + m_row)))
        hc_t = num_t[:ML_V_DIM] * inv
        ms = jnp.sum(hc_t * hc_t, axis=0, keepdims=True) * (1.0 / ML_V_DIM)
        hn_t = hc_t * lax.rsqrt(ms + EPS) * hnorm_ref[h]
        hn = jnp.concatenate([hn_t, jnp.zeros((ML_V_PAD - ML_V_DIM, L), F32)], axis=0).T
        og = o_ref[nb, :, h * ML_V_PAD:(h + 1) * ML_V_PAD].astype(F32)
        y_ref[nb, :, h * ML_V_PAD:(h + 1) * ML_V_PAD] = (hn * jax.nn.sigmoid(og)).astype(BF16)

        m_end = m_row[:, L - 1:L]
        u_row = jnp.exp(c_row - m_end)
        decay = jnp.exp(m_prev - m_end)
        uv_t = (vth.astype(F32) * u_row).astype(BF16)
        ct_ref[nb, h] = decay * ct_prev + jnp.dot(uv_t, kh, preferred_element_type=F32)
        m_ref[nb, h] = jnp.broadcast_to(b_row[:, L - 1:L] + m_end, (1, LANES))

    prepared = [prepare(nb) for nb in range(ML_NB)]
    for h in range(ML_HEADS):
        for nb in range(ML_NB):
            head(nb, h, *prepared[nb])


def _mlstm(qk, vt, o, gates, wconv, bias, hnorm):
    B, S, _ = qk.shape
    L = ML_CHUNK
    assert B % ML_NB == 0
    tok = lambda w: pl.BlockSpec((ML_NB, L, w), lambda b, c: (b, c, 0))
    return pl.pallas_call(
        _mlstm_kernel,
        grid=(B // ML_NB, S // L),
        in_specs=[tok(qk.shape[2]), pl.BlockSpec((ML_NB, vt.shape[1], L), lambda b, c: (b, 0, c)),
                  tok(o.shape[2]), tok(gates.shape[2]),
                  _const_spec(wconv.shape), _const_spec(bias.shape), _const_spec(hnorm.shape)],
        out_specs=tok(o.shape[2]),
        out_shape=jax.ShapeDtypeStruct(o.shape, BF16),
        scratch_shapes=[pltpu.VMEM((ML_NB, ML_HEADS, ML_V_PAD, ML_QK_PAD), F32),
                        pltpu.VMEM((ML_NB, ML_HEADS, 1, LANES), F32),
                        pltpu.VMEM((ML_NB, 8, qk.shape[2]), F32)],
        compiler_params=_params("arbitrary", "arbitrary"),
        name="mlstm",
    )(qk, vt, o, gates, wconv, bias, hnorm)


def _outproj_kernel(x_ref, y_ref, qm_ref, kt_ref, vm_ref, w_ref, out_ref):
    tm = x_ref.shape[1]
    halves = [slice(r0, r0 + tm // 2) for r0 in range(0, tm, tm // 2)]
    scores = [[jnp.dot(qm_ref[0, rows, :], kt_ref[0, h], preferred_element_type=F32)
               for h in range(MEM_HEADS)] for rows in halves]
    for rows, s_heads in zip(halves, scores):
        ymem = None
        for h in range(MEM_HEADS):
            s = s_heads[h]
            e = jnp.exp(s - jnp.max(s, axis=-1, keepdims=True))
            p = (e * (1.0 / jnp.sum(e, axis=-1, keepdims=True))).astype(BF16)
            oh = jnp.dot(p, vm_ref[0, h], preferred_element_type=F32)
            ymem = oh if ymem is None else ymem + oh
        ycat = jnp.concatenate([y_ref[0, rows, :], ymem.astype(BF16)], axis=1)
        out_ref[0, rows, :] = x_ref[0, rows, :] + jnp.dot(ycat, w_ref[...], preferred_element_type=F32)


def _outproj(x, y, qm, kt, vm, w):
    B, S, D = x.shape
    tm = min(TM_PROJ, S)
    tok = lambda w: pl.BlockSpec((1, tm, w), lambda b, i: (b, i, 0))
    per_b = pl.BlockSpec((1,) + kt.shape[1:], lambda b, i: (b, 0, 0, 0))
    return pl.pallas_call(
        _outproj_kernel,
        grid=(B, S // tm),
        in_specs=[tok(D), tok(y.shape[2]), tok(qm.shape[2]), per_b, per_b,
                  _const_spec(w.shape)],
        out_specs=tok(D),
        out_shape=jax.ShapeDtypeStruct(x.shape, F32),
        compiler_params=_params("arbitrary", "arbitrary"),
        name="outproj",
    )(x, y, qm, kt, vm, w)


def _ffn_kernel(x_ref, g_ref, w1_ref, w2_ref, gf_ref, out_ref, *, final_norm):
    x = x_ref[0]
    hn = _rms(x, g_ref[...]).astype(BF16)
    acc = x
    for c in range(D_FF // FF_CHUNK):
        a = jnp.dot(hn, w1_ref[:, c * FF_CHUNK:(c + 1) * FF_CHUNK], preferred_element_type=F32)
        a = jnp.square(jnp.maximum(a, 0.0)).astype(BF16)
        acc = acc + jnp.dot(a, w2_ref[c * FF_CHUNK:(c + 1) * FF_CHUNK, :], preferred_element_type=F32)
    if final_norm:
        acc = _rms(acc, gf_ref[...])
    out_ref[0] = acc


def _ffn(x, g, w1, w2, gf, final_norm):
    B, S, D = x.shape
    tm = min(TM_PROJ, S)
    tok = pl.BlockSpec((1, tm, D), lambda b, i: (b, i, 0))
    return pl.pallas_call(
        functools.partial(_ffn_kernel, final_norm=final_norm),
        grid=(B, S // tm),
        in_specs=[tok, _const_spec(g.shape), _const_spec(w1.shape), _const_spec(w2.shape),
                  _const_spec(gf.shape)],
        out_specs=tok,
        out_shape=jax.ShapeDtypeStruct(x.shape, F32),
        compiler_params=_params("arbitrary", "arbitrary"),
        name="ffn",
    )(x, g, w1, w2, gf)


def _rope_kernel(pos_ref, inv_ref, cos_ref, sin_ref):
    ang = pos_ref[0].astype(F32) * inv_ref[...]
    cos_ref[0] = jnp.cos(ang)
    sin_ref[0] = jnp.sin(ang)


def _rope_tables(positions):
    B, S = positions.shape
    nf = MLA_ROPE // 2
    inv = (ROPE_THETA ** (-jnp.arange(0, MLA_ROPE, 2, dtype=F32) / MLA_ROPE)).reshape(nf, 1)
    out = jax.ShapeDtypeStruct((B, nf, S), F32)
    spec = pl.BlockSpec((1, nf, S), lambda b: (b, 0, 0))
    return pl.pallas_call(
        _rope_kernel,
        grid=(B,),
        in_specs=[pl.BlockSpec((1, 1, S), lambda b: (b, 0, 0)), _const_spec(inv.shape)],
        out_specs=[spec, spec],
        out_shape=[out, out],
        compiler_params=_params("arbitrary"),
        name="rope_tables",
    )(positions.reshape(B, 1, S), inv)


def _inproj1_kernel(x_ref, g_ref, win_ref, wkrt_ref, gq_ref, gkv_ref, wuqt_ref, wuk_ref, wuvt_ref,
                    cos_ref, sin_ref, qt_ref, k_ref, vt_ref, qm_ref):
    half = MLA_ROPE // 2
    xn = _rms(x_ref[0], g_ref[...]).astype(BF16)
    c = jnp.dot(xn, win_ref[...], preferred_element_type=F32)
    o_kv = MLA_Q_RANK
    o_qm = o_kv + MLA_KV_RANK
    qm_ref[0] = c[:, o_qm:].astype(BF16)
    cq = _rms(c[:, :o_kv], gq_ref[...]).astype(BF16)
    ckv = _rms(c[:, o_kv:o_qm], gkv_ref[...]).astype(BF16)
    cos_t, sin_t = cos_ref[0], sin_ref[0]

    def rope_t(x1, x2):
        return x1 * cos_t - x2 * sin_t, x2 * cos_t + x1 * sin_t

    qscale = ((MLA_NOPE + MLA_ROPE) ** -0.5) * math.log2(math.e)
    qt = lax.dot_general(wuqt_ref[...], cq, _NT, preferred_element_type=F32)
    zpad = jnp.zeros((MLA_QK_PAD - MLA_NOPE - MLA_ROPE, qt.shape[1]), F32)
    for h in range(MLA_HEADS):
        b0 = h * MLA_QK_PAD
        r1, r2 = rope_t(qt[b0 + MLA_NOPE:b0 + MLA_NOPE + half],
                        qt[b0 + MLA_NOPE + half:b0 + MLA_NOPE + MLA_ROPE])
        blk = jnp.concatenate([qt[b0:b0 + MLA_NOPE], r1, r2, zpad], axis=0) * qscale
        qt_ref[0, b0:b0 + MLA_QK_PAD] = blk.astype(BF16)

    krt = lax.dot_general(wkrt_ref[...], xn, _NT, preferred_element_type=F32)
    r1, r2 = rope_t(krt[MLA_NOPE:MLA_NOPE + half], krt[MLA_NOPE + half:MLA_NOPE + MLA_ROPE])
    kr = jnp.concatenate([krt[:MLA_NOPE], r1, r2, krt[MLA_NOPE + MLA_ROPE:]], axis=0).T
    k = jnp.dot(ckv, wuk_ref[...], preferred_element_type=F32)
    for h in range(MLA_HEADS):
        sl = slice(h * MLA_QK_PAD, (h + 1) * MLA_QK_PAD)
        k_ref[0, :, sl] = (k[:, sl] + kr).astype(BF16)

    vt = lax.dot_general(wuvt_ref[...], ckv, _NT, preferred_element_type=F32)
    rowid = lax.broadcasted_iota(jnp.int32, vt.shape, 0) % MLA_VT_ROWS
    vt_ref[0, 0] = jnp.where(rowid == MLA_V, 1.0, vt).astype(BF16)


def _inproj1(x, g, win, wkrt, gq, gkv, wuqt, wuk, wuvt, cos_t, sin_t):
    B, S, D = x.shape
    tm = TQ
    assert S % tm == 0
    tok = lambda w: pl.BlockSpec((1, tm, w), lambda b, i: (b, i, 0))
    tok_t = lambda r: pl.BlockSpec((1, r, tm), lambda b, i: (b, 0, i))
    consts = (g, win, wkrt, gq, gkv, wuqt, wuk, wuvt)
    nvt = wuvt.shape[0]
    return pl.pallas_call(
        _inproj1_kernel,
        grid=(B, S // tm),
        in_specs=[tok(D)] + [_const_spec(a.shape) for a in consts] + [tok_t(cos_t.shape[1])] * 2,
        out_specs=[tok_t(wuqt.shape[0]), tok(wuk.shape[1]),
                   pl.BlockSpec((1, 1, nvt, tm), lambda b, i: (b, i, 0, 0)), tok(MEM_WIDTH)],
        out_shape=[jax.ShapeDtypeStruct((B, wuqt.shape[0], S), BF16),
                   jax.ShapeDtypeStruct((B, S, wuk.shape[1]), BF16),
                   jax.ShapeDtypeStruct((B, S // tm, nvt, tm), BF16),
                   jax.ShapeDtypeStruct((B, S, MEM_WIDTH), BF16)],
        compiler_params=_params("arbitrary", "arbitrary"),
        name="inproj1",
    )(x, *consts, cos_t, sin_t)


def _flash_kernel(qt_ref, k_ref, vt_ref, y_ref, sa_ref, sb_ref, smax_ref, acc_ref, m_ref):
    i = pl.program_id(2)
    slots = (sa_ref, sb_ref)

    def scores_t(hh, blk):
        kh = k_ref[0, pl.ds(pl.multiple_of(blk * TQ, TQ), TQ), hh * MLA_QK_PAD:(hh + 1) * MLA_QK_PAD]
        qh = qt_ref[0, hh * MLA_QK_PAD:(hh + 1) * MLA_QK_PAD, :]
        return jnp.dot(kh, qh, preferred_element_type=F32)

    def v_t(hh, blk):
        return vt_ref[0, blk, hh * MLA_VT_ROWS:(hh + 1) * MLA_VT_ROWS, :]

    def col_max(s):
        parts = [jnp.max(s[r:r + 128], axis=0, keepdims=True) for r in range(0, TQ, 128)]
        return jnp.maximum(jnp.maximum(parts[0], parts[1]), jnp.maximum(parts[2], parts[3]))

    key_i = lax.broadcasted_iota(jnp.int32, (TQ, TQ), 0)
    qry_i = lax.broadcasted_iota(jnp.int32, (TQ, TQ), 1)

    def qk_stage(slot, blk, diagonal=False):
        for hh in range(2):
            s = scores_t(hh, blk)
            if diagonal:
                s = jnp.where(key_i <= qry_i, s, -jnp.inf)
            slots[slot][hh] = s
            smax_ref[slot, hh] = col_max(s)

    def pv_stage(slot, blk):
        for hh in range(2):
            m_prev = m_ref[hh]
            m_new = jnp.maximum(m_prev, smax_ref[slot, hh])
            alpha = jnp.exp2(m_prev - m_new)
            p = jnp.exp2(slots[slot][hh] - m_new).astype(BF16)
            acc_ref[hh] = alpha * acc_ref[hh] + jnp.dot(v_t(hh, blk), p, preferred_element_type=F32)
            m_ref[hh] = m_new

    m_ref[...] = jnp.full(m_ref.shape, -jnp.inf, F32)
    acc_ref[...] = jnp.zeros_like(acc_ref)

    @pl.when(i == 0)
    def _():
        qk_stage(0, 0, diagonal=True)
        pv_stage(0, 0)

    @pl.when(i >= 1)
    def _():
        qk_stage(0, 0)

    def body(t, carry):
        qk_stage(1, 2 * t + 1)
        pv_stage(0, 2 * t)
        qk_stage(0, 2 * t + 2)
        pv_stage(1, 2 * t + 1)
        return carry

    lax.fori_loop(0, (i - 1) // 2, body, 0)

    @pl.when(i % 2 == 1)
    def _():
        qk_stage(1, i, diagonal=True)
        pv_stage(0, i - 1)
        pv_stage(1, i)

    @pl.when((i % 2 == 0) & (i >= 2))
    def _():
        qk_stage(1, i - 1)
        pv_stage(0, i - 2)
        qk_stage(0, i, diagonal=True)
        pv_stage(1, i - 1)
        pv_stage(0, i)

    outs = []
    for hh in range(2):
        acc = acc_ref[hh]
        outs.append(acc[:MLA_V] * (1.0 / acc[MLA_V:MLA_V + 1]))
    y_ref[0] = jnp.concatenate(outs, axis=0).T.astype(BF16)


def _flash(qt, k, vt):
    B, S, _ = k.shape
    assert S % TQ == 0
    pairs = MLA_HEADS // 2
    return pl.pallas_call(
        _flash_kernel,
        grid=(B, pairs, S // TQ),
        in_specs=[pl.BlockSpec((1, 2 * MLA_QK_PAD, TQ), lambda b, p, i: (b, p, i)),
                  pl.BlockSpec((1, S, 2 * MLA_QK_PAD), lambda b, p, i: (b, 0, p)),
                  pl.BlockSpec((1, S // TQ, 2 * MLA_VT_ROWS, TQ), lambda b, p, i: (b, 0, p, 0))],
        out_specs=pl.BlockSpec((1, TQ, 2 * MLA_V), lambda b, p, i: (b, i, p)),
        out_shape=jax.ShapeDtypeStruct((B, S, MLA_HEADS * MLA_V), BF16),
        scratch_shapes=[pltpu.VMEM((2, TQ, TQ), F32),
                        pltpu.VMEM((2, TQ, TQ), F32),
                        pltpu.VMEM((2, 2, 1, TQ), F32),
                        pltpu.VMEM((2, MLA_VT_ROWS, TQ), F32),
                        pltpu.VMEM((2, 1, TQ), F32)],
        compiler_params=_params("arbitrary", "arbitrary", "arbitrary"),
        name="mla_flash",
    )(qt, k, vt)


def _pad_heads(w, heads, d, dp):
    lead = w.shape[:-1]
    w = w.reshape(lead + (heads, d))
    w = jnp.pad(w, [(0, 0)] * len(lead) + [(0, 0), (0, dp - d)])
    return w.reshape(lead + (heads * dp,))


def _pad_head_rows(w, heads, d, dp):
    return _pad_heads(w.T, heads, d, dp).T


def kernel(x, mem, positions, mem_norm, w_mem_kv, norm_mix0, w_in0, b_igate0, b_fgate0, w_conv0, w_hnorm0, w_out0, norm_ffn0, w_ff1_0, w_ff2_0, norm_mix1, w_in1, w_qnorm1, w_uq1, w_kvnorm1, w_ukv1, w_out1, norm_ffn1, w_ff1_1, w_ff2_1, final_norm):
    row = lambda g: g.reshape(1, -1).astype(F32)

    kt, vm = _mem_kv(mem, row(mem_norm), w_mem_kv[:, :MEM_WIDTH].T.astype(BF16),
                     w_mem_kv[:, MEM_WIDTH:].astype(BF16))

    nq = ML_HEADS * ML_QK_DIM
    o_v, o_o, o_g = 2 * nq, 2 * nq + MIX_WIDTH, 2 * nq + 2 * MIX_WIDTH
    o_qm = o_g + 2 * ML_HEADS
    wqk = jnp.concatenate([_pad_heads(w_in0[:, :nq], ML_HEADS, ML_QK_DIM, ML_QK_PAD),
                           _pad_heads(w_in0[:, nq:o_v], ML_HEADS, ML_QK_DIM, ML_QK_PAD)], axis=1)
    wvt0 = _pad_heads(w_in0[:, o_v:o_o], ML_HEADS, ML_V_DIM, ML_V_PAD).T
    wo0 = _pad_heads(w_in0[:, o_o:o_g], ML_HEADS, ML_V_DIM, ML_V_PAD)
    wg0 = jnp.pad(w_in0[:, o_g:o_qm], ((0, 0), (0, LANES - 2 * ML_HEADS)))
    wqm0 = w_in0[:, o_qm:]
    wconv = jnp.concatenate([_pad_heads(w_conv0[:, :nq], ML_HEADS, ML_QK_DIM, ML_QK_PAD),
                             _pad_heads(w_conv0[:, nq:], ML_HEADS, ML_QK_DIM, ML_QK_PAD)], axis=1).astype(F32)
    gate_bias = jnp.pad(jnp.concatenate([b_igate0, b_fgate0]), (0, LANES - 2 * ML_HEADS)).reshape(1, LANES).astype(F32)
    hnorm = jnp.broadcast_to(w_hnorm0.astype(F32)[:, :, None], (ML_HEADS, ML_V_DIM, LANES))
    wout0 = jnp.concatenate([_pad_head_rows(w_out0[:MIX_WIDTH], ML_HEADS, ML_V_DIM, ML_V_PAD),
                             w_out0[MIX_WIDTH:]], axis=0).astype(BF16)

    qk, vt0, o0, qm0, gates = _inproj0(x, row(norm_mix0), wqk.astype(BF16), wvt0.astype(BF16),
                                       wo0.astype(BF16), wqm0.astype(BF16), wg0.astype(BF16))
    y0 = _mlstm(qk, vt0, o0, gates, wconv, gate_bias, hnorm)
    x = _outproj(x, y0, qm0, kt, vm, wout0)
    x = _ffn(x, row(norm_ffn0), w_ff1_0.astype(BF16), w_ff2_0.astype(BF16), row(final_norm), False)

    o_kr = MLA_Q_RANK + MLA_KV_RANK
    o_qm1 = o_kr + MLA_ROPE
    win1 = jnp.concatenate([w_in1[:, :o_kr], w_in1[:, o_qm1:]], axis=1)
    wkrt = jnp.pad(w_in1[:, o_kr:o_qm1].T, ((MLA_NOPE, LANES - MLA_NOPE - MLA_ROPE), (0, 0)))
    wuqt = _pad_heads(w_uq1, MLA_HEADS, MLA_NOPE + MLA_ROPE, MLA_QK_PAD).T
    ukv = w_ukv1.reshape(MLA_KV_RANK, MLA_HEADS, MLA_NOPE + MLA_V)
    wuk = _pad_heads(ukv[:, :, :MLA_NOPE].reshape(MLA_KV_RANK, -1), MLA_HEADS, MLA_NOPE, MLA_QK_PAD)
    wuvt = _pad_heads(ukv[:, :, MLA_NOPE:].reshape(MLA_KV_RANK, -1), MLA_HEADS, MLA_V, MLA_VT_ROWS).T

    cos_t, sin_t = _rope_tables(positions)
    qt1, k1, vt1, qm1 = _inproj1(x, row(norm_mix1), win1.astype(BF16), wkrt.astype(BF16),
                                 row(w_qnorm1), row(w_kvnorm1), wuqt.astype(BF16), wuk.astype(BF16),
                                 wuvt.astype(BF16), cos_t, sin_t)
    y1 = _flash(qt1, k1, vt1)
    x = _outproj(x, y1, qm1, kt, vm, w_out1.astype(BF16))
    return _ffn(x, row(norm_ffn1), w_ff1_1.astype(BF16), w_ff2_1.astype(BF16), row(final_norm), True)
```

```python
import functools
import math

import jax
import jax.numpy as jnp
from jax import lax
from jax.experimental import pallas as pl
from jax.experimental.pallas import tpu as pltpu

F32 = jnp.float32
BF16 = jnp.bfloat16
EPS = 1e-6

D_MODEL = 1024
N_MEM = 256
MEM_HEADS = 4
MEM_HEAD_DIM = 64
MEM_WIDTH = MEM_HEADS * MEM_HEAD_DIM
MIX_WIDTH = D_MODEL - MEM_WIDTH

ML_HEADS = 4
ML_V_DIM = MIX_WIDTH // ML_HEADS
ML_QK_DIM = ML_V_DIM // 2
ML_CONV = 4
ML_CHUNK = 128
ML_QK_PAD = 128
ML_VT_ROWS = 256
ML_ONES_ROW = ML_V_DIM
ML_NB = 4

MLA_HEADS = 12
MLA_NOPE = 64
MLA_ROPE = 32
MLA_V = 64
MLA_Q_RANK = 384
MLA_KV_RANK = 256
MLA_QK_PAD = 128
MLA_VT_ROWS = 80
ROPE_THETA = 10000.0
D_FF = 4 * D_MODEL

LANES = 128
VMEM_LIMIT = 56 * 1024 * 1024

TM_PROJ = 512
FF_CHUNK = 1024
TQ = 512

_NT = (((1,), (1,)), ((), ()))


def _params(*sem):
    return pltpu.CompilerParams(dimension_semantics=sem, vmem_limit_bytes=VMEM_LIMIT)


def _rms(x, g):
    return x * lax.rsqrt(jnp.mean(x * x, axis=-1, keepdims=True) + EPS) * g


def _const_spec(shape):
    nd = len(shape)
    return pl.BlockSpec(shape, lambda *_: (0,) * nd)


def _mem_kv_kernel(mem_ref, g_ref, wkt_ref, wv_ref, kt_ref, v_ref):
    xn = _rms(mem_ref[0], g_ref[...]).astype(BF16)
    kt = lax.dot_general(wkt_ref[...], xn, _NT, preferred_element_type=F32)
    v = jnp.dot(xn, wv_ref[...], preferred_element_type=F32)
    kt = kt * (MEM_HEAD_DIM ** -0.5)
    row_head = lax.broadcasted_iota(jnp.int32, kt.shape, 0) // MEM_HEAD_DIM
    col_head = lax.broadcasted_iota(jnp.int32, v.shape, 1) // MEM_HEAD_DIM
    for h in range(MEM_HEADS):
        kt_ref[0, h] = jnp.where(row_head == h, kt, 0.0).astype(BF16)
        v_ref[0, h] = jnp.where(col_head == h, v, 0.0).astype(BF16)


def _mem_kv(mem, g, wkt, wv):
    B = mem.shape[0]
    out = jax.ShapeDtypeStruct((B, MEM_HEADS, MEM_WIDTH, N_MEM), BF16)
    return pl.pallas_call(
        _mem_kv_kernel,
        grid=(B,),
        in_specs=[pl.BlockSpec((1, N_MEM, D_MODEL), lambda b: (b, 0, 0)),
                  _const_spec(g.shape), _const_spec(wkt.shape), _const_spec(wv.shape)],
        out_specs=[pl.BlockSpec((1, MEM_HEADS, MEM_WIDTH, N_MEM), lambda b: (b, 0, 0, 0)),
                   pl.BlockSpec((1, MEM_HEADS, N_MEM, MEM_WIDTH), lambda b: (b, 0, 0, 0))],
        out_shape=[out, out],
        compiler_params=_params("arbitrary"),
        name="mem_kv",
    )(mem, g, wkt, wv)


def _inproj0_kernel(x_ref, g_ref, wqk_ref, wv_ref, wo_ref, wqm_ref, wg_ref,
                    qk_ref, v_ref, o_ref, qm_ref, gate_ref):
    xn = _rms(x_ref[0], g_ref[...]).astype(BF16)
    qk_ref[0] = jnp.dot(xn, wqk_ref[...], preferred_element_type=F32).astype(BF16)
    vt = lax.dot_general(wv_ref[...], xn, _NT, preferred_element_type=F32)
    rowid = lax.broadcasted_iota(jnp.int32, vt.shape, 0) % ML_VT_ROWS
    v_ref[0] = jnp.where(rowid == ML_ONES_ROW, 1.0, vt).astype(BF16)
    o_ref[0] = lax.dot_general(wo_ref[...], xn, _NT, preferred_element_type=F32).astype(BF16)
    qm_ref[0] = jnp.dot(xn, wqm_ref[...], preferred_element_type=F32).astype(BF16)
    gate_ref[0] = jnp.dot(xn, wg_ref[...], preferred_element_type=F32)


def _inproj0(x, g, wqk, wvt, wot, wqm, wg):
    B, S, D = x.shape
    tm = min(TM_PROJ, S)
    tok = lambda w: pl.BlockSpec((1, tm, w), lambda b, i: (b, i, 0))
    tok_t = lambda r: pl.BlockSpec((1, r, tm), lambda b, i: (b, 0, i))
    consts = (g, wqk, wvt, wot, wqm, wg)
    return pl.pallas_call(
        _inproj0_kernel,
        grid=(B, S // tm),
        in_specs=[tok(D)] + [_const_spec(a.shape) for a in consts],
        out_specs=[tok(wqk.shape[1]), tok_t(wvt.shape[0]), tok_t(wot.shape[0]), tok(wqm.shape[1]),
                   tok(wg.shape[1])],
        out_shape=[jax.ShapeDtypeStruct((B, S, wqk.shape[1]), BF16),
                   jax.ShapeDtypeStruct((B, wvt.shape[0], S), BF16),
                   jax.ShapeDtypeStruct((B, wot.shape[0], S), BF16),
                   jax.ShapeDtypeStruct((B, S, wqm.shape[1]), BF16),
                   jax.ShapeDtypeStruct((B, S, wg.shape[1]), F32)],
        compiler_params=_params("arbitrary", "arbitrary"),
        name="inproj0",
    )(x, *consts)


def _split3(x):
    hi = x.astype(BF16)
    r = x - hi.astype(F32)
    mid = r.astype(BF16)
    lo = (r - mid.astype(F32)).astype(BF16)
    return hi, mid, lo


def _mlstm_kernel(qk_ref, vt_ref, o_ref, gate_ref, wconv_ref, bias_ref, hnorm_ref,
                  y_ref, ct_ref, m_ref, tail_ref):
    L = ML_CHUNK
    HQ = ML_HEADS * ML_QK_PAD

    @pl.when(pl.program_id(1) == 0)
    def _():
        ct_ref[...] = jnp.zeros_like(ct_ref)
        m_ref[...] = jnp.zeros_like(m_ref)
        tail_ref[...] = jnp.zeros_like(tail_ref)

    key_i = lax.broadcasted_iota(jnp.int32, (L, L), 0)
    qry_i = lax.broadcasted_iota(jnp.int32, (L, L), 1)
    tril = jnp.where(qry_i <= key_i, 1.0, 0.0).astype(BF16)
    causal_t = key_i <= qry_i
    wc = wconv_ref[...]

    def prepare(nb):
        x = qk_ref[nb].astype(F32)
        tail = tail_ref[nb]
        row8 = lax.broadcasted_iota(jnp.int32, tail.shape, 0)
        conv = x * wc[ML_CONV - 1:ML_CONV]
        for s in range(1, ML_CONV):
            xs = pltpu.roll(x, s, axis=0)
            head = jnp.where(row8 < s, pltpu.roll(tail, s, axis=0), xs[0:8])
            xs = jnp.concatenate([head, xs[8:]], axis=0)
            conv = conv + xs * wc[ML_CONV - 1 - s:ML_CONV - s]
        tail_ref[nb] = x[L - 8:L]
        act = conv * jax.nn.sigmoid(conv)
        q = act[:, :HQ].astype(BF16)
        k = (act[:, HQ:] * (ML_QK_DIM ** -0.5)).astype(BF16)

        gates = gate_ref[nb] + bias_ref[...]
        lane = lax.broadcasted_iota(jnp.int32, gates.shape, 1)
        log_f = jnp.minimum(gates, 0.0) - jnp.log1p(jnp.exp(-jnp.abs(gates)))
        log_f = jnp.where((lane >= ML_HEADS) & (lane < 2 * ML_HEADS), log_f, 0.0)
        b_cols = sum(jnp.dot(tril, part, preferred_element_type=F32) for part in _split3(log_f))
        return q, k, gates, b_cols, b_cols.T, gates.T

    def head(nb, h, q, k, gates, b_cols, b_rows, g_rows):
        b_row = b_rows[ML_HEADS + h:ML_HEADS + h + 1, :]
        c_row = g_rows[h:h + 1, :] - b_row
        c_col = gates[:, h:h + 1] - b_cols[:, ML_HEADS + h:ML_HEADS + h + 1]
        m_prev = m_ref[nb, h][:, 0:1]
        qh = q[:, h * ML_QK_PAD:(h + 1) * ML_QK_PAD]
        kh = k[:, h * ML_QK_PAD:(h + 1) * ML_QK_PAD]
        vth = vt_ref[nb, h * ML_VT_ROWS:(h + 1) * ML_VT_ROWS, :]
        ct_prev = ct_ref[nb, h]

        cm = jnp.where(causal_t, c_col, -jnp.inf)
        m_row = jnp.maximum(m_prev, jnp.max(cm, axis=0, keepdims=True))
        w_t = jnp.exp(cm - m_row)
        a_row = jnp.exp(m_prev - m_row)
        s_t = (lax.dot_general(kh, qh, _NT, preferred_element_type=F32) * w_t).astype(BF16)
        num_t = (jnp.dot(vth, s_t, preferred_element_type=F32)
                 + a_row * lax.dot_general(ct_prev.astype(BF16), qh, _NT, preferred_element_type=F32))
        den = num_t[ML_ONES_ROW:ML_ONES_ROW + 1]
        inv = 1.0 / jnp.maximum(jnp.abs(den), jnp.exp(-(b_row + m_row)))
        hc_t = num_t[:ML_V_DIM] * inv
        ms = jnp.sum(hc_t * hc_t, axis=0, keepdims=True) * (1.0 / ML_V_DIM)
        hn_t = hc_t * lax.rsqrt(ms + EPS) * hnorm_ref[h]
        og_t = o_ref[nb, h * ML_V_DIM:(h + 1) * ML_V_DIM, :].astype(F32)
        y_ref[nb, h * ML_V_DIM:(h + 1) * ML_V_DIM, :] = (hn_t * jax.nn.sigmoid(og_t)).astype(BF16)

        m_end = m_row[:, L - 1:L]
        u_row = jnp.exp(c_row - m_end)
        decay = jnp.exp(m_prev - m_end)
        uv_t = (vth.astype(F32) * u_row).astype(BF16)
        ct_ref[nb, h] = decay * ct_prev + jnp.dot(uv_t, kh, preferred_element_type=F32)
        m_ref[nb, h] = jnp.broadcast_to(b_row[:, L - 1:L] + m_end, (1, LANES))

    prepared = [prepare(nb) for nb in range(ML_NB)]
    for h in range(ML_HEADS):
        for nb in range(ML_NB):
            head(nb, h, *prepared[nb])


def _mlstm(qk, vt, ot, gates, wconv, bias, hnorm):
    B, S, _ = qk.shape
    L = ML_CHUNK
    assert B % ML_NB == 0
    tok = lambda w: pl.BlockSpec((ML_NB, L, w), lambda b, c: (b, c, 0))
    tok_t = lambda r: pl.BlockSpec((ML_NB, r, L), lambda b, c: (b, 0, c))
    return pl.pallas_call(
        _mlstm_kernel,
        grid=(B // ML_NB, S // L),
        in_specs=[tok(qk.shape[2]), tok_t(vt.shape[1]), tok_t(ot.shape[1]), tok(gates.shape[2]),
                  _const_spec(wconv.shape), _const_spec(bias.shape), _const_spec(hnorm.shape)],
        out_specs=tok_t(ot.shape[1]),
        out_shape=jax.ShapeDtypeStruct(ot.shape, BF16),
        scratch_shapes=[pltpu.VMEM((ML_NB, ML_HEADS, ML_VT_ROWS, ML_QK_PAD), F32),
                        pltpu.VMEM((ML_NB, ML_HEADS, 1, LANES), F32),
                        pltpu.VMEM((ML_NB, 8, qk.shape[2]), F32)],
        compiler_params=_params("arbitrary", "arbitrary"),
        name="mlstm",
    )(qk, vt, ot, gates, wconv, bias, hnorm)


def _outproj_kernel(x_ref, y_ref, qm_ref, kt_ref, vm_ref, w_ref, out_ref, *, y_transposed):
    tm = x_ref.shape[1]
    halves = [slice(r0, r0 + tm // 2) for r0 in range(0, tm, tm // 2)]
    scores = [[jnp.dot(qm_ref[0, rows, :], kt_ref[0, h], preferred_element_type=F32)
               for h in range(MEM_HEADS)] for rows in halves]
    for rows, s_heads in zip(halves, scores):
        ymem = None
        for h in range(MEM_HEADS):
            s = s_heads[h]
            e = jnp.exp(s - jnp.max(s, axis=-1, keepdims=True))
            p = (e * (1.0 / jnp.sum(e, axis=-1, keepdims=True))).astype(BF16)
            oh = jnp.dot(p, vm_ref[0, h], preferred_element_type=F32)
            ymem = oh if ymem is None else ymem + oh
        y = y_ref[0, :, rows].T if y_transposed else y_ref[0, rows, :]
        ycat = jnp.concatenate([y, ymem.astype(BF16)], axis=1)
        out_ref[0, rows, :] = x_ref[0, rows, :] + jnp.dot(ycat, w_ref[...], preferred_element_type=F32)


def _outproj(x, y, qm, kt, vm, w, y_transposed):
    B, S, D = x.shape
    tm = min(TM_PROJ, S)
    tok = lambda w: pl.BlockSpec((1, tm, w), lambda b, i: (b, i, 0))
    y_spec = pl.BlockSpec((1, y.shape[1], tm), lambda b, i: (b, 0, i)) if y_transposed else tok(y.shape[2])
    per_b = pl.BlockSpec((1,) + kt.shape[1:], lambda b, i: (b, 0, 0, 0))
    return pl.pallas_call(
        functools.partial(_outproj_kernel, y_transposed=y_transposed),
        grid=(B, S // tm),
        in_specs=[tok(D), y_spec, tok(qm.shape[2]), per_b, per_b,
                  _const_spec(w.shape)],
        out_specs=tok(D),
        out_shape=jax.ShapeDtypeStruct(x.shape, F32),
        compiler_params=_params("arbitrary", "arbitrary"),
        name="outproj",
    )(x, y, qm, kt, vm, w)


def _ffn_kernel(x_ref, g_ref, w1_ref, w2_ref, gf_ref, out_ref, *, final_norm):
    x = x_ref[0]
    hn = _rms(x, g_ref[...]).astype(BF16)
    acc = x
    for c in range(D_FF // FF_CHUNK):
        a = jnp.dot(hn, w1_ref[:, c * FF_CHUNK:(c + 1) * FF_CHUNK], preferred_element_type=F32)
        a = jnp.square(jnp.maximum(a, 0.0)).astype(BF16)
        acc = acc + jnp.dot(a, w2_ref[c * FF_CHUNK:(c + 1) * FF_CHUNK, :], preferred_element_type=F32)
    if final_norm:
        acc = _rms(acc, gf_ref[...])
    out_ref[0] = acc


def _ffn(x, g, w1, w2, gf, final_norm):
    B, S, D = x.shape
    tm = min(TM_PROJ, S)
    tok = pl.BlockSpec((1, tm, D), lambda b, i: (b, i, 0))
    return pl.pallas_call(
        functools.partial(_ffn_kernel, final_norm=final_norm),
        grid=(B, S // tm),
        in_specs=[tok, _const_spec(g.shape), _const_spec(w1.shape), _const_spec(w2.shape),
                  _const_spec(gf.shape)],
        out_specs=tok,
        out_shape=jax.ShapeDtypeStruct(x.shape, F32),
        compiler_params=_params("arbitrary", "arbitrary"),
        name="ffn",
    )(x, g, w1, w2, gf)


def _rope_kernel(pos_ref, inv_ref, cos_ref, sin_ref):
    ang = pos_ref[0].astype(F32) * inv_ref[...]
    cos_ref[0] = jnp.cos(ang)
    sin_ref[0] = jnp.sin(ang)


def _rope_tables(positions):
    B, S = positions.shape
    nf = MLA_ROPE // 2
    inv = (ROPE_THETA ** (-jnp.arange(0, MLA_ROPE, 2, dtype=F32) / MLA_ROPE)).reshape(nf, 1)
    out = jax.ShapeDtypeStruct((B, nf, S), F32)
    spec = pl.BlockSpec((1, nf, S), lambda b: (b, 0, 0))
    return pl.pallas_call(
        _rope_kernel,
        grid=(B,),
        in_specs=[pl.BlockSpec((1, 1, S), lambda b: (b, 0, 0)), _const_spec(inv.shape)],
        out_specs=[spec, spec],
        out_shape=[out, out],
        compiler_params=_params("arbitrary"),
        name="rope_tables",
    )(positions.reshape(B, 1, S), inv)


def _inproj1_kernel(x_ref, g_ref, win_ref, wkrt_ref, gq_ref, gkv_ref, wuqt_ref, wuk_ref, wuvt_ref,
                    cos_ref, sin_ref, qt_ref, k_ref, vt_ref, qm_ref):
    half = MLA_ROPE // 2
    xn = _rms(x_ref[0], g_ref[...]).astype(BF16)
    c = jnp.dot(xn, win_ref[...], preferred_element_type=F32)
    o_kv = MLA_Q_RANK
    o_qm = o_kv + MLA_KV_RANK
    qm_ref[0] = c[:, o_qm:].astype(BF16)
    cq = _rms(c[:, :o_kv], gq_ref[...]).astype(BF16)
    ckv = _rms(c[:, o_kv:o_qm], gkv_ref[...]).astype(BF16)
    cos_t, sin_t = cos_ref[0], sin_ref[0]

    def rope_t(x1, x2):
        return x1 * cos_t - x2 * sin_t, x2 * cos_t + x1 * sin_t

    qscale = ((MLA_NOPE + MLA_ROPE) ** -0.5) * math.log2(math.e)
    qt = lax.dot_general(wuqt_ref[...], cq, _NT, preferred_element_type=F32)
    zpad = jnp.zeros((MLA_QK_PAD - MLA_NOPE - MLA_ROPE, qt.shape[1]), F32)
    for h in range(MLA_HEADS):
        b0 = h * MLA_QK_PAD
        r1, r2 = rope_t(qt[b0 + MLA_NOPE:b0 + MLA_NOPE + half],
                        qt[b0 + MLA_NOPE + half:b0 + MLA_NOPE + MLA_ROPE])
        blk = jnp.concatenate([qt[b0:b0 + MLA_NOPE], r1, r2, zpad], axis=0) * qscale
        qt_ref[0, b0:b0 + MLA_QK_PAD] = blk.astype(BF16)

    krt = lax.dot_general(wkrt_ref[...], xn, _NT, preferred_element_type=F32)
    r1, r2 = rope_t(krt[MLA_NOPE:MLA_NOPE + half], krt[MLA_NOPE + half:MLA_NOPE + MLA_ROPE])
    kr = jnp.concatenate([krt[:MLA_NOPE], r1, r2, krt[MLA_NOPE + MLA_ROPE:]], axis=0).T
    k = jnp.dot(ckv, wuk_ref[...], preferred_element_type=F32)
    for h in range(MLA_HEADS):
        sl = slice(h * MLA_QK_PAD, (h + 1) * MLA_QK_PAD)
        k_ref[0, :, sl] = (k[:, sl] + kr).astype(BF16)

    vt = lax.dot_general(wuvt_ref[...], ckv, _NT, preferred_element_type=F32)
    rowid = lax.broadcasted_iota(jnp.int32, vt.shape, 0) % MLA_VT_ROWS
    vt_ref[0, 0] = jnp.where(rowid == MLA_V, 1.0, vt).astype(BF16)


def _inproj1(x, g, win, wkrt, gq, gkv, wuqt, wuk, wuvt, cos_t, sin_t):
    B, S, D = x.shape
    tm = TQ
    assert S % tm == 0
    tok = lambda w: pl.BlockSpec((1, tm, w), lambda b, i: (b, i, 0))
    tok_t = lambda r: pl.BlockSpec((1, r, tm), lambda b, i: (b, 0, i))
    consts = (g, win, wkrt, gq, gkv, wuqt, wuk, wuvt)
    nvt = wuvt.shape[0]
    return pl.pallas_call(
        _inproj1_kernel,
        grid=(B, S // tm),
        in_specs=[tok(D)] + [_const_spec(a.shape) for a in consts] + [tok_t(cos_t.shape[1])] * 2,
        out_specs=[tok_t(wuqt.shape[0]), tok(wuk.shape[1]),
                   pl.BlockSpec((1, 1, nvt, tm), lambda b, i: (b, i, 0, 0)), tok(MEM_WIDTH)],
        out_shape=[jax.ShapeDtypeStruct((B, wuqt.shape[0], S), BF16),
                   jax.ShapeDtypeStruct((B, S, wuk.shape[1]), BF16),
                   jax.ShapeDtypeStruct((B, S // tm, nvt, tm), BF16),
                   jax.ShapeDtypeStruct((B, S, MEM_WIDTH), BF16)],
        compiler_params=_params("arbitrary", "arbitrary"),
        name="inproj1",
    )(x, *consts, cos_t, sin_t)


def _flash_kernel(qt_ref, k_ref, vt_ref, y_ref, sa_ref, sb_ref, smax_ref, acc_ref, m_ref):
    i = pl.program_id(2)
    slots = (sa_ref, sb_ref)

    def scores_t(hh, blk):
        kh = k_ref[0, pl.ds(pl.multiple_of(blk * TQ, TQ), TQ), hh * MLA_QK_PAD:(hh + 1) * MLA_QK_PAD]
        qh = qt_ref[0, hh * MLA_QK_PAD:(hh + 1) * MLA_QK_PAD, :]
        return jnp.dot(kh, qh, preferred_element_type=F32)

    def v_t(hh, blk):
        return vt_ref[0, blk, hh * MLA_VT_ROWS:(hh + 1) * MLA_VT_ROWS, :]

    def col_max(s):
        parts = [jnp.max(s[r:r + 128], axis=0, keepdims=True) for r in range(0, TQ, 128)]
        return jnp.maximum(jnp.maximum(parts[0], parts[1]), jnp.maximum(parts[2], parts[3]))

    key_i = lax.broadcasted_iota(jnp.int32, (TQ, TQ), 0)
    qry_i = lax.broadcasted_iota(jnp.int32, (TQ, TQ), 1)

    def qk_stage(slot, blk, diagonal=False):
        for hh in range(2):
            s = scores_t(hh, blk)
            if diagonal:
                s = jnp.where(key_i <= qry_i, s, -jnp.inf)
            slots[slot][hh] = s
            smax_ref[slot, hh] = col_max(s)

    def pv_stage(slot, blk):
        for hh in range(2):
            m_prev = m_ref[hh]
            m_new = jnp.maximum(m_prev, smax_ref[slot, hh])
            alpha = jnp.exp2(m_prev - m_new)
            p = jnp.exp2(slots[slot][hh] - m_new).astype(BF16)
            acc_ref[hh] = alpha * acc_ref[hh] + jnp.dot(v_t(hh, blk), p, preferred_element_type=F32)
            m_ref[hh] = m_new

    m_ref[...] = jnp.full(m_ref.shape, -jnp.inf, F32)
    acc_ref[...] = jnp.zeros_like(acc_ref)

    @pl.when(i == 0)
    def _():
        qk_stage(0, 0, diagonal=True)
        pv_stage(0, 0)

    @pl.when(i >= 1)
    def _():
        qk_stage(0, 0)

    def body(t, carry):
        qk_stage(1, 2 * t + 1)
        pv_stage(0, 2 * t)
        qk_stage(0, 2 * t + 2)
        pv_stage(1, 2 * t + 1)
        return carry

    lax.fori_loop(0, (i - 1) // 2, body, 0)

    @pl.when(i % 2 == 1)
    def _():
        qk_stage(1, i, diagonal=True)
        pv_stage(0, i - 1)
        pv_stage(1, i)

    @pl.when((i % 2 == 0) & (i >= 2))
    def _():
        qk_stage(1, i - 1)
        pv_stage(0, i - 2)
        qk_stage(0, i, diagonal=True)
        pv_stage(1, i - 1)
        pv_stage(0, i)

    outs = []
    for hh in range(2):
        acc = acc_ref[hh]
        outs.append(acc[:MLA_V] * (1.0 / acc[MLA_V:MLA_V + 1]))
    y_ref[0] = jnp.concatenate(outs, axis=0).T.astype(BF16)


def _flash(qt, k, vt):
    B, S, _ = k.shape
    assert S % TQ == 0
    pairs = MLA_HEADS // 2
    return pl.pallas_call(
        _flash_kernel,
        grid=(B, pairs, S // TQ),
        in_specs=[pl.BlockSpec((1, 2 * MLA_QK_PAD, TQ), lambda b, p, i: (b, p, i)),
                  pl.BlockSpec((1, S, 2 * MLA_QK_PAD), lambda b, p, i: (b, 0, p)),
                  pl.BlockSpec((1, S // TQ, 2 * MLA_VT_ROWS, TQ), lambda b, p, i: (b, 0, p, 0))],
        out_specs=pl.BlockSpec((1, TQ, 2 * MLA_V), lambda b, p, i: (b, i, p)),
        out_shape=jax.ShapeDtypeStruct((B, S, MLA_HEADS * MLA_V), BF16),
        scratch_shapes=[pltpu.VMEM((2, TQ, TQ), F32),
                        pltpu.VMEM((2, TQ, TQ), F32),
                        pltpu.VMEM((2, 2, 1, TQ), F32),
                        pltpu.VMEM((2, MLA_VT_ROWS, TQ), F32),
                        pltpu.VMEM((2, 1, TQ), F32)],
        compiler_params=_params("arbitrary", "arbitrary", "arbitrary"),
        name="mla_flash",
    )(qt, k, vt)


def _pad_heads(w, heads, d, dp):
    lead = w.shape[:-1]
    w = w.reshape(lead + (heads, d))
    w = jnp.pad(w, [(0, 0)] * len(lead) + [(0, 0), (0, dp - d)])
    return w.reshape(lead + (heads * dp,))


def kernel(x, mem, positions, mem_norm, w_mem_kv, norm_mix0, w_in0, b_igate0, b_fgate0, w_conv0, w_hnorm0, w_out0, norm_ffn0, w_ff1_0, w_ff2_0, norm_mix1, w_in1, w_qnorm1, w_uq1, w_kvnorm1, w_ukv1, w_out1, norm_ffn1, w_ff1_1, w_ff2_1, final_norm):
    row = lambda g: g.reshape(1, -1).astype(F32)

    kt, vm = _mem_kv(mem, row(mem_norm), w_mem_kv[:, :MEM_WIDTH].T.astype(BF16),
                     w_mem_kv[:, MEM_WIDTH:].astype(BF16))

    nq = ML_HEADS * ML_QK_DIM
    o_v, o_o, o_g = 2 * nq, 2 * nq + MIX_WIDTH, 2 * nq + 2 * MIX_WIDTH
    o_qm = o_g + 2 * ML_HEADS
    wqk = jnp.concatenate([_pad_heads(w_in0[:, :nq], ML_HEADS, ML_QK_DIM, ML_QK_PAD),
                           _pad_heads(w_in0[:, nq:o_v], ML_HEADS, ML_QK_DIM, ML_QK_PAD)], axis=1)
    wvt0 = _pad_heads(w_in0[:, o_v:o_o], ML_HEADS, ML_V_DIM, ML_VT_ROWS).T
    wot0 = w_in0[:, o_o:o_g].T
    wg0 = jnp.pad(w_in0[:, o_g:o_qm], ((0, 0), (0, LANES - 2 * ML_HEADS)))
    wqm0 = w_in0[:, o_qm:]
    wconv = jnp.concatenate([_pad_heads(w_conv0[:, :nq], ML_HEADS, ML_QK_DIM, ML_QK_PAD),
                             _pad_heads(w_conv0[:, nq:], ML_HEADS, ML_QK_DIM, ML_QK_PAD)], axis=1).astype(F32)
    gate_bias = jnp.pad(jnp.concatenate([b_igate0, b_fgate0]), (0, LANES - 2 * ML_HEADS)).reshape(1, LANES).astype(F32)
    hnorm = jnp.broadcast_to(w_hnorm0.astype(F32)[:, :, None], (ML_HEADS, ML_V_DIM, LANES))

    qk, vt0, ot0, qm0, gates = _inproj0(x, row(norm_mix0), wqk.astype(BF16), wvt0.astype(BF16),
                                        wot0.astype(BF16), wqm0.astype(BF16), wg0.astype(BF16))
    yt0 = _mlstm(qk, vt0, ot0, gates, wconv, gate_bias, hnorm)
    x = _outproj(x, yt0, qm0, kt, vm, w_out0.astype(BF16), True)
    x = _ffn(x, row(norm_ffn0), w_ff1_0.astype(BF16), w_ff2_0.astype(BF16), row(final_norm), False)

    o_kr = MLA_Q_RANK + MLA_KV_RANK
    o_qm1 = o_kr + MLA_ROPE
    win1 = jnp.concatenate([w_in1[:, :o_kr], w_in1[:, o_qm1:]], axis=1)
    wkrt = jnp.pad(w_in1[:, o_kr:o_qm1].T, ((MLA_NOPE, LANES - MLA_NOPE - MLA_ROPE), (0, 0)))
    wuqt = _pad_heads(w_uq1, MLA_HEADS, MLA_NOPE + MLA_ROPE, MLA_QK_PAD).T
    ukv = w_ukv1.reshape(MLA_KV_RANK, MLA_HEADS, MLA_NOPE + MLA_V)
    wuk = _pad_heads(ukv[:, :, :MLA_NOPE].reshape(MLA_KV_RANK, -1), MLA_HEADS, MLA_NOPE, MLA_QK_PAD)
    wuvt = _pad_heads(ukv[:, :, MLA_NOPE:].reshape(MLA_KV_RANK, -1), MLA_HEADS, MLA_V, MLA_VT_ROWS).T

    cos_t, sin_t = _rope_tables(positions)
    qt1, k1, vt1, qm1 = _inproj1(x, row(norm_mix1), win1.astype(BF16), wkrt.astype(BF16),
                                 row(w_qnorm1), row(w_kvnorm1), wuqt.astype(BF16), wuk.astype(BF16),
                                 wuvt.astype(BF16), cos_t, sin_t)
    y1 = _flash(qt1, k1, vt1)
    x = _outproj(x, y1, qm1, kt, vm, w_out1.astype(BF16), False)
    return _ffn(x, row(norm_ffn1), w_ff1_1.astype(BF16), w_ff2_1.astype(BF16), row(final_norm), True)
```

```python
import functools
import math

import jax
import jax.numpy as jnp
from jax import lax
from jax.experimental import pallas as pl
from jax.experimental.pallas import tpu as pltpu

F32 = jnp.float32
BF16 = jnp.bfloat16
EPS = 1e-6

D_MODEL = 1024
N_MEM = 256
MEM_HEADS = 4
MEM_HEAD_DIM = 64
MEM_WIDTH = MEM_HEADS * MEM_HEAD_DIM
MIX_WIDTH = D_MODEL - MEM_WIDTH

ML_HEADS = 4
ML_V_DIM = MIX_WIDTH // ML_HEADS
ML_QK_DIM = ML_V_DIM // 2
ML_CONV = 4
ML_CHUNK = 128
ML_QK_PAD = 128
ML_VT_ROWS = 256
ML_ONES_ROW = ML_V_DIM
ML_NB = 4
ML_GATE_ROWS = 16

MLA_HEADS = 12
MLA_NOPE = 64
MLA_ROPE = 32
MLA_V = 64
MLA_Q_RANK = 384
MLA_KV_RANK = 256
MLA_QK_PAD = 128
MLA_VT_ROWS = 80
ROPE_THETA = 10000.0
D_FF = 4 * D_MODEL

LANES = 128
VMEM_LIMIT = 56 * 1024 * 1024

TM_PROJ = 512
FF_CHUNK = 1024
TQ = 512

_NT = (((1,), (1,)), ((), ()))


def _params(*sem):
    return pltpu.CompilerParams(dimension_semantics=sem, vmem_limit_bytes=VMEM_LIMIT)


def _rms(x, g):
    return x * lax.rsqrt(jnp.mean(x * x, axis=-1, keepdims=True) + EPS) * g


def _const_spec(shape):
    nd = len(shape)
    return pl.BlockSpec(shape, lambda *_: (0,) * nd)


def _single_buffered_spec(shape):
    nd = len(shape)
    return pl.BlockSpec(shape, lambda *_: (0,) * nd, pipeline_mode=pl.Buffered(1))


def _mem_kv_kernel(mem_ref, g_ref, wk_ref, wvt_ref, k_ref, vt_ref):
    xn = _rms(mem_ref[0], g_ref[...]).astype(BF16)
    k = jnp.dot(xn, wk_ref[...], preferred_element_type=F32)
    vt = lax.dot_general(wvt_ref[...], xn, _NT, preferred_element_type=F32)
    k = k * (MEM_HEAD_DIM ** -0.5)
    col_head = lax.broadcasted_iota(jnp.int32, k.shape, 1) // MEM_HEAD_DIM
    row_head = lax.broadcasted_iota(jnp.int32, vt.shape, 0) // MEM_HEAD_DIM
    for h in range(MEM_HEADS):
        k_ref[0, h] = jnp.where(col_head == h, k, 0.0).astype(BF16)
        vt_ref[0, h] = jnp.where(row_head == h, vt, 0.0).astype(BF16)


def _mem_kv(mem, g, wk, wvt):
    B = mem.shape[0]
    out = jax.ShapeDtypeStruct((B, MEM_HEADS, MEM_WIDTH, N_MEM), BF16)
    return pl.pallas_call(
        _mem_kv_kernel,
        grid=(B,),
        in_specs=[pl.BlockSpec((1, N_MEM, D_MODEL), lambda b: (b, 0, 0)),
                  _const_spec(g.shape), _const_spec(wk.shape), _const_spec(wvt.shape)],
        out_specs=[pl.BlockSpec((1, MEM_HEADS, N_MEM, MEM_WIDTH), lambda b: (b, 0, 0, 0)),
                   pl.BlockSpec((1, MEM_HEADS, MEM_WIDTH, N_MEM), lambda b: (b, 0, 0, 0))],
        out_shape=[out, out],
        compiler_params=_params("arbitrary"),
        name="mem_kv",
    )(mem, g, wk, wvt)


def _inproj0_kernel(x_ref, g_ref, wqk_ref, wv_ref, wo_ref, wqm_ref,
                    qk_ref, v_ref, o_ref, qm_ref, gate_ref):
    xn = _rms(x_ref[0], g_ref[...]).astype(BF16)
    qk_ref[0] = jnp.dot(xn, wqk_ref[...], preferred_element_type=F32).astype(BF16)
    vg = lax.dot_general(wv_ref[...], xn, _NT, preferred_element_type=F32)
    nv = v_ref.shape[1]
    vt = vg[:nv]
    rowid = lax.broadcasted_iota(jnp.int32, vt.shape, 0) % ML_VT_ROWS
    v_ref[0] = jnp.where(rowid == ML_ONES_ROW, 1.0, vt).astype(BF16)
    gate_ref[0] = vg[nv:]
    o_ref[0] = lax.dot_general(wo_ref[...], xn, _NT, preferred_element_type=F32).astype(BF16)
    qm_ref[0] = lax.dot_general(wqm_ref[...], xn, _NT, preferred_element_type=F32).astype(BF16)


def _inproj0(x, g, wqk, wvgt, wot, wqmt):
    B, S, D = x.shape
    tm = min(TM_PROJ, S)
    tok = lambda w: pl.BlockSpec((1, tm, w), lambda b, i: (b, i, 0))
    tok_t = lambda r: pl.BlockSpec((1, r, tm), lambda b, i: (b, 0, i))
    consts = (g, wqk, wvgt, wot, wqmt)
    nv = wvgt.shape[0] - ML_GATE_ROWS
    return pl.pallas_call(
        _inproj0_kernel,
        grid=(B, S // tm),
        in_specs=[tok(D)] + [_const_spec(a.shape) for a in consts],
        out_specs=[tok(wqk.shape[1]), tok_t(nv), tok_t(wot.shape[0]), tok_t(wqmt.shape[0]),
                   tok_t(ML_GATE_ROWS)],
        out_shape=[jax.ShapeDtypeStruct((B, S, wqk.shape[1]), BF16),
                   jax.ShapeDtypeStruct((B, nv, S), BF16),
                   jax.ShapeDtypeStruct((B, wot.shape[0], S), BF16),
                   jax.ShapeDtypeStruct((B, wqmt.shape[0], S), BF16),
                   jax.ShapeDtypeStruct((B, ML_GATE_ROWS, S), F32)],
        compiler_params=_params("arbitrary", "arbitrary"),
        name="inproj0",
    )(x, *consts)


def _split3(x):
    hi = x.astype(BF16)
    r = x - hi.astype(F32)
    mid = r.astype(BF16)
    lo = (r - mid.astype(F32)).astype(BF16)
    return hi, mid, lo


def _mlstm_kernel(qk_ref, vt_ref, o_ref, gate_ref, wconv_ref, bias_ref, hnorm_ref,
                  y_ref, ct_ref, m_ref, tail_ref):
    L = ML_CHUNK
    HQ = ML_HEADS * ML_QK_PAD

    @pl.when(pl.program_id(1) == 0)
    def _():
        ct_ref[...] = jnp.zeros_like(ct_ref)
        m_ref[...] = jnp.zeros_like(m_ref)
        tail_ref[...] = jnp.zeros_like(tail_ref)

    key_i = lax.broadcasted_iota(jnp.int32, (L, L), 0)
    qry_i = lax.broadcasted_iota(jnp.int32, (L, L), 1)
    causal_t = key_i <= qry_i
    triu = jnp.where(causal_t, 1.0, 0.0).astype(BF16)
    wc = wconv_ref[...]

    def prepare(nb):
        x = qk_ref[nb].astype(F32)
        tail = tail_ref[nb]
        row8 = lax.broadcasted_iota(jnp.int32, tail.shape, 0)
        conv = x * wc[ML_CONV - 1:ML_CONV]
        for s in range(1, ML_CONV):
            xs = pltpu.roll(x, s, axis=0)
            head = jnp.where(row8 < s, pltpu.roll(tail, s, axis=0), xs[0:8])
            xs = jnp.concatenate([head, xs[8:]], axis=0)
            conv = conv + xs * wc[ML_CONV - 1 - s:ML_CONV - s]
        tail_ref[nb] = x[L - 8:L]
        act = conv * jax.nn.sigmoid(conv)
        q = act[:, :HQ].astype(BF16)
        k = (act[:, HQ:] * (ML_QK_DIM ** -0.5)).astype(BF16)

        g_rows = gate_ref[nb] + bias_ref[...]
        log_f = jnp.minimum(g_rows, 0.0) - jnp.log1p(jnp.exp(-jnp.abs(g_rows)))
        b_rows = sum(jnp.dot(part, triu, preferred_element_type=F32) for part in _split3(log_f))
        c_rows = g_rows[:ML_HEADS] - b_rows[ML_HEADS:2 * ML_HEADS]
        c_cols = jnp.concatenate([c_rows, jnp.zeros((L - ML_HEADS, L), F32)], axis=0).T
        return q, k, b_rows, c_rows, c_cols

    def head(nb, h, q, k, b_rows, c_rows, c_cols):
        b_row = b_rows[ML_HEADS + h:ML_HEADS + h + 1, :]
        c_row = c_rows[h:h + 1, :]
        c_col = c_cols[:, h:h + 1]
        m_prev = m_ref[nb, h][:, 0:1]
        qh = q[:, h * ML_QK_PAD:(h + 1) * ML_QK_PAD]
        kh = k[:, h * ML_QK_PAD:(h + 1) * ML_QK_PAD]
        vth = vt_ref[nb, h * ML_VT_ROWS:(h + 1) * ML_VT_ROWS, :]
        ct_prev = ct_ref[nb, h]

        cm = jnp.where(causal_t, c_col, -jnp.inf)
        m_row = jnp.maximum(m_prev, jnp.max(cm, axis=0, keepdims=True))
        w_t = jnp.exp(cm - m_row)
        a_row = jnp.exp(m_prev - m_row)
        s_t = (lax.dot_general(kh, qh, _NT, preferred_element_type=F32) * w_t).astype(BF16)
        num_t = (jnp.dot(vth, s_t, preferred_element_type=F32)
                 + a_row * lax.dot_general(ct_prev.astype(BF16), qh, _NT, preferred_element_type=F32))
        den = num_t[ML_ONES_ROW:ML_ONES_ROW + 1]
        inv = 1.0 / jnp.maximum(jnp.abs(den), jnp.exp(-(b_row + m_row)))
        hc_t = num_t[:ML_V_DIM] * inv
        ms = jnp.sum(hc_t * hc_t, axis=0, keepdims=True) * (1.0 / ML_V_DIM)
        hn_t = hc_t * lax.rsqrt(ms + EPS) * hnorm_ref[h]
        og_t = o_ref[nb, h * ML_V_DIM:(h + 1) * ML_V_DIM, :].astype(F32)
        y_ref[nb, h * ML_V_DIM:(h + 1) * ML_V_DIM, :] = (hn_t * jax.nn.sigmoid(og_t)).astype(BF16)

        m_end = m_row[:, L - 1:L]
        u_row = jnp.exp(c_row - m_end)
        decay = jnp.exp(m_prev - m_end)
        uv_t = (vth.astype(F32) * u_row).astype(BF16)
        ct_ref[nb, h] = decay * ct_prev + jnp.dot(uv_t, kh, preferred_element_type=F32)
        m_ref[nb, h] = jnp.broadcast_to(b_row[:, L - 1:L] + m_end, (1, LANES))

    prepared = [prepare(nb) for nb in range(ML_NB)]
    for h in range(ML_HEADS):
        for nb in range(ML_NB):
            head(nb, h, *prepared[nb])


def _mlstm(qk, vt, ot, gates, wconv, bias, hnorm):
    B, S, _ = qk.shape
    L = ML_CHUNK
    assert B % ML_NB == 0
    tok = lambda w: pl.BlockSpec((ML_NB, L, w), lambda b, c: (b, c, 0))
    tok_t = lambda r: pl.BlockSpec((ML_NB, r, L), lambda b, c: (b, 0, c))
    return pl.pallas_call(
        _mlstm_kernel,
        grid=(B // ML_NB, S // L),
        in_specs=[tok(qk.shape[2]), tok_t(vt.shape[1]), tok_t(ot.shape[1]), tok_t(gates.shape[1]),
                  _const_spec(wconv.shape), _const_spec(bias.shape), _const_spec(hnorm.shape)],
        out_specs=tok_t(ot.shape[1]),
        out_shape=jax.ShapeDtypeStruct(ot.shape, BF16),
        scratch_shapes=[pltpu.VMEM((ML_NB, ML_HEADS, ML_VT_ROWS, ML_QK_PAD), F32),
                        pltpu.VMEM((ML_NB, ML_HEADS, 1, LANES), F32),
                        pltpu.VMEM((ML_NB, 8, qk.shape[2]), F32)],
        compiler_params=_params("arbitrary", "arbitrary"),
        name="mlstm",
    )(qk, vt, ot, gates, wconv, bias, hnorm)


def _outproj_kernel(x_ref, y_ref, qmt_ref, km_ref, vmt_ref, w_ref, out_ref, *, y_transposed):
    tm = x_ref.shape[1]
    halves = [slice(r0, r0 + tm // 2) for r0 in range(0, tm, tm // 2)]
    scores = [[jnp.dot(km_ref[0, h], qmt_ref[0, :, cols], preferred_element_type=F32)
               for h in range(MEM_HEADS)] for cols in halves]
    for cols, s_heads in zip(halves, scores):
        ymem_t = None
        for h in range(MEM_HEADS):
            s = s_heads[h]
            e = jnp.exp(s - jnp.max(s, axis=0, keepdims=True))
            p = (e * (1.0 / jnp.sum(e, axis=0, keepdims=True))).astype(BF16)
            oh = jnp.dot(vmt_ref[0, h], p, preferred_element_type=F32)
            ymem_t = oh if ymem_t is None else ymem_t + oh
        if y_transposed:
            ycat = jnp.concatenate([y_ref[0, :, cols], ymem_t.astype(BF16)], axis=0).T
        else:
            ycat = jnp.concatenate([y_ref[0, cols, :], ymem_t.astype(BF16).T], axis=1)
        out_ref[0, cols, :] = x_ref[0, cols, :] + jnp.dot(ycat, w_ref[...], preferred_element_type=F32)


def _outproj(x, y, qmt, km, vmt, w, y_transposed):
    B, S, D = x.shape
    tm = min(TM_PROJ, S)
    tok = lambda w: pl.BlockSpec((1, tm, w), lambda b, i: (b, i, 0))
    y_spec = pl.BlockSpec((1, y.shape[1], tm), lambda b, i: (b, 0, i)) if y_transposed else tok(y.shape[2])
    per_b = pl.BlockSpec((1,) + km.shape[1:], lambda b, i: (b, 0, 0, 0))
    return pl.pallas_call(
        functools.partial(_outproj_kernel, y_transposed=y_transposed),
        grid=(B, S // tm),
        in_specs=[tok(D), y_spec, pl.BlockSpec((1, qmt.shape[1], tm), lambda b, i: (b, 0, i)), per_b, per_b,
                  _const_spec(w.shape)],
        out_specs=tok(D),
        out_shape=jax.ShapeDtypeStruct(x.shape, F32),
        compiler_params=_params("arbitrary", "arbitrary"),
        name="outproj",
    )(x, y, qmt, km, vmt, w)


def _ffn_kernel(x_ref, g_ref, w1_ref, w2_ref, gf_ref, out_ref, *, final_norm):
    x = x_ref[0]
    hn = _rms(x, g_ref[...]).astype(BF16)
    acc = x
    for c in range(D_FF // FF_CHUNK):
        w1c = w1_ref[:, c * FF_CHUNK:(c + 1) * FF_CHUNK].astype(BF16)
        a = jnp.dot(hn, w1c, preferred_element_type=F32)
        a = jnp.square(jnp.maximum(a, 0.0)).astype(BF16)
        w2c = w2_ref[c * FF_CHUNK:(c + 1) * FF_CHUNK, :].astype(BF16)
        acc = acc + jnp.dot(a, w2c, preferred_element_type=F32)
    if final_norm:
        acc = _rms(acc, gf_ref[...])
    out_ref[0] = acc


def _ffn(x, g, w1, w2, gf, final_norm):
    B, S, D = x.shape
    tm = min(TM_PROJ, S)
    tok = pl.BlockSpec((1, tm, D), lambda b, i: (b, i, 0))
    return pl.pallas_call(
        functools.partial(_ffn_kernel, final_norm=final_norm),
        grid=(B, S // tm),
        in_specs=[tok, _const_spec(g.shape), _single_buffered_spec(w1.shape), _single_buffered_spec(w2.shape),
                  _const_spec(gf.shape)],
        out_specs=tok,
        out_shape=jax.ShapeDtypeStruct(x.shape, F32),
        compiler_params=_params("arbitrary", "arbitrary"),
        name="ffn",
    )(x, g, w1, w2, gf)


def _rope_kernel(pos_ref, inv_ref, cos_ref, sin_ref):
    ang = pos_ref[0].astype(F32) * inv_ref[...]
    cos_ref[0] = jnp.cos(ang)
    sin_ref[0] = jnp.sin(ang)


def _rope_tables(positions):
    B, S = positions.shape
    nf = MLA_ROPE // 2
    inv = (ROPE_THETA ** (-jnp.arange(0, MLA_ROPE, 2, dtype=F32) / MLA_ROPE)).reshape(nf, 1)
    out = jax.ShapeDtypeStruct((B, nf, S), F32)
    spec = pl.BlockSpec((1, nf, S), lambda b: (b, 0, 0))
    return pl.pallas_call(
        _rope_kernel,
        grid=(B,),
        in_specs=[pl.BlockSpec((1, 1, S), lambda b: (b, 0, 0)), _const_spec(inv.shape)],
        out_specs=[spec, spec],
        out_shape=[out, out],
        compiler_params=_params("arbitrary"),
        name="rope_tables",
    )(positions.reshape(B, 1, S), inv)


def _inproj1_kernel(x_ref, g_ref, win_ref, wqmt_ref, wkrt_ref, gq_ref, gkv_ref, wuqt_ref, wuk_ref, wuvt_ref,
                    cos_ref, sin_ref, qt_ref, k_ref, vt_ref, qm_ref):
    half = MLA_ROPE // 2
    xn = _rms(x_ref[0], g_ref[...]).astype(BF16)
    c = jnp.dot(xn, win_ref[...], preferred_element_type=F32)
    o_kv = MLA_Q_RANK
    qm_ref[0] = lax.dot_general(wqmt_ref[...], xn, _NT, preferred_element_type=F32).astype(BF16)
    cq = _rms(c[:, :o_kv], gq_ref[...]).astype(BF16)
    ckv = _rms(c[:, o_kv:], gkv_ref[...]).astype(BF16)
    cos_t, sin_t = cos_ref[0], sin_ref[0]

    def rope_t(x1, x2):
        return x1 * cos_t - x2 * sin_t, x2 * cos_t + x1 * sin_t

    qscale = ((MLA_NOPE + MLA_ROPE) ** -0.5) * math.log2(math.e)
    qt = lax.dot_general(wuqt_ref[...], cq, _NT, preferred_element_type=F32)
    zpad = jnp.zeros((MLA_QK_PAD - MLA_NOPE - MLA_ROPE, qt.shape[1]), F32)
    for h in range(MLA_HEADS):
        b0 = h * MLA_QK_PAD
        r1, r2 = rope_t(qt[b0 + MLA_NOPE:b0 + MLA_NOPE + half],
                        qt[b0 + MLA_NOPE + half:b0 + MLA_NOPE + MLA_ROPE])
        blk = jnp.concatenate([qt[b0:b0 + MLA_NOPE], r1, r2, zpad], axis=0) * qscale
        qt_ref[0, b0:b0 + MLA_QK_PAD] = blk.astype(BF16)

    krt = lax.dot_general(wkrt_ref[...], xn, _NT, preferred_element_type=F32)
    r1, r2 = rope_t(krt[MLA_NOPE:MLA_NOPE + half], krt[MLA_NOPE + half:MLA_NOPE + MLA_ROPE])
    kr = jnp.concatenate([krt[:MLA_NOPE], r1, r2, krt[MLA_NOPE + MLA_ROPE:]], axis=0).T
    k = jnp.dot(ckv, wuk_ref[...], preferred_element_type=F32)
    for h in range(MLA_HEADS):
        sl = slice(h * MLA_QK_PAD, (h + 1) * MLA_QK_PAD)
        k_ref[0, :, sl] = (k[:, sl] + kr).astype(BF16)

    vt = lax.dot_general(wuvt_ref[...], ckv, _NT, preferred_element_type=F32)
    rowid = lax.broadcasted_iota(jnp.int32, vt.shape, 0) % MLA_VT_ROWS
    vt_ref[0, 0] = jnp.where(rowid == MLA_V, 1.0, vt).astype(BF16)


def _inproj1(x, g, win, wqmt, wkrt, gq, gkv, wuqt, wuk, wuvt, cos_t, sin_t):
    B, S, D = x.shape
    tm = TQ
    assert S % tm == 0
    tok = lambda w: pl.BlockSpec((1, tm, w), lambda b, i: (b, i, 0))
    tok_t = lambda r: pl.BlockSpec((1, r, tm), lambda b, i: (b, 0, i))
    consts = (g, win, wqmt, wkrt, gq, gkv, wuqt, wuk, wuvt)
    nvt = wuvt.shape[0]
    return pl.pallas_call(
        _inproj1_kernel,
        grid=(B, S // tm),
        in_specs=[tok(D)] + [_const_spec(a.shape) for a in consts] + [tok_t(cos_t.shape[1])] * 2,
        out_specs=[tok_t(wuqt.shape[0]), tok(wuk.shape[1]),
                   pl.BlockSpec((1, 1, nvt, tm), lambda b, i: (b, i, 0, 0)), tok_t(MEM_WIDTH)],
        out_shape=[jax.ShapeDtypeStruct((B, wuqt.shape[0], S), BF16),
                   jax.ShapeDtypeStruct((B, S, wuk.shape[1]), BF16),
                   jax.ShapeDtypeStruct((B, S // tm, nvt, tm), BF16),
                   jax.ShapeDtypeStruct((B, MEM_WIDTH, S), BF16)],
        compiler_params=_params("arbitrary", "arbitrary"),
        name="inproj1",
    )(x, *consts, cos_t, sin_t)


def _flash_kernel(qt_ref, k_ref, vt_ref, y_ref, sa_ref, sb_ref, smax_ref, acc_ref, m_ref):
    i = pl.program_id(2)
    slots = (sa_ref, sb_ref)

    def scores_t(hh, blk):
        kh = k_ref[0, pl.ds(pl.multiple_of(blk * TQ, TQ), TQ), hh * MLA_QK_PAD:(hh + 1) * MLA_QK_PAD]
        qh = qt_ref[0, hh * MLA_QK_PAD:(hh + 1) * MLA_QK_PAD, :]
        return jnp.dot(kh, qh, preferred_element_type=F32)

    def v_t(hh, blk):
        return vt_ref[0, blk, hh * MLA_VT_ROWS:(hh + 1) * MLA_VT_ROWS, :]

    def col_max(s):
        parts = [jnp.max(s[r:r + 128], axis=0, keepdims=True) for r in range(0, TQ, 128)]
        return jnp.maximum(jnp.maximum(parts[0], parts[1]), jnp.maximum(parts[2], parts[3]))

    key_i = lax.broadcasted_iota(jnp.int32, (TQ, TQ), 0)
    qry_i = lax.broadcasted_iota(jnp.int32, (TQ, TQ), 1)

    def qk_stage(slot, blk, diagonal=False):
        for hh in range(2):
            s = scores_t(hh, blk)
            if diagonal:
                s = jnp.where(key_i <= qry_i, s, -jnp.inf)
            slots[slot][hh] = s
            smax_ref[slot, hh] = col_max(s)

    def pv_stage(slot, blk):
        for hh in range(2):
            m_prev = m_ref[hh]
            m_new = jnp.maximum(m_prev, smax_ref[slot, hh])
            alpha = jnp.exp2(m_prev - m_new)
            p = jnp.exp2(slots[slot][hh] - m_new).astype(BF16)
            acc_ref[hh] = alpha * acc_ref[hh] + jnp.dot(v_t(hh, blk), p, preferred_element_type=F32)
            m_ref[hh] = m_new

    m_ref[...] = jnp.full(m_ref.shape, -jnp.inf, F32)
    acc_ref[...] = jnp.zeros_like(acc_ref)

    @pl.when(i == 0)
    def _():
        qk_stage(0, 0, diagonal=True)
        pv_stage(0, 0)

    @pl.when(i >= 1)
    def _():
        qk_stage(0, 0)

    def body(t, carry):
        qk_stage(1, 2 * t + 1)
        pv_stage(0, 2 * t)
        qk_stage(0, 2 * t + 2)
        pv_stage(1, 2 * t + 1)
        return carry

    lax.fori_loop(0, (i - 1) // 2, body, 0)

    @pl.when(i % 2 == 1)
    def _():
        qk_stage(1, i, diagonal=True)
        pv_stage(0, i - 1)
        pv_stage(1, i)

    @pl.when((i % 2 == 0) & (i >= 2))
    def _():
        qk_stage(1, i - 1)
        pv_stage(0, i - 2)
        qk_stage(0, i, diagonal=True)
        pv_stage(1, i - 1)
        pv_stage(0, i)

    outs = []
    for hh in range(2):
        acc = acc_ref[hh]
        outs.append(acc[:MLA_V] * (1.0 / acc[MLA_V:MLA_V + 1]))
    y_ref[0] = jnp.concatenate(outs, axis=0).T.astype(BF16)


def _flash(qt, k, vt):
    B, S, _ = k.shape
    assert S % TQ == 0
    pairs = MLA_HEADS // 2
    return pl.pallas_call(
        _flash_kernel,
        grid=(B, pairs, S // TQ),
        in_specs=[pl.BlockSpec((1, 2 * MLA_QK_PAD, TQ), lambda b, p, i: (b, p, i)),
                  pl.BlockSpec((1, S, 2 * MLA_QK_PAD), lambda b, p, i: (b, 0, p)),
                  pl.BlockSpec((1, S // TQ, 2 * MLA_VT_ROWS, TQ), lambda b, p, i: (b, 0, p, 0))],
        out_specs=pl.BlockSpec((1, TQ, 2 * MLA_V), lambda b, p, i: (b, i, p)),
        out_shape=jax.ShapeDtypeStruct((B, S, MLA_HEADS * MLA_V), BF16),
        scratch_shapes=[pltpu.VMEM((2, TQ, TQ), F32),
                        pltpu.VMEM((2, TQ, TQ), F32),
                        pltpu.VMEM((2, 2, 1, TQ), F32),
                        pltpu.VMEM((2, MLA_VT_ROWS, TQ), F32),
                        pltpu.VMEM((2, 1, TQ), F32)],
        compiler_params=_params("arbitrary", "arbitrary", "arbitrary"),
        name="mla_flash",
    )(qt, k, vt)


def _pad_heads(w, heads, d, dp):
    lead = w.shape[:-1]
    w = w.reshape(lead + (heads, d))
    w = jnp.pad(w, [(0, 0)] * len(lead) + [(0, 0), (0, dp - d)])
    return w.reshape(lead + (heads * dp,))


def kernel(x, mem, positions, mem_norm, w_mem_kv, norm_mix0, w_in0, b_igate0, b_fgate0, w_conv0, w_hnorm0, w_out0, norm_ffn0, w_ff1_0, w_ff2_0, norm_mix1, w_in1, w_qnorm1, w_uq1, w_kvnorm1, w_ukv1, w_out1, norm_ffn1, w_ff1_1, w_ff2_1, final_norm):
    row = lambda g: g.reshape(1, -1).astype(F32)

    km, vmt = _mem_kv(mem, row(mem_norm), w_mem_kv[:, :MEM_WIDTH].astype(BF16),
                      w_mem_kv[:, MEM_WIDTH:].T.astype(BF16))

    nq = ML_HEADS * ML_QK_DIM
    o_v, o_o, o_g = 2 * nq, 2 * nq + MIX_WIDTH, 2 * nq + 2 * MIX_WIDTH
    o_qm = o_g + 2 * ML_HEADS
    wqk = jnp.concatenate([_pad_heads(w_in0[:, :nq], ML_HEADS, ML_QK_DIM, ML_QK_PAD),
                           _pad_heads(w_in0[:, nq:o_v], ML_HEADS, ML_QK_DIM, ML_QK_PAD)], axis=1)
    wvgt0 = jnp.concatenate([_pad_heads(w_in0[:, o_v:o_o], ML_HEADS, ML_V_DIM, ML_VT_ROWS),
                             jnp.pad(w_in0[:, o_g:o_qm], ((0, 0), (0, ML_GATE_ROWS - 2 * ML_HEADS)))], axis=1).T
    wot0 = w_in0[:, o_o:o_g].T
    wqmt0 = w_in0[:, o_qm:].T
    wconv = jnp.concatenate([_pad_heads(w_conv0[:, :nq], ML_HEADS, ML_QK_DIM, ML_QK_PAD),
                             _pad_heads(w_conv0[:, nq:], ML_HEADS, ML_QK_DIM, ML_QK_PAD)], axis=1).astype(F32)
    gate_bias = jnp.pad(jnp.concatenate([b_igate0, b_fgate0]), (0, ML_GATE_ROWS - 2 * ML_HEADS)).astype(F32)
    gate_bias = jnp.broadcast_to(gate_bias[:, None], (ML_GATE_ROWS, LANES))
    hnorm = jnp.broadcast_to(w_hnorm0.astype(F32)[:, :, None], (ML_HEADS, ML_V_DIM, LANES))

    qk, vt0, ot0, qm0, gates = _inproj0(x, row(norm_mix0), wqk.astype(BF16), wvgt0.astype(BF16),
                                        wot0.astype(BF16), wqmt0.astype(BF16))
    yt0 = _mlstm(qk, vt0, ot0, gates, wconv, gate_bias, hnorm)
    x = _outproj(x, yt0, qm0, km, vmt, w_out0.astype(BF16), True)
    x = _ffn(x, row(norm_ffn0), w_ff1_0, w_ff2_0, row(final_norm), False)

    o_kr = MLA_Q_RANK + MLA_KV_RANK
    o_qm1 = o_kr + MLA_ROPE
    win1 = w_in1[:, :o_kr]
    wqmt1 = w_in1[:, o_qm1:].T
    wkrt = jnp.pad(w_in1[:, o_kr:o_qm1].T, ((MLA_NOPE, LANES - MLA_NOPE - MLA_ROPE), (0, 0)))
    wuqt = _pad_heads(w_uq1, MLA_HEADS, MLA_NOPE + MLA_ROPE, MLA_QK_PAD).T
    ukv = w_ukv1.reshape(MLA_KV_RANK, MLA_HEADS, MLA_NOPE + MLA_V)
    wuk = _pad_heads(ukv[:, :, :MLA_NOPE].reshape(MLA_KV_RANK, -1), MLA_HEADS, MLA_NOPE, MLA_QK_PAD)
    wuvt = _pad_heads(ukv[:, :, MLA_NOPE:].reshape(MLA_KV_RANK, -1), MLA_HEADS, MLA_V, MLA_VT_ROWS).T

    cos_t, sin_t = _rope_tables(positions)
    qt1, k1, vt1, qm1 = _inproj1(x, row(norm_mix1), win1.astype(BF16), wqmt1.astype(BF16), wkrt.astype(BF16),
                                 row(w_qnorm1), row(w_kvnorm1), wuqt.astype(BF16), wuk.astype(BF16),
                                 wuvt.astype(BF16), cos_t, sin_t)
    y1 = _flash(qt1, k1, vt1)
    x = _outproj(x, y1, qm1, km, vmt, w_out1.astype(BF16), False)
    return _ffn(x, row(norm_ffn1), w_ff1_1, w_ff2_1, row(final_norm), True)
```

```python
import functools
import math

import jax
import jax.numpy as jnp
from jax import lax
from jax.experimental import pallas as pl
from jax.experimental.pallas import tpu as pltpu

F32 = jnp.float32
BF16 = jnp.bfloat16
EPS = 1e-6

D_MODEL = 1024
N_MEM = 256
MEM_HEADS = 4
MEM_HEAD_DIM = 64
MEM_WIDTH = MEM_HEADS * MEM_HEAD_DIM
MIX_WIDTH = D_MODEL - MEM_WIDTH

ML_HEADS = 4
ML_V_DIM = MIX_WIDTH // ML_HEADS
ML_QK_DIM = ML_V_DIM // 2
ML_CONV = 4
ML_CHUNK = 128
ML_QK_PAD = 128
ML_VT_ROWS = 256
ML_ONES_ROW = ML_V_DIM
ML_NB = 4
ML_GATE_ROWS = 16

MLA_HEADS = 12
MLA_NOPE = 64
MLA_ROPE = 32
MLA_V = 64
MLA_Q_RANK = 384
MLA_KV_RANK = 256
MLA_QK_PAD = 128
MLA_VT_ROWS = 80
ROPE_THETA = 10000.0
D_FF = 4 * D_MODEL

LANES = 128
VMEM_LIMIT = 56 * 1024 * 1024

TM_PROJ = 512
FF_CHUNK = 1024
TQ = 512

_NT = (((1,), (1,)), ((), ()))


def _params(*sem):
    return pltpu.CompilerParams(dimension_semantics=sem, vmem_limit_bytes=VMEM_LIMIT)


def _rms(x, g):
    return x * lax.rsqrt(jnp.mean(x * x, axis=-1, keepdims=True) + EPS) * g


def _const_spec(shape):
    nd = len(shape)
    return pl.BlockSpec(shape, lambda *_: (0,) * nd)


def _single_buffered_spec(shape):
    nd = len(shape)
    return pl.BlockSpec(shape, lambda *_: (0,) * nd, pipeline_mode=pl.Buffered(1))


def _mem_kv_kernel(mem_ref, g_ref, wk_ref, wvt_ref, k_ref, vt_ref):
    xn = _rms(mem_ref[0], g_ref[...]).astype(BF16)
    k = jnp.dot(xn, wk_ref[...], preferred_element_type=F32)
    vt = lax.dot_general(wvt_ref[...], xn, _NT, preferred_element_type=F32)
    k = k * (MEM_HEAD_DIM ** -0.5)
    col_head = lax.broadcasted_iota(jnp.int32, k.shape, 1) // MEM_HEAD_DIM
    row_head = lax.broadcasted_iota(jnp.int32, vt.shape, 0) // MEM_HEAD_DIM
    for h in range(MEM_HEADS):
        k_ref[0, h] = jnp.where(col_head == h, k, 0.0).astype(BF16)
        vt_ref[0, h] = jnp.where(row_head == h, vt, 0.0).astype(BF16)


def _mem_kv(mem, g, wk, wvt):
    B = mem.shape[0]
    out = jax.ShapeDtypeStruct((B, MEM_HEADS, MEM_WIDTH, N_MEM), BF16)
    return pl.pallas_call(
        _mem_kv_kernel,
        grid=(B,),
        in_specs=[pl.BlockSpec((1, N_MEM, D_MODEL), lambda b: (b, 0, 0)),
                  _const_spec(g.shape), _const_spec(wk.shape), _const_spec(wvt.shape)],
        out_specs=[pl.BlockSpec((1, MEM_HEADS, N_MEM, MEM_WIDTH), lambda b: (b, 0, 0, 0)),
                   pl.BlockSpec((1, MEM_HEADS, MEM_WIDTH, N_MEM), lambda b: (b, 0, 0, 0))],
        out_shape=[out, out],
        compiler_params=_params("arbitrary"),
        name="mem_kv",
    )(mem, g, wk, wvt)


def _inproj0_kernel(x_ref, g_ref, wqk_ref, wv_ref, wo_ref, wqm_ref,
                    qk_ref, v_ref, o_ref, qm_ref, gate_ref):
    xn = _rms(x_ref[0], g_ref[...]).astype(BF16)
    qk_ref[0] = jnp.dot(xn, wqk_ref[...], preferred_element_type=F32).astype(BF16)
    vg = lax.dot_general(wv_ref[...], xn, _NT, preferred_element_type=F32)
    nv = v_ref.shape[1]
    vt = vg[:nv]
    rowid = lax.broadcasted_iota(jnp.int32, vt.shape, 0) % ML_VT_ROWS
    v_ref[0] = jnp.where(rowid == ML_ONES_ROW, 1.0, vt).astype(BF16)
    gate_ref[0] = vg[nv:]
    o_ref[0] = lax.dot_general(wo_ref[...], xn, _NT, preferred_element_type=F32).astype(BF16)
    qm_ref[0] = lax.dot_general(wqm_ref[...], xn, _NT, preferred_element_type=F32).astype(BF16)


def _inproj0(x, g, wqk, wvgt, wot, wqmt):
    B, S, D = x.shape
    tm = min(TM_PROJ, S)
    tok = lambda w: pl.BlockSpec((1, tm, w), lambda b, i: (b, i, 0))
    tok_t = lambda r: pl.BlockSpec((1, r, tm), lambda b, i: (b, 0, i))
    consts = (g, wqk, wvgt, wot, wqmt)
    nv = wvgt.shape[0] - ML_GATE_ROWS
    return pl.pallas_call(
        _inproj0_kernel,
        grid=(B, S // tm),
        in_specs=[tok(D)] + [_const_spec(a.shape) for a in consts],
        out_specs=[tok(wqk.shape[1]), tok_t(nv), tok_t(wot.shape[0]), tok_t(wqmt.shape[0]),
                   tok_t(ML_GATE_ROWS)],
        out_shape=[jax.ShapeDtypeStruct((B, S, wqk.shape[1]), BF16),
                   jax.ShapeDtypeStruct((B, nv, S), BF16),
                   jax.ShapeDtypeStruct((B, wot.shape[0], S), BF16),
                   jax.ShapeDtypeStruct((B, wqmt.shape[0], S), BF16),
                   jax.ShapeDtypeStruct((B, ML_GATE_ROWS, S), F32)],
        compiler_params=_params("arbitrary", "arbitrary"),
        name="inproj0",
    )(x, *consts)


def _split3(x):
    hi = x.astype(BF16)
    r = x - hi.astype(F32)
    mid = r.astype(BF16)
    lo = (r - mid.astype(F32)).astype(BF16)
    return hi, mid, lo


def _mlstm_kernel(qk_ref, vt_ref, o_ref, gate_ref, wconv_ref, bias_ref, hnorm_ref,
                  y_ref, ct_ref, m_ref, tail_ref):
    L = ML_CHUNK
    HQ = ML_HEADS * ML_QK_PAD

    @pl.when(pl.program_id(1) == 0)
    def _():
        ct_ref[...] = jnp.zeros_like(ct_ref)
        m_ref[...] = jnp.zeros_like(m_ref)
        tail_ref[...] = jnp.zeros_like(tail_ref)

    key_i = lax.broadcasted_iota(jnp.int32, (L, L), 0)
    qry_i = lax.broadcasted_iota(jnp.int32, (L, L), 1)
    causal_t = key_i <= qry_i
    triu = jnp.where(causal_t, 1.0, 0.0).astype(BF16)
    wc = wconv_ref[...]

    def prepare(nb):
        x = qk_ref[nb].astype(F32)
        tail = tail_ref[nb]
        row8 = lax.broadcasted_iota(jnp.int32, tail.shape, 0)
        conv = x * wc[ML_CONV - 1:ML_CONV]
        for s in range(1, ML_CONV):
            xs = pltpu.roll(x, s, axis=0)
            head = jnp.where(row8 < s, pltpu.roll(tail, s, axis=0), xs[0:8])
            xs = jnp.concatenate([head, xs[8:]], axis=0)
            conv = conv + xs * wc[ML_CONV - 1 - s:ML_CONV - s]
        tail_ref[nb] = x[L - 8:L]
        act = conv * jax.nn.sigmoid(conv)
        q = act[:, :HQ].astype(BF16)
        k = (act[:, HQ:] * (ML_QK_DIM ** -0.5)).astype(BF16)

        g_rows = gate_ref[nb] + bias_ref[...]
        log_f = jnp.minimum(g_rows, 0.0) - jnp.log1p(jnp.exp(-jnp.abs(g_rows)))
        b_rows = sum(jnp.dot(part, triu, preferred_element_type=F32) for part in _split3(log_f))
        c_rows = g_rows[:ML_HEADS] - b_rows[ML_HEADS:2 * ML_HEADS]
        c_cols = jnp.concatenate([c_rows, jnp.zeros((L - ML_HEADS, L), F32)], axis=0).T
        return q, k, b_rows, c_rows, c_cols

    def head(nb, h, q, k, b_rows, c_rows, c_cols):
        b_row = b_rows[ML_HEADS + h:ML_HEADS + h + 1, :]
        c_row = c_rows[h:h + 1, :]
        c_col = c_cols[:, h:h + 1]
        m_prev = m_ref[nb, h][:, 0:1]
        qh = q[:, h * ML_QK_PAD:(h + 1) * ML_QK_PAD]
        kh = k[:, h * ML_QK_PAD:(h + 1) * ML_QK_PAD]
        vth = vt_ref[nb, h * ML_VT_ROWS:(h + 1) * ML_VT_ROWS, :]
        ct_prev = ct_ref[nb, h]

        cm = jnp.where(causal_t, c_col, -jnp.inf)
        m_row = jnp.maximum(m_prev, jnp.max(cm, axis=0, keepdims=True))
        w_t = jnp.exp(cm - m_row)
        a_row = jnp.exp(m_prev - m_row)
        s_t = (lax.dot_general(kh, qh, _NT, preferred_element_type=F32) * w_t).astype(BF16)
        num_t = (jnp.dot(vth, s_t, preferred_element_type=F32)
                 + a_row * lax.dot_general(ct_prev.astype(BF16), qh, _NT, preferred_element_type=F32))
        den = num_t[ML_ONES_ROW:ML_ONES_ROW + 1]
        inv = 1.0 / jnp.maximum(jnp.abs(den), jnp.exp(-(b_row + m_row)))
        hc_t = num_t[:ML_V_DIM] * inv
        ms = jnp.sum(hc_t * hc_t, axis=0, keepdims=True) * (1.0 / ML_V_DIM)
        hn_t = hc_t * lax.rsqrt(ms + EPS) * hnorm_ref[h]
        og_t = o_ref[nb, h * ML_V_DIM:(h + 1) * ML_V_DIM, :].astype(F32)
        y_ref[nb, h * ML_V_DIM:(h + 1) * ML_V_DIM, :] = (hn_t * jax.nn.sigmoid(og_t)).astype(BF16)

        m_end = m_row[:, L - 1:L]
        u_row = jnp.exp(c_row - m_end)
        decay = jnp.exp(m_prev - m_end)
        uv_t = (vth.astype(F32) * u_row).astype(BF16)
        ct_ref[nb, h] = decay * ct_prev + jnp.dot(uv_t, kh, preferred_element_type=F32)
        m_ref[nb, h] = jnp.broadcast_to(b_row[:, L - 1:L] + m_end, (1, LANES))

    prepared = [prepare(nb) for nb in range(ML_NB)]
    for h in range(ML_HEADS):
        for nb in range(ML_NB):
            head(nb, h, *prepared[nb])


def _mlstm(qk, vt, ot, gates, wconv, bias, hnorm):
    B, S, _ = qk.shape
    L = ML_CHUNK
    assert B % ML_NB == 0
    tok = lambda w: pl.BlockSpec((ML_NB, L, w), lambda b, c: (b, c, 0))
    tok_t = lambda r: pl.BlockSpec((ML_NB, r, L), lambda b, c: (b, 0, c))
    return pl.pallas_call(
        _mlstm_kernel,
        grid=(B // ML_NB, S // L),
        in_specs=[tok(qk.shape[2]), tok_t(vt.shape[1]), tok_t(ot.shape[1]), tok_t(gates.shape[1]),
                  _const_spec(wconv.shape), _const_spec(bias.shape), _const_spec(hnorm.shape)],
        out_specs=tok_t(ot.shape[1]),
        out_shape=jax.ShapeDtypeStruct(ot.shape, BF16),
        scratch_shapes=[pltpu.VMEM((ML_NB, ML_HEADS, ML_VT_ROWS, ML_QK_PAD), F32),
                        pltpu.VMEM((ML_NB, ML_HEADS, 1, LANES), F32),
                        pltpu.VMEM((ML_NB, 8, qk.shape[2]), F32)],
        compiler_params=_params("arbitrary", "arbitrary"),
        name="mlstm",
    )(qk, vt, ot, gates, wconv, bias, hnorm)


def _outproj_kernel(x_ref, y_ref, qmt_ref, km_ref, vmt_ref, w_ref, out_ref, *, y_transposed):
    tm = x_ref.shape[1]
    halves = [slice(r0, r0 + tm // 2) for r0 in range(0, tm, tm // 2)]
    scores = [[jnp.dot(km_ref[0, h], qmt_ref[0, :, cols], preferred_element_type=F32)
               for h in range(MEM_HEADS)] for cols in halves]
    for cols, s_heads in zip(halves, scores):
        ymem_t = None
        for h in range(MEM_HEADS):
            s = s_heads[h]
            e = jnp.exp(s - jnp.max(s, axis=0, keepdims=True))
            p = (e * (1.0 / jnp.sum(e, axis=0, keepdims=True))).astype(BF16)
            oh = jnp.dot(vmt_ref[0, h], p, preferred_element_type=F32)
            ymem_t = oh if ymem_t is None else ymem_t + oh
        if y_transposed:
            ycat = jnp.concatenate([y_ref[0, :, cols], ymem_t.astype(BF16)], axis=0).T
        else:
            ycat = jnp.concatenate([y_ref[0, cols, :], ymem_t.astype(BF16).T], axis=1)
        out_ref[0, cols, :] = x_ref[0, cols, :] + jnp.dot(ycat, w_ref[...], preferred_element_type=F32)


def _outproj(x, y, qmt, km, vmt, w, y_transposed):
    B, S, D = x.shape
    tm = min(TM_PROJ, S)
    tok = lambda w: pl.BlockSpec((1, tm, w), lambda b, i: (b, i, 0))
    y_spec = pl.BlockSpec((1, y.shape[1], tm), lambda b, i: (b, 0, i)) if y_transposed else tok(y.shape[2])
    per_b = pl.BlockSpec((1,) + km.shape[1:], lambda b, i: (b, 0, 0, 0))
    return pl.pallas_call(
        functools.partial(_outproj_kernel, y_transposed=y_transposed),
        grid=(B, S // tm),
        in_specs=[tok(D), y_spec, pl.BlockSpec((1, qmt.shape[1], tm), lambda b, i: (b, 0, i)), per_b, per_b,
                  _const_spec(w.shape)],
        out_specs=tok(D),
        out_shape=jax.ShapeDtypeStruct(x.shape, F32),
        compiler_params=_params("arbitrary", "arbitrary"),
        name="outproj",
    )(x, y, qmt, km, vmt, w)


def _ffn_kernel(x_ref, g_ref, w1_ref, w2_ref, gf_ref, out_ref, *, final_norm):
    x = x_ref[0]
    hn = _rms(x, g_ref[...]).astype(BF16)
    acc = x
    for c in range(D_FF // FF_CHUNK):
        w1c = w1_ref[:, c * FF_CHUNK:(c + 1) * FF_CHUNK].astype(BF16)
        a = jnp.dot(hn, w1c, preferred_element_type=F32)
        a = jnp.square(jnp.maximum(a, 0.0)).astype(BF16)
        w2c = w2_ref[c * FF_CHUNK:(c + 1) * FF_CHUNK, :].astype(BF16)
        acc = acc + jnp.dot(a, w2c, preferred_element_type=F32)
    if final_norm:
        acc = _rms(acc, gf_ref[...])
    out_ref[0] = acc


def _ffn(x, g, w1, w2, gf, final_norm):
    B, S, D = x.shape
    tm = min(TM_PROJ, S)
    tok = pl.BlockSpec((1, tm, D), lambda b, i: (b, i, 0))
    return pl.pallas_call(
        functools.partial(_ffn_kernel, final_norm=final_norm),
        grid=(B, S // tm),
        in_specs=[tok, _const_spec(g.shape), _single_buffered_spec(w1.shape), _single_buffered_spec(w2.shape),
                  _const_spec(gf.shape)],
        out_specs=tok,
        out_shape=jax.ShapeDtypeStruct(x.shape, F32),
        compiler_params=_params("arbitrary", "arbitrary"),
        name="ffn",
    )(x, g, w1, w2, gf)


def _rope_kernel(pos_ref, inv_ref, cos_ref, sin_ref):
    ang = pos_ref[0].astype(F32) * inv_ref[...]
    cos_ref[0] = jnp.cos(ang)
    sin_ref[0] = jnp.sin(ang)


def _rope_tables(positions):
    B, S = positions.shape
    nf = MLA_ROPE // 2
    inv = (ROPE_THETA ** (-jnp.arange(0, MLA_ROPE, 2, dtype=F32) / MLA_ROPE)).reshape(nf, 1)
    out = jax.ShapeDtypeStruct((B, nf, S), F32)
    spec = pl.BlockSpec((1, nf, S), lambda b: (b, 0, 0))
    return pl.pallas_call(
        _rope_kernel,
        grid=(B,),
        in_specs=[pl.BlockSpec((1, 1, S), lambda b: (b, 0, 0)), _const_spec(inv.shape)],
        out_specs=[spec, spec],
        out_shape=[out, out],
        compiler_params=_params("arbitrary"),
        name="rope_tables",
    )(positions.reshape(B, 1, S), inv)


def _inproj1_kernel(x_ref, g_ref, win_ref, wqmt_ref, wkrt_ref, gq_ref, gkv_ref, wuqt_ref, wuk_ref, wuvt_ref,
                    cos_ref, sin_ref, qt_ref, k_ref, vt_ref, qm_ref):
    half = MLA_ROPE // 2
    xn = _rms(x_ref[0], g_ref[...]).astype(BF16)
    c = jnp.dot(xn, win_ref[...], preferred_element_type=F32)
    o_kv = MLA_Q_RANK
    qm_ref[0] = lax.dot_general(wqmt_ref[...], xn, _NT, preferred_element_type=F32).astype(BF16)
    cq = _rms(c[:, :o_kv], gq_ref[...]).astype(BF16)
    ckv = _rms(c[:, o_kv:], gkv_ref[...]).astype(BF16)
    cos_t, sin_t = cos_ref[0], sin_ref[0]

    def rope_t(x1, x2):
        return x1 * cos_t - x2 * sin_t, x2 * cos_t + x1 * sin_t

    qscale = ((MLA_NOPE + MLA_ROPE) ** -0.5) * math.log2(math.e)
    qt = lax.dot_general(wuqt_ref[...], cq, _NT, preferred_element_type=F32)
    zpad = jnp.zeros((MLA_QK_PAD - MLA_NOPE - MLA_ROPE, qt.shape[1]), F32)
    for h in range(MLA_HEADS):
        b0 = h * MLA_QK_PAD
        r1, r2 = rope_t(qt[b0 + MLA_NOPE:b0 + MLA_NOPE + half],
                        qt[b0 + MLA_NOPE + half:b0 + MLA_NOPE + MLA_ROPE])
        blk = jnp.concatenate([qt[b0:b0 + MLA_NOPE], r1, r2, zpad], axis=0) * qscale
        qt_ref[0, b0:b0 + MLA_QK_PAD] = blk.astype(BF16)

    krt = lax.dot_general(wkrt_ref[...], xn, _NT, preferred_element_type=F32)
    r1, r2 = rope_t(krt[MLA_NOPE:MLA_NOPE + half], krt[MLA_NOPE + half:MLA_NOPE + MLA_ROPE])
    kr = jnp.concatenate([krt[:MLA_NOPE], r1, r2, krt[MLA_NOPE + MLA_ROPE:]], axis=0).T
    k = jnp.dot(ckv, wuk_ref[...], preferred_element_type=F32)
    for h in range(MLA_HEADS):
        sl = slice(h * MLA_QK_PAD, (h + 1) * MLA_QK_PAD)
        k_ref[0, :, sl] = (k[:, sl] + kr).astype(BF16)

    vt = lax.dot_general(wuvt_ref[...], ckv, _NT, preferred_element_type=F32)
    rowid = lax.broadcasted_iota(jnp.int32, vt.shape, 0) % MLA_VT_ROWS
    vt_ref[0, 0] = jnp.where(rowid == MLA_V, 1.0, vt).astype(BF16)


def _inproj1(x, g, win, wqmt, wkrt, gq, gkv, wuqt, wuk, wuvt, cos_t, sin_t):
    B, S, D = x.shape
    tm = TQ
    assert S % tm == 0
    tok = lambda w: pl.BlockSpec((1, tm, w), lambda b, i: (b, i, 0))
    tok_t = lambda r: pl.BlockSpec((1, r, tm), lambda b, i: (b, 0, i))
    consts = (g, win, wqmt, wkrt, gq, gkv, wuqt, wuk, wuvt)
    nvt = wuvt.shape[0]
    return pl.pallas_call(
        _inproj1_kernel,
        grid=(B, S // tm),
        in_specs=[tok(D)] + [_const_spec(a.shape) for a in consts] + [tok_t(cos_t.shape[1])] * 2,
        out_specs=[tok_t(wuqt.shape[0]), tok(wuk.shape[1]),
                   pl.BlockSpec((1, 1, nvt, tm), lambda b, i: (b, i, 0, 0)), tok_t(MEM_WIDTH)],
        out_shape=[jax.ShapeDtypeStruct((B, wuqt.shape[0], S), BF16),
                   jax.ShapeDtypeStruct((B, S, wuk.shape[1]), BF16),
                   jax.ShapeDtypeStruct((B, S // tm, nvt, tm), BF16),
                   jax.ShapeDtypeStruct((B, MEM_WIDTH, S), BF16)],
        compiler_params=_params("arbitrary", "arbitrary"),
        name="inproj1",
    )(x, *consts, cos_t, sin_t)


def _flash_kernel(qt_ref, k_ref, vt_ref, y_ref, sa_ref, sb_ref, smax_ref, acc_ref, m_ref):
    j = pl.program_id(2)
    ia = 2 * j
    ib = 2 * j + 1
    slots = (sa_ref, sb_ref)

    def scores_t(qt, hh, blk):
        kh = k_ref[0, pl.ds(pl.multiple_of(blk * TQ, TQ), TQ), hh * MLA_QK_PAD:(hh + 1) * MLA_QK_PAD]
        qh = qt_ref[0, hh * MLA_QK_PAD:(hh + 1) * MLA_QK_PAD, qt * TQ:(qt + 1) * TQ]
        return jnp.dot(kh, qh, preferred_element_type=F32)

    def v_t(hh, blk):
        return vt_ref[0, blk, hh * MLA_VT_ROWS:(hh + 1) * MLA_VT_ROWS, :]

    def col_max(s):
        parts = [jnp.max(s[r:r + 128], axis=0, keepdims=True) for r in range(0, TQ, 128)]
        return jnp.maximum(jnp.maximum(parts[0], parts[1]), jnp.maximum(parts[2], parts[3]))

    key_i = lax.broadcasted_iota(jnp.int32, (TQ, TQ), 0)
    qry_i = lax.broadcasted_iota(jnp.int32, (TQ, TQ), 1)

    def qk_stage(qt, slot, blk, diagonal=False):
        for hh in range(2):
            s = scores_t(qt, hh, blk)
            if diagonal:
                s = jnp.where(key_i <= qry_i, s, -jnp.inf)
            slots[slot][hh] = s
            smax_ref[slot, hh] = col_max(s)

    def pv_stage(qt, slot, blk):
        for hh in range(2):
            m_prev = m_ref[qt, hh]
            m_new = jnp.maximum(m_prev, smax_ref[slot, hh])
            alpha = jnp.exp2(m_prev - m_new)
            p = jnp.exp2(slots[slot][hh] - m_new).astype(BF16)
            acc_ref[qt, hh] = alpha * acc_ref[qt, hh] + jnp.dot(v_t(hh, blk), p, preferred_element_type=F32)
            m_ref[qt, hh] = m_new

    def finalize(qt):
        outs = []
        for hh in range(2):
            acc = acc_ref[qt, hh]
            outs.append(acc[:MLA_V] * (1.0 / acc[MLA_V:MLA_V + 1]))
        y_ref[0, qt * TQ:(qt + 1) * TQ, :] = jnp.concatenate(outs, axis=0).T.astype(BF16)

    m_ref[...] = jnp.full(m_ref.shape, -jnp.inf, F32)
    acc_ref[...] = jnp.zeros_like(acc_ref)

    @pl.when(j == 0)
    def _():
        qk_stage(0, 0, 0, diagonal=True)

    @pl.when(j >= 1)
    def _():
        qk_stage(0, 0, 0)

    def body_a(t, carry):
        qk_stage(0, 1, 2 * t + 1)
        pv_stage(0, 0, 2 * t)
        qk_stage(0, 0, 2 * t + 2)
        pv_stage(0, 1, 2 * t + 1)
        return carry

    lax.fori_loop(0, j - 1, body_a, 0)

    @pl.when(j >= 1)
    def _():
        qk_stage(0, 1, ia - 1)
        pv_stage(0, 0, ia - 2)
        qk_stage(0, 0, ia, diagonal=True)
        pv_stage(0, 1, ia - 1)

    qk_stage(1, 1, 0)
    pv_stage(0, 0, ia)
    finalize(0)

    def body_b(t, carry):
        qk_stage(1, 0, 2 * t + 1)
        pv_stage(1, 1, 2 * t)
        qk_stage(1, 1, 2 * t + 2)
        pv_stage(1, 0, 2 * t + 1)
        return carry

    lax.fori_loop(0, j, body_b, 0)

    qk_stage(1, 0, ib, diagonal=True)
    pv_stage(1, 1, ib - 1)
    pv_stage(1, 0, ib)
    finalize(1)


def _flash(qt, k, vt):
    B, S, _ = k.shape
    assert S % (2 * TQ) == 0
    pairs = MLA_HEADS // 2
    return pl.pallas_call(
        _flash_kernel,
        grid=(B, pairs, S // (2 * TQ)),
        in_specs=[pl.BlockSpec((1, 2 * MLA_QK_PAD, 2 * TQ), lambda b, p, j: (b, p, j)),
                  pl.BlockSpec((1, S, 2 * MLA_QK_PAD), lambda b, p, j: (b, 0, p)),
                  pl.BlockSpec((1, S // TQ, 2 * MLA_VT_ROWS, TQ), lambda b, p, j: (b, 0, p, 0))],
        out_specs=pl.BlockSpec((1, 2 * TQ, 2 * MLA_V), lambda b, p, j: (b, j, p)),
        out_shape=jax.ShapeDtypeStruct((B, S, MLA_HEADS * MLA_V), BF16),
        scratch_shapes=[pltpu.VMEM((2, TQ, TQ), F32),
                        pltpu.VMEM((2, TQ, TQ), F32),
                        pltpu.VMEM((2, 2, 1, TQ), F32),
                        pltpu.VMEM((2, 2, MLA_VT_ROWS, TQ), F32),
                        pltpu.VMEM((2, 2, 1, TQ), F32)],
        compiler_params=_params("arbitrary", "arbitrary", "arbitrary"),
        name="mla_flash",
    )(qt, k, vt)


def _pad_heads(w, heads, d, dp):
    lead = w.shape[:-1]
    w = w.reshape(lead + (heads, d))
    w = jnp.pad(w, [(0, 0)] * len(lead) + [(0, 0), (0, dp - d)])
    return w.reshape(lead + (heads * dp,))


def kernel(x, mem, positions, mem_norm, w_mem_kv, norm_mix0, w_in0, b_igate0, b_fgate0, w_conv0, w_hnorm0, w_out0, norm_ffn0, w_ff1_0, w_ff2_0, norm_mix1, w_in1, w_qnorm1, w_uq1, w_kvnorm1, w_ukv1, w_out1, norm_ffn1, w_ff1_1, w_ff2_1, final_norm):
    row = lambda g: g.reshape(1, -1).astype(F32)

    km, vmt = _mem_kv(mem, row(mem_norm), w_mem_kv[:, :MEM_WIDTH].astype(BF16),
                      w_mem_kv[:, MEM_WIDTH:].T.astype(BF16))

    nq = ML_HEADS * ML_QK_DIM
    o_v, o_o, o_g = 2 * nq, 2 * nq + MIX_WIDTH, 2 * nq + 2 * MIX_WIDTH
    o_qm = o_g + 2 * ML_HEADS
    wqk = jnp.concatenate([_pad_heads(w_in0[:, :nq], ML_HEADS, ML_QK_DIM, ML_QK_PAD),
                           _pad_heads(w_in0[:, nq:o_v], ML_HEADS, ML_QK_DIM, ML_QK_PAD)], axis=1)
    wvgt0 = jnp.concatenate([_pad_heads(w_in0[:, o_v:o_o], ML_HEADS, ML_V_DIM, ML_VT_ROWS),
                             jnp.pad(w_in0[:, o_g:o_qm], ((0, 0), (0, ML_GATE_ROWS - 2 * ML_HEADS)))], axis=1).T
    wot0 = w_in0[:, o_o:o_g].T
    wqmt0 = w_in0[:, o_qm:].T
    wconv = jnp.concatenate([_pad_heads(w_conv0[:, :nq], ML_HEADS, ML_QK_DIM, ML_QK_PAD),
                             _pad_heads(w_conv0[:, nq:], ML_HEADS, ML_QK_DIM, ML_QK_PAD)], axis=1).astype(F32)
    gate_bias = jnp.pad(jnp.concatenate([b_igate0, b_fgate0]), (0, ML_GATE_ROWS - 2 * ML_HEADS)).astype(F32)
    gate_bias = jnp.broadcast_to(gate_bias[:, None], (ML_GATE_ROWS, LANES))
    hnorm = jnp.broadcast_to(w_hnorm0.astype(F32)[:, :, None], (ML_HEADS, ML_V_DIM, LANES))

    qk, vt0, ot0, qm0, gates = _inproj0(x, row(norm_mix0), wqk.astype(BF16), wvgt0.astype(BF16),
                                        wot0.astype(BF16), wqmt0.astype(BF16))
    yt0 = _mlstm(qk, vt0, ot0, gates, wconv, gate_bias, hnorm)
    x = _outproj(x, yt0, qm0, km, vmt, w_out0.astype(BF16), True)
    x = _ffn(x, row(norm_ffn0), w_ff1_0, w_ff2_0, row(final_norm), False)

    o_kr = MLA_Q_RANK + MLA_KV_RANK
    o_qm1 = o_kr + MLA_ROPE
    win1 = w_in1[:, :o_kr]
    wqmt1 = w_in1[:, o_qm1:].T
    wkrt = jnp.pad(w_in1[:, o_kr:o_qm1].T, ((MLA_NOPE, LANES - MLA_NOPE - MLA_ROPE), (0, 0)))
    wuqt = _pad_heads(w_uq1, MLA_HEADS, MLA_NOPE + MLA_ROPE, MLA_QK_PAD).T
    ukv = w_ukv1.reshape(MLA_KV_RANK, MLA_HEADS, MLA_NOPE + MLA_V)
    wuk = _pad_heads(ukv[:, :, :MLA_NOPE].reshape(MLA_KV_RANK, -1), MLA_HEADS, MLA_NOPE, MLA_QK_PAD)
    wuvt = _pad_heads(ukv[:, :, MLA_NOPE:].reshape(MLA_KV_RANK, -1), MLA_HEADS, MLA_V, MLA_VT_ROWS).T

    cos_t, sin_t = _rope_tables(positions)
    qt1, k1, vt1, qm1 = _inproj1(x, row(norm_mix1), win1.astype(BF16), wqmt1.astype(BF16), wkrt.astype(BF16),
                                 row(w_qnorm1), row(w_kvnorm1), wuqt.astype(BF16), wuk.astype(BF16),
                                 wuvt.astype(BF16), cos_t, sin_t)
    y1 = _flash(qt1, k1, vt1)
    x = _outproj(x, y1, qm1, km, vmt, w_out1.astype(BF16), False)
    return _ffn(x, row(norm_ffn1), w_ff1_1, w_ff2_1, row(final_norm), True)
```

```python
import functools
import math

import jax
import jax.numpy as jnp
from jax import lax
from jax.experimental import pallas as pl
from jax.experimental.pallas import tpu as pltpu

F32 = jnp.float32
BF16 = jnp.bfloat16
EPS = 1e-6

D_MODEL = 1024
N_MEM = 256
MEM_HEADS = 4
MEM_HEAD_DIM = 64
MEM_WIDTH = MEM_HEADS * MEM_HEAD_DIM
MIX_WIDTH = D_MODEL - MEM_WIDTH

ML_HEADS = 4
ML_V_DIM = MIX_WIDTH // ML_HEADS
ML_QK_DIM = ML_V_DIM // 2
ML_CONV = 4
ML_CHUNK = 128
ML_QK_PAD = 128
ML_VT_ROWS = 256
ML_ONES_ROW = ML_V_DIM
ML_NB = 4
ML_GATE_ROWS = 16

MLA_HEADS = 12
MLA_NOPE = 64
MLA_ROPE = 32
MLA_V = 64
MLA_Q_RANK = 384
MLA_KV_RANK = 256
MLA_QK_PAD = 128
MLA_VT_ROWS = 80
ROPE_THETA = 10000.0
D_FF = 4 * D_MODEL

LANES = 128
VMEM_LIMIT = 56 * 1024 * 1024

TM_PROJ = 512
FF_CHUNK = 1024
TQ = 512

_NT = (((1,), (1,)), ((), ()))


def _params(*sem):
    return pltpu.CompilerParams(dimension_semantics=sem, vmem_limit_bytes=VMEM_LIMIT)


def _rms(x, g):
    return x * lax.rsqrt(jnp.mean(x * x, axis=-1, keepdims=True) + EPS) * g


def _const_spec(shape):
    nd = len(shape)
    return pl.BlockSpec(shape, lambda *_: (0,) * nd)


def _single_buffered_spec(shape):
    nd = len(shape)
    return pl.BlockSpec(shape, lambda *_: (0,) * nd, pipeline_mode=pl.Buffered(1))


def _mem_kv_kernel(mem_ref, g_ref, wk_ref, wvt_ref, k_ref, vt_ref):
    xn = _rms(mem_ref[0], g_ref[...]).astype(BF16)
    k = jnp.dot(xn, wk_ref[...], preferred_element_type=F32)
    vt = lax.dot_general(wvt_ref[...], xn, _NT, preferred_element_type=F32)
    k = k * (MEM_HEAD_DIM ** -0.5)
    col_head = lax.broadcasted_iota(jnp.int32, k.shape, 1) // MEM_HEAD_DIM
    row_head = lax.broadcasted_iota(jnp.int32, vt.shape, 0) // MEM_HEAD_DIM
    for h in range(MEM_HEADS):
        k_ref[0, h] = jnp.where(col_head == h, k, 0.0).astype(BF16)
        vt_ref[0, h] = jnp.where(row_head == h, vt, 0.0).astype(BF16)


def _mem_kv(mem, g, wk, wvt):
    B = mem.shape[0]
    out = jax.ShapeDtypeStruct((B, MEM_HEADS, MEM_WIDTH, N_MEM), BF16)
    return pl.pallas_call(
        _mem_kv_kernel,
        grid=(B,),
        in_specs=[pl.BlockSpec((1, N_MEM, D_MODEL), lambda b: (b, 0, 0)),
                  _const_spec(g.shape), _const_spec(wk.shape), _const_spec(wvt.shape)],
        out_specs=[pl.BlockSpec((1, MEM_HEADS, N_MEM, MEM_WIDTH), lambda b: (b, 0, 0, 0)),
                   pl.BlockSpec((1, MEM_HEADS, MEM_WIDTH, N_MEM), lambda b: (b, 0, 0, 0))],
        out_shape=[out, out],
        compiler_params=_params("arbitrary"),
        name="mem_kv",
    )(mem, g, wk, wvt)


def _inproj0_kernel(x_ref, g_ref, wqk_ref, wv_ref, wo_ref, wqm_ref,
                    qk_ref, v_ref, o_ref, qm_ref, gate_ref):
    xn = _rms(x_ref[0], g_ref[...]).astype(BF16)
    qk_ref[0] = jnp.dot(xn, wqk_ref[...], preferred_element_type=F32).astype(BF16)
    vg = lax.dot_general(wv_ref[...], xn, _NT, preferred_element_type=F32)
    nv = v_ref.shape[1]
    vt = vg[:nv]
    rowid = lax.broadcasted_iota(jnp.int32, vt.shape, 0) % ML_VT_ROWS
    v_ref[0] = jnp.where(rowid == ML_ONES_ROW, 1.0, vt).astype(BF16)
    gate_ref[0] = vg[nv:]
    o_ref[0] = lax.dot_general(wo_ref[...], xn, _NT, preferred_element_type=F32).astype(BF16)
    qm_ref[0] = lax.dot_general(wqm_ref[...], xn, _NT, preferred_element_type=F32).astype(BF16)


def _inproj0(x, g, wqk, wvgt, wot, wqmt):
    B, S, D = x.shape
    tm = min(TM_PROJ, S)
    tok = lambda w: pl.BlockSpec((1, tm, w), lambda b, i: (b, i, 0))
    tok_t = lambda r: pl.BlockSpec((1, r, tm), lambda b, i: (b, 0, i))
    consts = (g, wqk, wvgt, wot, wqmt)
    nv = wvgt.shape[0] - ML_GATE_ROWS
    return pl.pallas_call(
        _inproj0_kernel,
        grid=(B, S // tm),
        in_specs=[tok(D)] + [_const_spec(a.shape) for a in consts],
        out_specs=[tok(wqk.shape[1]), tok_t(nv), tok_t(wot.shape[0]), tok_t(wqmt.shape[0]),
                   tok_t(ML_GATE_ROWS)],
        out_shape=[jax.ShapeDtypeStruct((B, S, wqk.shape[1]), BF16),
                   jax.ShapeDtypeStruct((B, nv, S), BF16),
                   jax.ShapeDtypeStruct((B, wot.shape[0], S), BF16),
                   jax.ShapeDtypeStruct((B, wqmt.shape[0], S), BF16),
                   jax.ShapeDtypeStruct((B, ML_GATE_ROWS, S), F32)],
        compiler_params=_params("arbitrary", "arbitrary"),
        name="inproj0",
    )(x, *consts)


def _split3(x):
    hi = x.astype(BF16)
    r = x - hi.astype(F32)
    mid = r.astype(BF16)
    lo = (r - mid.astype(F32)).astype(BF16)
    return hi, mid, lo


def _mlstm_kernel(qk_ref, vt_ref, o_ref, gate_ref, wconv_ref, bias_ref, hnorm_ref,
                  y_ref, ct_ref, m_ref, tail_ref):
    L = ML_CHUNK
    HQ = ML_HEADS * ML_QK_PAD

    @pl.when(pl.program_id(1) == 0)
    def _():
        ct_ref[...] = jnp.zeros_like(ct_ref)
        m_ref[...] = jnp.zeros_like(m_ref)
        tail_ref[...] = jnp.zeros_like(tail_ref)

    key_i = lax.broadcasted_iota(jnp.int32, (L, L), 0)
    qry_i = lax.broadcasted_iota(jnp.int32, (L, L), 1)
    causal_t = key_i <= qry_i
    triu = jnp.where(causal_t, 1.0, 0.0).astype(BF16)
    wc = wconv_ref[...]

    def prepare(nb):
        x = qk_ref[nb].astype(F32)
        tail = tail_ref[nb]
        row8 = lax.broadcasted_iota(jnp.int32, tail.shape, 0)
        conv = x * wc[ML_CONV - 1:ML_CONV]
        for s in range(1, ML_CONV):
            xs = pltpu.roll(x, s, axis=0)
            head = jnp.where(row8 < s, pltpu.roll(tail, s, axis=0), xs[0:8])
            xs = jnp.concatenate([head, xs[8:]], axis=0)
            conv = conv + xs * wc[ML_CONV - 1 - s:ML_CONV - s]
        tail_ref[nb] = x[L - 8:L]
        act = conv * jax.nn.sigmoid(conv)
        q = act[:, :HQ].astype(BF16)
        k = (act[:, HQ:] * (ML_QK_DIM ** -0.5)).astype(BF16)

        g_rows = gate_ref[nb] + bias_ref[...]
        log_f = jnp.minimum(g_rows, 0.0) - jnp.log1p(jnp.exp(-jnp.abs(g_rows)))
        b_rows = sum(jnp.dot(part, triu, preferred_element_type=F32) for part in _split3(log_f))
        c_rows = g_rows[:ML_HEADS] - b_rows[ML_HEADS:2 * ML_HEADS]
        c_cols = jnp.concatenate([c_rows, jnp.zeros((L - ML_HEADS, L), F32)], axis=0).T
        return q, k, b_rows, c_rows, c_cols

    def head(nb, h, q, k, b_rows, c_rows, c_cols):
        b_row = b_rows[ML_HEADS + h:ML_HEADS + h + 1, :]
        c_row = c_rows[h:h + 1, :]
        c_col = c_cols[:, h:h + 1]
        m_prev = m_ref[nb, h][:, 0:1]
        qh = q[:, h * ML_QK_PAD:(h + 1) * ML_QK_PAD]
        kh = k[:, h * ML_QK_PAD:(h + 1) * ML_QK_PAD]
        vth = vt_ref[nb, h * ML_VT_ROWS:(h + 1) * ML_VT_ROWS, :]
        ct_prev = ct_ref[nb, h]

        cm = jnp.where(causal_t, c_col, -jnp.inf)
        m_row = jnp.maximum(m_prev, jnp.max(cm, axis=0, keepdims=True))
        w_t = jnp.exp(cm - m_row)
        a_row = jnp.exp(m_prev - m_row)
        s_t = (lax.dot_general(kh, qh, _NT, preferred_element_type=F32) * w_t).astype(BF16)
        num_t = (jnp.dot(vth, s_t, preferred_element_type=F32)
                 + a_row * lax.dot_general(ct_prev.astype(BF16), qh, _NT, preferred_element_type=F32))
        den = num_t[ML_ONES_ROW:ML_ONES_ROW + 1]
        inv = 1.0 / jnp.maximum(jnp.abs(den), jnp.exp(-(b_row + m_row)))
        hc_t = num_t[:ML_V_DIM] * inv
        ms = jnp.sum(hc_t * hc_t, axis=0, keepdims=True) * (1.0 / ML_V_DIM)
        hn_t = hc_t * lax.rsqrt(ms + EPS) * hnorm_ref[h]
        og_t = o_ref[nb, h * ML_V_DIM:(h + 1) * ML_V_DIM, :].astype(F32)
        y_ref[nb, h * ML_V_DIM:(h + 1) * ML_V_DIM, :] = (hn_t * jax.nn.sigmoid(og_t)).astype(BF16)

        m_end = m_row[:, L - 1:L]
        u_row = jnp.exp(c_row - m_end)
        decay = jnp.exp(m_prev - m_end)
        uv_t = (vth.astype(F32) * u_row).astype(BF16)
        ct_ref[nb, h] = decay * ct_prev + jnp.dot(uv_t, kh, preferred_element_type=F32)
        m_ref[nb, h] = jnp.broadcast_to(b_row[:, L - 1:L] + m_end, (1, LANES))

    prepared = [prepare(nb) for nb in range(ML_NB)]
    for h in range(ML_HEADS):
        for nb in range(ML_NB):
            head(nb, h, *prepared[nb])


def _mlstm(qk, vt, ot, gates, wconv, bias, hnorm):
    B, S, _ = qk.shape
    L = ML_CHUNK
    assert B % ML_NB == 0
    tok = lambda w: pl.BlockSpec((ML_NB, L, w), lambda b, c: (b, c, 0))
    tok_t = lambda r: pl.BlockSpec((ML_NB, r, L), lambda b, c: (b, 0, c))
    return pl.pallas_call(
        _mlstm_kernel,
        grid=(B // ML_NB, S // L),
        in_specs=[tok(qk.shape[2]), tok_t(vt.shape[1]), tok_t(ot.shape[1]), tok_t(gates.shape[1]),
                  _const_spec(wconv.shape), _const_spec(bias.shape), _const_spec(hnorm.shape)],
        out_specs=tok_t(ot.shape[1]),
        out_shape=jax.ShapeDtypeStruct(ot.shape, BF16),
        scratch_shapes=[pltpu.VMEM((ML_NB, ML_HEADS, ML_VT_ROWS, ML_QK_PAD), F32),
                        pltpu.VMEM((ML_NB, ML_HEADS, 1, LANES), F32),
                        pltpu.VMEM((ML_NB, 8, qk.shape[2]), F32)],
        compiler_params=_params("arbitrary", "arbitrary"),
        name="mlstm",
    )(qk, vt, ot, gates, wconv, bias, hnorm)


def _outproj_kernel(x_ref, yt_ref, qmt_ref, km_ref, vmt_ref, w_ref, out_ref):
    tm = x_ref.shape[1]
    halves = [slice(r0, r0 + tm // 2) for r0 in range(0, tm, tm // 2)]
    scores = [[jnp.dot(km_ref[0, h], qmt_ref[0, :, cols], preferred_element_type=F32)
               for h in range(MEM_HEADS)] for cols in halves]
    for cols, s_heads in zip(halves, scores):
        ymem_t = None
        for h in range(MEM_HEADS):
            s = s_heads[h]
            e = jnp.exp(s - jnp.max(s, axis=0, keepdims=True))
            p = (e * (1.0 / jnp.sum(e, axis=0, keepdims=True))).astype(BF16)
            oh = jnp.dot(vmt_ref[0, h], p, preferred_element_type=F32)
            ymem_t = oh if ymem_t is None else ymem_t + oh
        ycat = jnp.concatenate([yt_ref[0, :, cols], ymem_t.astype(BF16)], axis=0).T
        out_ref[0, cols, :] = x_ref[0, cols, :] + jnp.dot(ycat, w_ref[...], preferred_element_type=F32)


def _outproj(x, yt, qmt, km, vmt, w):
    B, S, D = x.shape
    tm = min(TM_PROJ, S)
    tok = lambda w: pl.BlockSpec((1, tm, w), lambda b, i: (b, i, 0))
    tok_t = lambda r: pl.BlockSpec((1, r, tm), lambda b, i: (b, 0, i))
    per_b = pl.BlockSpec((1,) + km.shape[1:], lambda b, i: (b, 0, 0, 0))
    return pl.pallas_call(
        _outproj_kernel,
        grid=(B, S // tm),
        in_specs=[tok(D), tok_t(yt.shape[1]), tok_t(qmt.shape[1]), per_b, per_b,
                  _const_spec(w.shape)],
        out_specs=tok(D),
        out_shape=jax.ShapeDtypeStruct(x.shape, F32),
        compiler_params=_params("arbitrary", "arbitrary"),
        name="outproj",
    )(x, yt, qmt, km, vmt, w)


def _ffn_kernel(x_ref, g_ref, w1_ref, w2_ref, gf_ref, out_ref, *, final_norm):
    x = x_ref[0]
    hn = _rms(x, g_ref[...]).astype(BF16)
    acc = x
    for c in range(D_FF // FF_CHUNK):
        w1c = w1_ref[:, c * FF_CHUNK:(c + 1) * FF_CHUNK].astype(BF16)
        a = jnp.dot(hn, w1c, preferred_element_type=F32)
        a = jnp.square(jnp.maximum(a, 0.0)).astype(BF16)
        w2c = w2_ref[c * FF_CHUNK:(c + 1) * FF_CHUNK, :].astype(BF16)
        acc = acc + jnp.dot(a, w2c, preferred_element_type=F32)
    if final_norm:
        acc = _rms(acc, gf_ref[...])
    out_ref[0] = acc


def _ffn(x, g, w1, w2, gf, final_norm):
    B, S, D = x.shape
    tm = min(TM_PROJ, S)
    tok = pl.BlockSpec((1, tm, D), lambda b, i: (b, i, 0))
    return pl.pallas_call(
        functools.partial(_ffn_kernel, final_norm=final_norm),
        grid=(B, S // tm),
        in_specs=[tok, _const_spec(g.shape), _single_buffered_spec(w1.shape), _single_buffered_spec(w2.shape),
                  _const_spec(gf.shape)],
        out_specs=tok,
        out_shape=jax.ShapeDtypeStruct(x.shape, F32),
        compiler_params=_params("arbitrary", "arbitrary"),
        name="ffn",
    )(x, g, w1, w2, gf)


def _rope_kernel(pos_ref, inv_ref, cos_ref, sin_ref):
    ang = pos_ref[0].astype(F32) * inv_ref[...]
    cos_ref[0] = jnp.cos(ang)
    sin_ref[0] = jnp.sin(ang)


def _rope_tables(positions):
    B, S = positions.shape
    nf = MLA_ROPE // 2
    inv = (ROPE_THETA ** (-jnp.arange(0, MLA_ROPE, 2, dtype=F32) / MLA_ROPE)).reshape(nf, 1)
    out = jax.ShapeDtypeStruct((B, nf, S), F32)
    spec = pl.BlockSpec((1, nf, S), lambda b: (b, 0, 0))
    return pl.pallas_call(
        _rope_kernel,
        grid=(B,),
        in_specs=[pl.BlockSpec((1, 1, S), lambda b: (b, 0, 0)), _const_spec(inv.shape)],
        out_specs=[spec, spec],
        out_shape=[out, out],
        compiler_params=_params("arbitrary"),
        name="rope_tables",
    )(positions.reshape(B, 1, S), inv)


def _inproj1_kernel(x_ref, g_ref, win_ref, wqmt_ref, wkrt_ref, gq_ref, gkv_ref, wuqt_ref, wuk_ref, wuvt_ref,
                    cos_ref, sin_ref, qt_ref, k_ref, vt_ref, qm_ref):
    half = MLA_ROPE // 2
    xn = _rms(x_ref[0], g_ref[...]).astype(BF16)
    c = jnp.dot(xn, win_ref[...], preferred_element_type=F32)
    o_kv = MLA_Q_RANK
    qm_ref[0] = lax.dot_general(wqmt_ref[...], xn, _NT, preferred_element_type=F32).astype(BF16)
    cq = _rms(c[:, :o_kv], gq_ref[...]).astype(BF16)
    ckv = _rms(c[:, o_kv:], gkv_ref[...]).astype(BF16)
    cos_t, sin_t = cos_ref[0], sin_ref[0]

    def rope_t(x1, x2):
        return x1 * cos_t - x2 * sin_t, x2 * cos_t + x1 * sin_t

    qscale = ((MLA_NOPE + MLA_ROPE) ** -0.5) * math.log2(math.e)
    qt = lax.dot_general(wuqt_ref[...], cq, _NT, preferred_element_type=F32)
    zpad = jnp.zeros((MLA_QK_PAD - MLA_NOPE - MLA_ROPE, qt.shape[1]), F32)
    for h in range(MLA_HEADS):
        b0 = h * MLA_QK_PAD
        r1, r2 = rope_t(qt[b0 + MLA_NOPE:b0 + MLA_NOPE + half],
                        qt[b0 + MLA_NOPE + half:b0 + MLA_NOPE + MLA_ROPE])
        blk = jnp.concatenate([qt[b0:b0 + MLA_NOPE], r1, r2, zpad], axis=0) * qscale
        qt_ref[0, b0:b0 + MLA_QK_PAD] = blk.astype(BF16)

    krt = lax.dot_general(wkrt_ref[...], xn, _NT, preferred_element_type=F32)
    r1, r2 = rope_t(krt[MLA_NOPE:MLA_NOPE + half], krt[MLA_NOPE + half:MLA_NOPE + MLA_ROPE])
    kr = jnp.concatenate([krt[:MLA_NOPE], r1, r2, krt[MLA_NOPE + MLA_ROPE:]], axis=0).T
    k = jnp.dot(ckv, wuk_ref[...], preferred_element_type=F32)
    for h in range(MLA_HEADS):
        sl = slice(h * MLA_QK_PAD, (h + 1) * MLA_QK_PAD)
        k_ref[0, :, sl] = (k[:, sl] + kr).astype(BF16)

    vt = lax.dot_general(wuvt_ref[...], ckv, _NT, preferred_element_type=F32)
    rowid = lax.broadcasted_iota(jnp.int32, vt.shape, 0) % MLA_VT_ROWS
    vt_ref[0, 0] = jnp.where(rowid == MLA_V, 1.0, vt).astype(BF16)


def _inproj1(x, g, win, wqmt, wkrt, gq, gkv, wuqt, wuk, wuvt, cos_t, sin_t):
    B, S, D = x.shape
    tm = TQ
    assert S % tm == 0
    tok = lambda w: pl.BlockSpec((1, tm, w), lambda b, i: (b, i, 0))
    tok_t = lambda r: pl.BlockSpec((1, r, tm), lambda b, i: (b, 0, i))
    consts = (g, win, wqmt, wkrt, gq, gkv, wuqt, wuk, wuvt)
    nvt = wuvt.shape[0]
    return pl.pallas_call(
        _inproj1_kernel,
        grid=(B, S // tm),
        in_specs=[tok(D)] + [_const_spec(a.shape) for a in consts] + [tok_t(cos_t.shape[1])] * 2,
        out_specs=[tok_t(wuqt.shape[0]), tok(wuk.shape[1]),
                   pl.BlockSpec((1, 1, nvt, tm), lambda b, i: (b, i, 0, 0)), tok_t(MEM_WIDTH)],
        out_shape=[jax.ShapeDtypeStruct((B, wuqt.shape[0], S), BF16),
                   jax.ShapeDtypeStruct((B, S, wuk.shape[1]), BF16),
                   jax.ShapeDtypeStruct((B, S // tm, nvt, tm), BF16),
                   jax.ShapeDtypeStruct((B, MEM_WIDTH, S), BF16)],
        compiler_params=_params("arbitrary", "arbitrary"),
        name="inproj1",
    )(x, *consts, cos_t, sin_t)


def _flash_kernel(qt_ref, k_ref, vt_ref, y_ref, sa_ref, sb_ref, smax_ref, acc_ref, m_ref):
    j = pl.program_id(2)
    ia = 2 * j
    ib = 2 * j + 1
    slots = (sa_ref, sb_ref)

    def scores_t(qt, hh, blk):
        kh = k_ref[0, pl.ds(pl.multiple_of(blk * TQ, TQ), TQ), hh * MLA_QK_PAD:(hh + 1) * MLA_QK_PAD]
        qh = qt_ref[0, hh * MLA_QK_PAD:(hh + 1) * MLA_QK_PAD, qt * TQ:(qt + 1) * TQ]
        return jnp.dot(kh, qh, preferred_element_type=F32)

    def v_t(hh, blk):
        return vt_ref[0, blk, hh * MLA_VT_ROWS:(hh + 1) * MLA_VT_ROWS, :]

    def col_max(s):
        parts = [jnp.max(s[r:r + 128], axis=0, keepdims=True) for r in range(0, TQ, 128)]
        return jnp.maximum(jnp.maximum(parts[0], parts[1]), jnp.maximum(parts[2], parts[3]))

    key_i = lax.broadcasted_iota(jnp.int32, (TQ, TQ), 0)
    qry_i = lax.broadcasted_iota(jnp.int32, (TQ, TQ), 1)

    def qk_stage(qt, slot, blk, diagonal=False):
        for hh in range(2):
            s = scores_t(qt, hh, blk)
            if diagonal:
                s = jnp.where(key_i <= qry_i, s, -jnp.inf)
            slots[slot][hh] = s
            smax_ref[slot, hh] = col_max(s)

    def pv_stage(qt, slot, blk):
        for hh in range(2):
            m_prev = m_ref[qt, hh]
            m_new = jnp.maximum(m_prev, smax_ref[slot, hh])
            alpha = jnp.exp2(m_prev - m_new)
            p = jnp.exp2(slots[slot][hh] - m_new).astype(BF16)
            acc_ref[qt, hh] = alpha * acc_ref[qt, hh] + jnp.dot(v_t(hh, blk), p, preferred_element_type=F32)
            m_ref[qt, hh] = m_new

    def finalize(qt):
        outs = []
        for hh in range(2):
            acc = acc_ref[qt, hh]
            outs.append(acc[:MLA_V] * (1.0 / acc[MLA_V:MLA_V + 1]))
        y_ref[0, :, qt * TQ:(qt + 1) * TQ] = jnp.concatenate(outs, axis=0).astype(BF16)

    m_ref[...] = jnp.full(m_ref.shape, -jnp.inf, F32)
    acc_ref[...] = jnp.zeros_like(acc_ref)

    @pl.when(j == 0)
    def _():
        qk_stage(0, 0, 0, diagonal=True)

    @pl.when(j >= 1)
    def _():
        qk_stage(0, 0, 0)

    def body_a(t, carry):
        qk_stage(0, 1, 2 * t + 1)
        pv_stage(0, 0, 2 * t)
        qk_stage(0, 0, 2 * t + 2)
        pv_stage(0, 1, 2 * t + 1)
        return carry

    lax.fori_loop(0, j - 1, body_a, 0)

    @pl.when(j >= 1)
    def _():
        qk_stage(0, 1, ia - 1)
        pv_stage(0, 0, ia - 2)
        qk_stage(0, 0, ia, diagonal=True)
        pv_stage(0, 1, ia - 1)

    qk_stage(1, 1, 0)
    pv_stage(0, 0, ia)
    finalize(0)

    def body_b(t, carry):
        qk_stage(1, 0, 2 * t + 1)
        pv_stage(1, 1, 2 * t)
        qk_stage(1, 1, 2 * t + 2)
        pv_stage(1, 0, 2 * t + 1)
        return carry

    lax.fori_loop(0, j, body_b, 0)

    qk_stage(1, 0, ib, diagonal=True)
    pv_stage(1, 1, ib - 1)
    pv_stage(1, 0, ib)
    finalize(1)


def _flash(qt, k, vt):
    B, S, _ = k.shape
    assert S % (2 * TQ) == 0
    pairs = MLA_HEADS // 2
    return pl.pallas_call(
        _flash_kernel,
        grid=(B, pairs, S // (2 * TQ)),
        in_specs=[pl.BlockSpec((1, 2 * MLA_QK_PAD, 2 * TQ), lambda b, p, j: (b, p, j)),
                  pl.BlockSpec((1, S, 2 * MLA_QK_PAD), lambda b, p, j: (b, 0, p)),
                  pl.BlockSpec((1, S // TQ, 2 * MLA_VT_ROWS, TQ), lambda b, p, j: (b, 0, p, 0))],
        out_specs=pl.BlockSpec((1, 2 * MLA_V, 2 * TQ), lambda b, p, j: (b, p, j)),
        out_shape=jax.ShapeDtypeStruct((B, MLA_HEADS * MLA_V, S), BF16),
        scratch_shapes=[pltpu.VMEM((2, TQ, TQ), F32),
                        pltpu.VMEM((2, TQ, TQ), F32),
                        pltpu.VMEM((2, 2, 1, TQ), F32),
                        pltpu.VMEM((2, 2, MLA_VT_ROWS, TQ), F32),
                        pltpu.VMEM((2, 2, 1, TQ), F32)],
        compiler_params=_params("arbitrary", "arbitrary", "arbitrary"),
        name="mla_flash",
    )(qt, k, vt)


def _pad_heads(w, heads, d, dp):
    lead = w.shape[:-1]
    w = w.reshape(lead + (heads, d))
    w = jnp.pad(w, [(0, 0)] * len(lead) + [(0, 0), (0, dp - d)])
    return w.reshape(lead + (heads * dp,))


def kernel(x, mem, positions, mem_norm, w_mem_kv, norm_mix0, w_in0, b_igate0, b_fgate0, w_conv0, w_hnorm0, w_out0, norm_ffn0, w_ff1_0, w_ff2_0, norm_mix1, w_in1, w_qnorm1, w_uq1, w_kvnorm1, w_ukv1, w_out1, norm_ffn1, w_ff1_1, w_ff2_1, final_norm):
    row = lambda g: g.reshape(1, -1).astype(F32)

    km, vmt = _mem_kv(mem, row(mem_norm), w_mem_kv[:, :MEM_WIDTH].astype(BF16),
                      w_mem_kv[:, MEM_WIDTH:].T.astype(BF16))

    nq = ML_HEADS * ML_QK_DIM
    o_v, o_o, o_g = 2 * nq, 2 * nq + MIX_WIDTH, 2 * nq + 2 * MIX_WIDTH
    o_qm = o_g + 2 * ML_HEADS
    wqk = jnp.concatenate([_pad_heads(w_in0[:, :nq], ML_HEADS, ML_QK_DIM, ML_QK_PAD),
                           _pad_heads(w_in0[:, nq:o_v], ML_HEADS, ML_QK_DIM, ML_QK_PAD)], axis=1)
    wvgt0 = jnp.concatenate([_pad_heads(w_in0[:, o_v:o_o], ML_HEADS, ML_V_DIM, ML_VT_ROWS),
                             jnp.pad(w_in0[:, o_g:o_qm], ((0, 0), (0, ML_GATE_ROWS - 2 * ML_HEADS)))], axis=1).T
    wot0 = w_in0[:, o_o:o_g].T
    wqmt0 = w_in0[:, o_qm:].T
    wconv = jnp.concatenate([_pad_heads(w_conv0[:, :nq], ML_HEADS, ML_QK_DIM, ML_QK_PAD),
                             _pad_heads(w_conv0[:, nq:], ML_HEADS, ML_QK_DIM, ML_QK_PAD)], axis=1).astype(F32)
    gate_bias = jnp.pad(jnp.concatenate([b_igate0, b_fgate0]), (0, ML_GATE_ROWS - 2 * ML_HEADS)).astype(F32)
    gate_bias = jnp.broadcast_to(gate_bias[:, None], (ML_GATE_ROWS, LANES))
    hnorm = jnp.broadcast_to(w_hnorm0.astype(F32)[:, :, None], (ML_HEADS, ML_V_DIM, LANES))

    qk, vt0, ot0, qm0, gates = _inproj0(x, row(norm_mix0), wqk.astype(BF16), wvgt0.astype(BF16),
                                        wot0.astype(BF16), wqmt0.astype(BF16))
    yt0 = _mlstm(qk, vt0, ot0, gates, wconv, gate_bias, hnorm)
    x = _outproj(x, yt0, qm0, km, vmt, w_out0.astype(BF16))
    x = _ffn(x, row(norm_ffn0), w_ff1_0, w_ff2_0, row(final_norm), False)

    o_kr = MLA_Q_RANK + MLA_KV_RANK
    o_qm1 = o_kr + MLA_ROPE
    win1 = w_in1[:, :o_kr]
    wqmt1 = w_in1[:, o_qm1:].T
    wkrt = jnp.pad(w_in1[:, o_kr:o_qm1].T, ((MLA_NOPE, LANES - MLA_NOPE - MLA_ROPE), (0, 0)))
    wuqt = _pad_heads(w_uq1, MLA_HEADS, MLA_NOPE + MLA_ROPE, MLA_QK_PAD).T
    ukv = w_ukv1.reshape(MLA_KV_RANK, MLA_HEADS, MLA_NOPE + MLA_V)
    wuk = _pad_heads(ukv[:, :, :MLA_NOPE].reshape(MLA_KV_RANK, -1), MLA_HEADS, MLA_NOPE, MLA_QK_PAD)
    wuvt = _pad_heads(ukv[:, :, MLA_NOPE:].reshape(MLA_KV_RANK, -1), MLA_HEADS, MLA_V, MLA_VT_ROWS).T

    cos_t, sin_t = _rope_tables(positions)
    qt1, k1, vt1, qm1 = _inproj1(x, row(norm_mix1), win1.astype(BF16), wqmt1.astype(BF16), wkrt.astype(BF16),
                                 row(w_qnorm1), row(w_kvnorm1), wuqt.astype(BF16), wuk.astype(BF16),
                                 wuvt.astype(BF16), cos_t, sin_t)
    yt1 = _flash(qt1, k1, vt1)
    x = _outproj(x, yt1, qm1, km, vmt, w_out1.astype(BF16))
    return _ffn(x, row(norm_ffn1), w_ff1_1, w_ff2_1, row(final_norm), True)
```

```python
import functools
import math

import jax
import jax.numpy as jnp
from jax import lax
from jax.experimental import pallas as pl
from jax.experimental.pallas import tpu as pltpu

F32 = jnp.float32
BF16 = jnp.bfloat16
EPS = 1e-6

D_MODEL = 1024
N_MEM = 256
MEM_HEADS = 4
MEM_HEAD_DIM = 64
MEM_WIDTH = MEM_HEADS * MEM_HEAD_DIM
MIX_WIDTH = D_MODEL - MEM_WIDTH

ML_HEADS = 4
ML_V_DIM = MIX_WIDTH // ML_HEADS
ML_QK_DIM = ML_V_DIM // 2
ML_CONV = 4
ML_CHUNK = 128
ML_QK_PAD = 128
ML_VT_ROWS = 256
ML_ONES_ROW = ML_V_DIM
ML_NB = 4
ML_GATE_ROWS = 16

MLA_HEADS = 12
MLA_NOPE = 64
MLA_ROPE = 32
MLA_V = 64
MLA_Q_RANK = 384
MLA_KV_RANK = 256
MLA_QK_PAD = 128
MLA_VT_ROWS = 80
ROPE_THETA = 10000.0
D_FF = 4 * D_MODEL

LANES = 128
VMEM_LIMIT = 56 * 1024 * 1024

TM_PROJ = 512
FF_CHUNK = 1024
TQ = 512

_NT = (((1,), (1,)), ((), ()))


def _params(*sem):
    return pltpu.CompilerParams(dimension_semantics=sem, vmem_limit_bytes=VMEM_LIMIT)


def _rms(x, g):
    return x * lax.rsqrt(jnp.mean(x * x, axis=-1, keepdims=True) + EPS) * g


def _const_spec(shape):
    nd = len(shape)
    return pl.BlockSpec(shape, lambda *_: (0,) * nd)


def _single_buffered_spec(shape):
    nd = len(shape)
    return pl.BlockSpec(shape, lambda *_: (0,) * nd, pipeline_mode=pl.Buffered(1))


def _mem_kv_kernel(mem_ref, g_ref, wk_ref, wvt_ref, k_ref, vt_ref):
    xn = _rms(mem_ref[0], g_ref[...]).astype(BF16)
    k = jnp.dot(xn, wk_ref[...], preferred_element_type=F32)
    vt = lax.dot_general(wvt_ref[...], xn, _NT, preferred_element_type=F32)
    k = k * (MEM_HEAD_DIM ** -0.5)
    col_head = lax.broadcasted_iota(jnp.int32, k.shape, 1) // MEM_HEAD_DIM
    row_head = lax.broadcasted_iota(jnp.int32, vt.shape, 0) // MEM_HEAD_DIM
    for h in range(MEM_HEADS):
        k_ref[0, h] = jnp.where(col_head == h, k, 0.0).astype(BF16)
        vt_ref[0, h] = jnp.where(row_head == h, vt, 0.0).astype(BF16)


def _mem_kv(mem, g, wk, wvt):
    B = mem.shape[0]
    out = jax.ShapeDtypeStruct((B, MEM_HEADS, MEM_WIDTH, N_MEM), BF16)
    return pl.pallas_call(
        _mem_kv_kernel,
        grid=(B,),
        in_specs=[pl.BlockSpec((1, N_MEM, D_MODEL), lambda b: (b, 0, 0)),
                  _const_spec(g.shape), _const_spec(wk.shape), _const_spec(wvt.shape)],
        out_specs=[pl.BlockSpec((1, MEM_HEADS, N_MEM, MEM_WIDTH), lambda b: (b, 0, 0, 0)),
                   pl.BlockSpec((1, MEM_HEADS, MEM_WIDTH, N_MEM), lambda b: (b, 0, 0, 0))],
        out_shape=[out, out],
        compiler_params=_params("arbitrary"),
        name="mem_kv",
    )(mem, g, wk, wvt)


def _inproj0_kernel(x_ref, g_ref, wqk_ref, wv_ref, wo_ref, wqm_ref,
                    qk_ref, v_ref, o_ref, qm_ref, gate_ref):
    xn = _rms(x_ref[0], g_ref[...]).astype(BF16)
    qk_ref[0] = jnp.dot(xn, wqk_ref[...], preferred_element_type=F32).astype(BF16)
    vg = lax.dot_general(wv_ref[...], xn, _NT, preferred_element_type=F32)
    nv = v_ref.shape[1]
    vt = vg[:nv]
    rowid = lax.broadcasted_iota(jnp.int32, vt.shape, 0) % ML_VT_ROWS
    v_ref[0] = jnp.where(rowid == ML_ONES_ROW, 1.0, vt).astype(BF16)
    gate_ref[0] = vg[nv:]
    o_ref[0] = lax.dot_general(wo_ref[...], xn, _NT, preferred_element_type=F32).astype(BF16)
    qm_ref[0] = lax.dot_general(wqm_ref[...], xn, _NT, preferred_element_type=F32).astype(BF16)


def _inproj0(x, g, wqk, wvgt, wot, wqmt):
    B, S, D = x.shape
    tm = min(TM_PROJ, S)
    tok = lambda w: pl.BlockSpec((1, tm, w), lambda b, i: (b, i, 0))
    tok_t = lambda r: pl.BlockSpec((1, r, tm), lambda b, i: (b, 0, i))
    consts = (g, wqk, wvgt, wot, wqmt)
    nv = wvgt.shape[0] - ML_GATE_ROWS
    return pl.pallas_call(
        _inproj0_kernel,
        grid=(B, S // tm),
        in_specs=[tok(D)] + [_const_spec(a.shape) for a in consts],
        out_specs=[tok(wqk.shape[1]), tok_t(nv), tok_t(wot.shape[0]), tok_t(wqmt.shape[0]),
                   tok_t(ML_GATE_ROWS)],
        out_shape=[jax.ShapeDtypeStruct((B, S, wqk.shape[1]), BF16),
                   jax.ShapeDtypeStruct((B, nv, S), BF16),
                   jax.ShapeDtypeStruct((B, wot.shape[0], S), BF16),
                   jax.ShapeDtypeStruct((B, wqmt.shape[0], S), BF16),
                   jax.ShapeDtypeStruct((B, ML_GATE_ROWS, S), F32)],
        compiler_params=_params("arbitrary", "arbitrary"),
        name="inproj0",
    )(x, *consts)


def _split3(x):
    hi = x.astype(BF16)
    r = x - hi.astype(F32)
    mid = r.astype(BF16)
    lo = (r - mid.astype(F32)).astype(BF16)
    return hi, mid, lo


def _mlstm_kernel(qk_ref, vt_ref, o_ref, gate_ref, wconv_ref, bias_ref, hnorm_ref,
                  y_ref, ct_ref, m_ref, tail_ref):
    L = ML_CHUNK
    HQ = ML_HEADS * ML_QK_PAD

    @pl.when(pl.program_id(1) == 0)
    def _():
        ct_ref[...] = jnp.zeros_like(ct_ref)
        m_ref[...] = jnp.zeros_like(m_ref)
        tail_ref[...] = jnp.zeros_like(tail_ref)

    key_i = lax.broadcasted_iota(jnp.int32, (L, L), 0)
    qry_i = lax.broadcasted_iota(jnp.int32, (L, L), 1)
    causal_t = key_i <= qry_i
    triu = jnp.where(causal_t, 1.0, 0.0).astype(BF16)
    wc = wconv_ref[...]

    def prepare(nb):
        x = qk_ref[nb].astype(F32)
        tail = tail_ref[nb]
        row8 = lax.broadcasted_iota(jnp.int32, tail.shape, 0)
        conv = x * wc[ML_CONV - 1:ML_CONV]
        for s in range(1, ML_CONV):
            xs = pltpu.roll(x, s, axis=0)
            head = jnp.where(row8 < s, pltpu.roll(tail, s, axis=0), xs[0:8])
            xs = jnp.concatenate([head, xs[8:]], axis=0)
            conv = conv + xs * wc[ML_CONV - 1 - s:ML_CONV - s]
        tail_ref[nb] = x[L - 8:L]
        act = conv * jax.nn.sigmoid(conv)
        q = act[:, :HQ].astype(BF16)
        k = (act[:, HQ:] * (ML_QK_DIM ** -0.5)).astype(BF16)

        g_rows = gate_ref[nb] + bias_ref[...]
        log_f = jnp.minimum(g_rows, 0.0) - jnp.log1p(jnp.exp(-jnp.abs(g_rows)))
        b_rows = sum(jnp.dot(part, triu, preferred_element_type=F32) for part in _split3(log_f))
        c_rows = g_rows[:ML_HEADS] - b_rows[ML_HEADS:2 * ML_HEADS]
        c_cols = jnp.concatenate([c_rows, jnp.zeros((L - ML_HEADS, L), F32)], axis=0).T
        return q, k, b_rows, c_rows, c_cols

    def head(nb, h, q, k, b_rows, c_rows, c_cols):
        b_row = b_rows[ML_HEADS + h:ML_HEADS + h + 1, :]
        c_row = c_rows[h:h + 1, :]
        c_col = c_cols[:, h:h + 1]
        m_prev = m_ref[nb, h][:, 0:1]
        qh = q[:, h * ML_QK_PAD:(h + 1) * ML_QK_PAD]
        kh = k[:, h * ML_QK_PAD:(h + 1) * ML_QK_PAD]
        vth = vt_ref[nb, h * ML_VT_ROWS:(h + 1) * ML_VT_ROWS, :]
        ct_prev = ct_ref[nb, h]

        cm = jnp.where(causal_t, c_col, -jnp.inf)
        m_row = jnp.maximum(m_prev, jnp.max(cm, axis=0, keepdims=True))
        w_t = jnp.exp(cm - m_row)
        a_row = jnp.exp(m_prev - m_row)
        s_t = (lax.dot_general(kh, qh, _NT, preferred_element_type=F32) * w_t).astype(BF16)
        num_t = (jnp.dot(vth, s_t, preferred_element_type=F32)
                 + a_row * lax.dot_general(ct_prev.astype(BF16), qh, _NT, preferred_element_type=F32))
        den = num_t[ML_ONES_ROW:ML_ONES_ROW + 1]
        inv = 1.0 / jnp.maximum(jnp.abs(den), jnp.exp(-(b_row + m_row)))
        hc_t = num_t[:ML_V_DIM] * inv
        ms = jnp.sum(hc_t * hc_t, axis=0, keepdims=True) * (1.0 / ML_V_DIM)
        hn_t = hc_t * lax.rsqrt(ms + EPS) * hnorm_ref[h]
        og_t = o_ref[nb, h * ML_V_DIM:(h + 1) * ML_V_DIM, :].astype(F32)
        y_ref[nb, 0, h * ML_V_DIM:(h + 1) * ML_V_DIM, :] = (hn_t * jax.nn.sigmoid(og_t)).astype(BF16)

        m_end = m_row[:, L - 1:L]
        u_row = jnp.exp(c_row - m_end)
        decay = jnp.exp(m_prev - m_end)
        uv_t = (vth.astype(F32) * u_row).astype(BF16)
        ct_ref[nb, h] = decay * ct_prev + jnp.dot(uv_t, kh, preferred_element_type=F32)
        m_ref[nb, h] = jnp.broadcast_to(b_row[:, L - 1:L] + m_end, (1, LANES))

    prepared = [prepare(nb) for nb in range(ML_NB)]
    for h in range(ML_HEADS):
        for nb in range(ML_NB):
            head(nb, h, *prepared[nb])


def _mlstm(qk, vt, ot, gates, wconv, bias, hnorm):
    B, S, _ = qk.shape
    L = ML_CHUNK
    assert B % ML_NB == 0 and S % TM_PROJ == 0
    per_tile = TM_PROJ // L
    tok = lambda w: pl.BlockSpec((ML_NB, L, w), lambda b, c: (b, c, 0))
    tok_t = lambda r: pl.BlockSpec((ML_NB, r, L), lambda b, c: (b, 0, c))
    return pl.pallas_call(
        _mlstm_kernel,
        grid=(B // ML_NB, S // L),
        in_specs=[tok(qk.shape[2]), tok_t(vt.shape[1]), tok_t(ot.shape[1]), tok_t(gates.shape[1]),
                  _const_spec(wconv.shape), _const_spec(bias.shape), _const_spec(hnorm.shape)],
        out_specs=pl.BlockSpec((ML_NB, 1, ot.shape[1], L), lambda b, c: (b, c // per_tile, 0, c % per_tile)),
        out_shape=jax.ShapeDtypeStruct((B, S // TM_PROJ, ot.shape[1], TM_PROJ), BF16),
        scratch_shapes=[pltpu.VMEM((ML_NB, ML_HEADS, ML_VT_ROWS, ML_QK_PAD), F32),
                        pltpu.VMEM((ML_NB, ML_HEADS, 1, LANES), F32),
                        pltpu.VMEM((ML_NB, 8, qk.shape[2]), F32)],
        compiler_params=_params("arbitrary", "arbitrary"),
        name="mlstm",
    )(qk, vt, ot, gates, wconv, bias, hnorm)


def _outproj_kernel(x_ref, yt_ref, qmt_ref, km_ref, vmt_ref, w_ref, out_ref):
    tm = x_ref.shape[1]
    halves = [slice(r0, r0 + tm // 2) for r0 in range(0, tm, tm // 2)]
    scores = [[jnp.dot(km_ref[0, h], qmt_ref[0, :, cols], preferred_element_type=F32)
               for h in range(MEM_HEADS)] for cols in halves]
    for cols, s_heads in zip(halves, scores):
        ymem_t = None
        for h in range(MEM_HEADS):
            s = s_heads[h]
            e = jnp.exp(s - jnp.max(s, axis=0, keepdims=True))
            p = (e * (1.0 / jnp.sum(e, axis=0, keepdims=True))).astype(BF16)
            oh = jnp.dot(vmt_ref[0, h], p, preferred_element_type=F32)
            ymem_t = oh if ymem_t is None else ymem_t + oh
        ycat = jnp.concatenate([yt_ref[0, 0, :, cols], ymem_t.astype(BF16)], axis=0).T
        out_ref[0, cols, :] = x_ref[0, cols, :] + jnp.dot(ycat, w_ref[...], preferred_element_type=F32)


def _outproj(x, yt, qmt, km, vmt, w):
    B, S, D = x.shape
    tm = yt.shape[3]
    tok = lambda w: pl.BlockSpec((1, tm, w), lambda b, i: (b, i, 0))
    tok_t = lambda r: pl.BlockSpec((1, r, tm), lambda b, i: (b, 0, i))
    per_b = pl.BlockSpec((1,) + km.shape[1:], lambda b, i: (b, 0, 0, 0))
    return pl.pallas_call(
        _outproj_kernel,
        grid=(B, S // tm),
        in_specs=[tok(D), pl.BlockSpec((1, 1, yt.shape[2], tm), lambda b, i: (b, i, 0, 0)),
                  tok_t(qmt.shape[1]), per_b, per_b,
                  _const_spec(w.shape)],
        out_specs=tok(D),
        out_shape=jax.ShapeDtypeStruct(x.shape, F32),
        compiler_params=_params("arbitrary", "arbitrary"),
        name="outproj",
    )(x, yt, qmt, km, vmt, w)


def _ffn_kernel(x_ref, g_ref, w1_ref, w2_ref, gf_ref, out_ref, *, final_norm):
    x = x_ref[0]
    hn = _rms(x, g_ref[...]).astype(BF16)
    acc = x
    for c in range(D_FF // FF_CHUNK):
        w1c = w1_ref[:, c * FF_CHUNK:(c + 1) * FF_CHUNK].astype(BF16)
        a = jnp.dot(hn, w1c, preferred_element_type=F32)
        a = jnp.square(jnp.maximum(a, 0.0)).astype(BF16)
        w2c = w2_ref[c * FF_CHUNK:(c + 1) * FF_CHUNK, :].astype(BF16)
        acc = acc + jnp.dot(a, w2c, preferred_element_type=F32)
    if final_norm:
        acc = _rms(acc, gf_ref[...])
    out_ref[0] = acc


def _ffn(x, g, w1, w2, gf, final_norm):
    B, S, D = x.shape
    tm = min(TM_PROJ, S)
    tok = pl.BlockSpec((1, tm, D), lambda b, i: (b, i, 0))
    return pl.pallas_call(
        functools.partial(_ffn_kernel, final_norm=final_norm),
        grid=(B, S // tm),
        in_specs=[tok, _const_spec(g.shape), _single_buffered_spec(w1.shape), _single_buffered_spec(w2.shape),
                  _const_spec(gf.shape)],
        out_specs=tok,
        out_shape=jax.ShapeDtypeStruct(x.shape, F32),
        compiler_params=_params("arbitrary", "arbitrary"),
        name="ffn",
    )(x, g, w1, w2, gf)


def _rope_kernel(pos_ref, inv_ref, cos_ref, sin_ref):
    ang = pos_ref[0].astype(F32) * inv_ref[...]
    cos_ref[0] = jnp.cos(ang)
    sin_ref[0] = jnp.sin(ang)


def _rope_tables(positions):
    B, S = positions.shape
    nf = MLA_ROPE // 2
    inv = (ROPE_THETA ** (-jnp.arange(0, MLA_ROPE, 2, dtype=F32) / MLA_ROPE)).reshape(nf, 1)
    out = jax.ShapeDtypeStruct((B, nf, S), F32)
    spec = pl.BlockSpec((1, nf, S), lambda b: (b, 0, 0))
    return pl.pallas_call(
        _rope_kernel,
        grid=(B,),
        in_specs=[pl.BlockSpec((1, 1, S), lambda b: (b, 0, 0)), _const_spec(inv.shape)],
        out_specs=[spec, spec],
        out_shape=[out, out],
        compiler_params=_params("arbitrary"),
        name="rope_tables",
    )(positions.reshape(B, 1, S), inv)


def _inproj1_kernel(x_ref, g_ref, win_ref, wqmt_ref, wkrt_ref, gq_ref, gkv_ref, wuqt_ref, wuk_ref, wuvt_ref,
                    cos_ref, sin_ref, qt_ref, k_ref, vt_ref, qm_ref):
    half = MLA_ROPE // 2
    xn = _rms(x_ref[0], g_ref[...]).astype(BF16)
    c = jnp.dot(xn, win_ref[...], preferred_element_type=F32)
    o_kv = MLA_Q_RANK
    qm_ref[0] = lax.dot_general(wqmt_ref[...], xn, _NT, preferred_element_type=F32).astype(BF16)
    cq = _rms(c[:, :o_kv], gq_ref[...]).astype(BF16)
    ckv = _rms(c[:, o_kv:], gkv_ref[...]).astype(BF16)
    cos_t, sin_t = cos_ref[0], sin_ref[0]

    def rope_t(x1, x2):
        return x1 * cos_t - x2 * sin_t, x2 * cos_t + x1 * sin_t

    qscale = ((MLA_NOPE + MLA_ROPE) ** -0.5) * math.log2(math.e)
    qt = lax.dot_general(wuqt_ref[...], cq, _NT, preferred_element_type=F32)
    zpad = jnp.zeros((MLA_QK_PAD - MLA_NOPE - MLA_ROPE, qt.shape[1]), F32)
    for h in range(MLA_HEADS):
        b0 = h * MLA_QK_PAD
        r1, r2 = rope_t(qt[b0 + MLA_NOPE:b0 + MLA_NOPE + half],
                        qt[b0 + MLA_NOPE + half:b0 + MLA_NOPE + MLA_ROPE])
        blk = jnp.concatenate([qt[b0:b0 + MLA_NOPE], r1, r2, zpad], axis=0) * qscale
        qt_ref[0, 0, b0:b0 + MLA_QK_PAD] = blk.astype(BF16)

    krt = lax.dot_general(wkrt_ref[...], xn, _NT, preferred_element_type=F32)
    r1, r2 = rope_t(krt[MLA_NOPE:MLA_NOPE + half], krt[MLA_NOPE + half:MLA_NOPE + MLA_ROPE])
    kr = jnp.concatenate([krt[:MLA_NOPE], r1, r2, krt[MLA_NOPE + MLA_ROPE:]], axis=0).T
    k = jnp.dot(ckv, wuk_ref[...], preferred_element_type=F32)
    for h in range(MLA_HEADS):
        sl = slice(h * MLA_QK_PAD, (h + 1) * MLA_QK_PAD)
        k_ref[0, :, sl] = (k[:, sl] + kr).astype(BF16)

    vt = lax.dot_general(wuvt_ref[...], ckv, _NT, preferred_element_type=F32)
    rowid = lax.broadcasted_iota(jnp.int32, vt.shape, 0) % MLA_VT_ROWS
    vt_ref[0, 0] = jnp.where(rowid == MLA_V, 1.0, vt).astype(BF16)


def _inproj1(x, g, win, wqmt, wkrt, gq, gkv, wuqt, wuk, wuvt, cos_t, sin_t):
    B, S, D = x.shape
    tm = TQ
    assert S % tm == 0
    tok = lambda w: pl.BlockSpec((1, tm, w), lambda b, i: (b, i, 0))
    tok_t = lambda r: pl.BlockSpec((1, r, tm), lambda b, i: (b, 0, i))
    consts = (g, win, wqmt, wkrt, gq, gkv, wuqt, wuk, wuvt)
    nvt = wuvt.shape[0]
    return pl.pallas_call(
        _inproj1_kernel,
        grid=(B, S // tm),
        in_specs=[tok(D)] + [_const_spec(a.shape) for a in consts] + [tok_t(cos_t.shape[1])] * 2,
        out_specs=[pl.BlockSpec((1, 1, wuqt.shape[0], tm), lambda b, i: (b, i, 0, 0)), tok(wuk.shape[1]),
                   pl.BlockSpec((1, 1, nvt, tm), lambda b, i: (b, i, 0, 0)), tok_t(MEM_WIDTH)],
        out_shape=[jax.ShapeDtypeStruct((B, S // tm, wuqt.shape[0], tm), BF16),
                   jax.ShapeDtypeStruct((B, S, wuk.shape[1]), BF16),
                   jax.ShapeDtypeStruct((B, S // tm, nvt, tm), BF16),
                   jax.ShapeDtypeStruct((B, MEM_WIDTH, S), BF16)],
        compiler_params=_params("arbitrary", "arbitrary"),
        name="inproj1",
    )(x, *consts, cos_t, sin_t)


def _flash_kernel(qt_ref, k_ref, vt_ref, y_ref, sa_ref, sb_ref, smax_ref, acc_ref, m_ref):
    slots = (sa_ref, sb_ref)
    key_i = lax.broadcasted_iota(jnp.int32, (TQ, TQ), 0)
    qry_i = lax.broadcasted_iota(jnp.int32, (TQ, TQ), 1)

    def col_max(s):
        parts = [jnp.max(s[r:r + 128], axis=0, keepdims=True) for r in range(0, TQ, 128)]
        return jnp.maximum(jnp.maximum(parts[0], parts[1]), jnp.maximum(parts[2], parts[3]))

    def v_t(hh, blk):
        return vt_ref[0, blk, hh * MLA_VT_ROWS:(hh + 1) * MLA_VT_ROWS, :]

    def tile_pair(j, carry):
        _flash_tile_pair(j, qt_ref, k_ref, y_ref, slots, smax_ref, acc_ref, m_ref, key_i, qry_i, col_max, v_t)
        return carry

    lax.fori_loop(0, qt_ref.shape[1] // 2, tile_pair, 0)


def _flash_tile_pair(j, qt_ref, k_ref, y_ref, slots, smax_ref, acc_ref, m_ref, key_i, qry_i, col_max, v_t):
    ia = 2 * j
    ib = 2 * j + 1
    tiles = (ia, ib)

    def scores_t(qt, hh, blk):
        kh = k_ref[0, pl.ds(pl.multiple_of(blk * TQ, TQ), TQ), hh * MLA_QK_PAD:(hh + 1) * MLA_QK_PAD]
        qh = qt_ref[0, tiles[qt], hh * MLA_QK_PAD:(hh + 1) * MLA_QK_PAD, :]
        return jnp.dot(kh, qh, preferred_element_type=F32)

    def qk_stage(qt, slot, blk, diagonal=False):
        for hh in range(2):
            s = scores_t(qt, hh, blk)
            if diagonal:
                s = jnp.where(key_i <= qry_i, s, -jnp.inf)
            slots[slot][hh] = s
            smax_ref[slot, hh] = col_max(s)

    def pv_stage(qt, slot, blk):
        for hh in range(2):
            m_prev = m_ref[qt, hh]
            m_new = jnp.maximum(m_prev, smax_ref[slot, hh])
            alpha = jnp.exp2(m_prev - m_new)
            p = jnp.exp2(slots[slot][hh] - m_new).astype(BF16)
            acc_ref[qt, hh] = alpha * acc_ref[qt, hh] + jnp.dot(v_t(hh, blk), p, preferred_element_type=F32)
            m_ref[qt, hh] = m_new

    def finalize(qt):
        outs = []
        for hh in range(2):
            acc = acc_ref[qt, hh]
            outs.append(acc[:MLA_V] * (1.0 / acc[MLA_V:MLA_V + 1]))
        y_ref[0, tiles[qt]] = jnp.concatenate(outs, axis=0).astype(BF16)

    m_ref[...] = jnp.full(m_ref.shape, -jnp.inf, F32)
    acc_ref[...] = jnp.zeros_like(acc_ref)

    @pl.when(j == 0)
    def _():
        qk_stage(0, 0, 0, diagonal=True)

    @pl.when(j >= 1)
    def _():
        qk_stage(0, 0, 0)

    def body_a(t, carry):
        qk_stage(0, 1, 2 * t + 1)
        pv_stage(0, 0, 2 * t)
        qk_stage(0, 0, 2 * t + 2)
        pv_stage(0, 1, 2 * t + 1)
        return carry

    lax.fori_loop(0, j - 1, body_a, 0)

    @pl.when(j >= 1)
    def _():
        qk_stage(0, 1, ia - 1)
        pv_stage(0, 0, ia - 2)
        qk_stage(0, 0, ia, diagonal=True)
        pv_stage(0, 1, ia - 1)

    qk_stage(1, 1, 0)
    pv_stage(0, 0, ia)
    finalize(0)

    def body_b(t, carry):
        qk_stage(1, 0, 2 * t + 1)
        pv_stage(1, 1, 2 * t)
        qk_stage(1, 1, 2 * t + 2)
        pv_stage(1, 0, 2 * t + 1)
        return carry

    lax.fori_loop(0, j, body_b, 0)

    qk_stage(1, 0, ib, diagonal=True)
    pv_stage(1, 1, ib - 1)
    pv_stage(1, 0, ib)
    finalize(1)


def _flash(qt, k, vt):
    B, S, _ = k.shape
    assert S % (2 * TQ) == 0
    nq = S // TQ
    pairs = MLA_HEADS // 2
    return pl.pallas_call(
        _flash_kernel,
        grid=(B, pairs),
        in_specs=[pl.BlockSpec((1, nq, 2 * MLA_QK_PAD, TQ), lambda b, p: (b, 0, p, 0)),
                  pl.BlockSpec((1, S, 2 * MLA_QK_PAD), lambda b, p: (b, 0, p)),
                  pl.BlockSpec((1, nq, 2 * MLA_VT_ROWS, TQ), lambda b, p: (b, 0, p, 0))],
        out_specs=pl.BlockSpec((1, nq, 2 * MLA_V, TQ), lambda b, p: (b, 0, p, 0)),
        out_shape=jax.ShapeDtypeStruct((B, nq, MLA_HEADS * MLA_V, TQ), BF16),
        scratch_shapes=[pltpu.VMEM((2, TQ, TQ), F32),
                        pltpu.VMEM((2, TQ, TQ), F32),
                        pltpu.VMEM((2, 2, 1, TQ), F32),
                        pltpu.VMEM((2, 2, MLA_VT_ROWS, TQ), F32),
                        pltpu.VMEM((2, 2, 1, TQ), F32)],
        compiler_params=_params("arbitrary", "arbitrary"),
        name="mla_flash",
    )(qt, k, vt)


def _pad_heads(w, heads, d, dp):
    lead = w.shape[:-1]
    w = w.reshape(lead + (heads, d))
    w = jnp.pad(w, [(0, 0)] * len(lead) + [(0, 0), (0, dp - d)])
    return w.reshape(lead + (heads * dp,))


def kernel(x, mem, positions, mem_norm, w_mem_kv, norm_mix0, w_in0, b_igate0, b_fgate0, w_conv0, w_hnorm0, w_out0, norm_ffn0, w_ff1_0, w_ff2_0, norm_mix1, w_in1, w_qnorm1, w_uq1, w_kvnorm1, w_ukv1, w_out1, norm_ffn1, w_ff1_1, w_ff2_1, final_norm):
    row = lambda g: g.reshape(1, -1).astype(F32)

    km, vmt = _mem_kv(mem, row(mem_norm), w_mem_kv[:, :MEM_WIDTH].astype(BF16),
                      w_mem_kv[:, MEM_WIDTH:].T.astype(BF16))

    nq = ML_HEADS * ML_QK_DIM
    o_v, o_o, o_g = 2 * nq, 2 * nq + MIX_WIDTH, 2 * nq + 2 * MIX_WIDTH
    o_qm = o_g + 2 * ML_HEADS
    wqk = jnp.concatenate([_pad_heads(w_in0[:, :nq], ML_HEADS, ML_QK_DIM, ML_QK_PAD),
                           _pad_heads(w_in0[:, nq:o_v], ML_HEADS, ML_QK_DIM, ML_QK_PAD)], axis=1)
    wvgt0 = jnp.concatenate([_pad_heads(w_in0[:, o_v:o_o], ML_HEADS, ML_V_DIM, ML_VT_ROWS),
                             jnp.pad(w_in0[:, o_g:o_qm], ((0, 0), (0, ML_GATE_ROWS - 2 * ML_HEADS)))], axis=1).T
    wot0 = w_in0[:, o_o:o_g].T
    wqmt0 = w_in0[:, o_qm:].T
    wconv = jnp.concatenate([_pad_heads(w_conv0[:, :nq], ML_HEADS, ML_QK_DIM, ML_QK_PAD),
                             _pad_heads(w_conv0[:, nq:], ML_HEADS, ML_QK_DIM, ML_QK_PAD)], axis=1).astype(F32)
    gate_bias = jnp.pad(jnp.concatenate([b_igate0, b_fgate0]), (0, ML_GATE_ROWS - 2 * ML_HEADS)).astype(F32)
    gate_bias = jnp.broadcast_to(gate_bias[:, None], (ML_GATE_ROWS, LANES))
    hnorm = jnp.broadcast_to(w_hnorm0.astype(F32)[:, :, None], (ML_HEADS, ML_V_DIM, LANES))

    qk, vt0, ot0, qm0, gates = _inproj0(x, row(norm_mix0), wqk.astype(BF16), wvgt0.astype(BF16),
                                        wot0.astype(BF16), wqmt0.astype(BF16))
    yt0 = _mlstm(qk, vt0, ot0, gates, wconv, gate_bias, hnorm)
    x = _outproj(x, yt0, qm0, km, vmt, w_out0.astype(BF16))
    x = _ffn(x, row(norm_ffn0), w_ff1_0, w_ff2_0, row(final_norm), False)

    o_kr = MLA_Q_RANK + MLA_KV_RANK
    o_qm1 = o_kr + MLA_ROPE
    win1 = w_in1[:, :o_kr]
    wqmt1 = w_in1[:, o_qm1:].T
    wkrt = jnp.pad(w_in1[:, o_kr:o_qm1].T, ((MLA_NOPE, LANES - MLA_NOPE - MLA_ROPE), (0, 0)))
    wuqt = _pad_heads(w_uq1, MLA_HEADS, MLA_NOPE + MLA_ROPE, MLA_QK_PAD).T
    ukv = w_ukv1.reshape(MLA_KV_RANK, MLA_HEADS, MLA_NOPE + MLA_V)
    wuk = _pad_heads(ukv[:, :, :MLA_NOPE].reshape(MLA_KV_RANK, -1), MLA_HEADS, MLA_NOPE, MLA_QK_PAD)
    wuvt = _pad_heads(ukv[:, :, MLA_NOPE:].reshape(MLA_KV_RANK, -1), MLA_HEADS, MLA_V, MLA_VT_ROWS).T

    cos_t, sin_t = _rope_tables(positions)
    qt1, k1, vt1, qm1 = _inproj1(x, row(norm_mix1), win1.astype(BF16), wqmt1.astype(BF16), wkrt.astype(BF16),
                                 row(w_qnorm1), row(w_kvnorm1), wuqt.astype(BF16), wuk.astype(BF16),
                                 wuvt.astype(BF16), cos_t, sin_t)
    yt1 = _flash(qt1, k1, vt1)
    x = _outproj(x, yt1, qm1, km, vmt, w_out1.astype(BF16))
    return _ffn(x, row(norm_ffn1), w_ff1_1, w_ff2_1, row(final_norm), True)
```

```python
import functools
import math

import jax
import jax.numpy as jnp
from jax import lax
from jax.experimental import pallas as pl
from jax.experimental.pallas import tpu as pltpu

F32 = jnp.float32
BF16 = jnp.bfloat16
EPS = 1e-6

D_MODEL = 1024
N_MEM = 256
MEM_HEADS = 4
MEM_HEAD_DIM = 64
MEM_WIDTH = MEM_HEADS * MEM_HEAD_DIM
MIX_WIDTH = D_MODEL - MEM_WIDTH

ML_HEADS = 4
ML_V_DIM = MIX_WIDTH // ML_HEADS
ML_QK_DIM = ML_V_DIM // 2
ML_CONV = 4
ML_CHUNK = 128
ML_QK_PAD = 128
ML_VT_ROWS = 256
ML_ONES_ROW = ML_V_DIM
ML_NB = 4
ML_GATE_ROWS = 16

MLA_HEADS = 12
MLA_NOPE = 64
MLA_ROPE = 32
MLA_V = 64
MLA_Q_RANK = 384
MLA_KV_RANK = 256
MLA_QK_PAD = 128
MLA_VT_ROWS = 80
ROPE_THETA = 10000.0
D_FF = 4 * D_MODEL

LANES = 128
VMEM_LIMIT = 56 * 1024 * 1024

TM_PROJ = 512
FF_CHUNK = 1024
TQ = 512

_NT = (((1,), (1,)), ((), ()))


def _params(*sem):
    return pltpu.CompilerParams(dimension_semantics=sem, vmem_limit_bytes=VMEM_LIMIT)


def _rms(x, g):
    return x * lax.rsqrt(jnp.mean(x * x, axis=-1, keepdims=True) + EPS) * g


def _const_spec(shape):
    nd = len(shape)
    return pl.BlockSpec(shape, lambda *_: (0,) * nd)


def _single_buffered_spec(shape):
    nd = len(shape)
    return pl.BlockSpec(shape, lambda *_: (0,) * nd, pipeline_mode=pl.Buffered(1))


def _mem_kv_kernel(mem_ref, g_ref, wk_ref, wvt_ref, k_ref, vt_ref):
    xn = _rms(mem_ref[0], g_ref[...]).astype(BF16)
    k = jnp.dot(xn, wk_ref[...], preferred_element_type=F32)
    vt = lax.dot_general(wvt_ref[...], xn, _NT, preferred_element_type=F32)
    k = k * (MEM_HEAD_DIM ** -0.5)
    col_head = lax.broadcasted_iota(jnp.int32, k.shape, 1) // MEM_HEAD_DIM
    row_head = lax.broadcasted_iota(jnp.int32, vt.shape, 0) // MEM_HEAD_DIM
    for h in range(MEM_HEADS):
        k_ref[0, h] = jnp.where(col_head == h, k, 0.0).astype(BF16)
        vt_ref[0, h] = jnp.where(row_head == h, vt, 0.0).astype(BF16)


def _mem_kv(mem, g, wk, wvt):
    B = mem.shape[0]
    out = jax.ShapeDtypeStruct((B, MEM_HEADS, MEM_WIDTH, N_MEM), BF16)
    return pl.pallas_call(
        _mem_kv_kernel,
        grid=(B,),
        in_specs=[pl.BlockSpec((1, N_MEM, D_MODEL), lambda b: (b, 0, 0)),
                  _const_spec(g.shape), _const_spec(wk.shape), _const_spec(wvt.shape)],
        out_specs=[pl.BlockSpec((1, MEM_HEADS, N_MEM, MEM_WIDTH), lambda b: (b, 0, 0, 0)),
                   pl.BlockSpec((1, MEM_HEADS, MEM_WIDTH, N_MEM), lambda b: (b, 0, 0, 0))],
        out_shape=[out, out],
        compiler_params=_params("arbitrary"),
        name="mem_kv",
    )(mem, g, wk, wvt)


def _inproj0_kernel(x_ref, g_ref, wqk_ref, wv_ref, wo_ref, wqm_ref,
                    qk_ref, v_ref, o_ref, qm_ref, gate_ref):
    xn = _rms(x_ref[0], g_ref[...]).astype(BF16)
    qk_ref[0] = jnp.dot(xn, wqk_ref[...], preferred_element_type=F32).astype(BF16)
    vg = lax.dot_general(wv_ref[...], xn, _NT, preferred_element_type=F32)
    nv = v_ref.shape[1]
    vt = vg[:nv]
    rowid = lax.broadcasted_iota(jnp.int32, vt.shape, 0) % ML_VT_ROWS
    v_ref[0] = jnp.where(rowid == ML_ONES_ROW, 1.0, vt).astype(BF16)
    gate_ref[0] = vg[nv:]
    o_ref[0] = lax.dot_general(wo_ref[...], xn, _NT, preferred_element_type=F32).astype(BF16)
    qm_ref[0] = lax.dot_general(wqm_ref[...], xn, _NT, preferred_element_type=F32).astype(BF16)


def _inproj0(x, g, wqk, wvgt, wot, wqmt):
    B, S, D = x.shape
    tm = min(TM_PROJ, S)
    tok = lambda w: pl.BlockSpec((1, tm, w), lambda b, i: (b, i, 0))
    tok_t = lambda r: pl.BlockSpec((1, r, tm), lambda b, i: (b, 0, i))
    consts = (g, wqk, wvgt, wot, wqmt)
    nv = wvgt.shape[0] - ML_GATE_ROWS
    return pl.pallas_call(
        _inproj0_kernel,
        grid=(B, S // tm),
        in_specs=[tok(D)] + [_const_spec(a.shape) for a in consts],
        out_specs=[tok(wqk.shape[1]), tok_t(nv), tok_t(wot.shape[0]), tok_t(wqmt.shape[0]),
                   tok_t(ML_GATE_ROWS)],
        out_shape=[jax.ShapeDtypeStruct((B, S, wqk.shape[1]), BF16),
                   jax.ShapeDtypeStruct((B, nv, S), BF16),
                   jax.ShapeDtypeStruct((B, wot.shape[0], S), BF16),
                   jax.ShapeDtypeStruct((B, wqmt.shape[0], S), BF16),
                   jax.ShapeDtypeStruct((B, ML_GATE_ROWS, S), F32)],
        compiler_params=_params("arbitrary", "arbitrary"),
        name="inproj0",
    )(x, *consts)


def _split3(x):
    hi = x.astype(BF16)
    r = x - hi.astype(F32)
    mid = r.astype(BF16)
    lo = (r - mid.astype(F32)).astype(BF16)
    return hi, mid, lo


def _mlstm_kernel(qk_ref, vt_ref, o_ref, gate_ref, wconv_ref, bias_ref, hnorm_ref,
                  y_ref, ct_ref, m_ref, tail_ref):
    L = ML_CHUNK
    HQ = ML_HEADS * ML_QK_PAD

    @pl.when(pl.program_id(1) == 0)
    def _():
        ct_ref[...] = jnp.zeros_like(ct_ref)
        m_ref[...] = jnp.zeros_like(m_ref)
        tail_ref[...] = jnp.zeros_like(tail_ref)

    key_i = lax.broadcasted_iota(jnp.int32, (L, L), 0)
    qry_i = lax.broadcasted_iota(jnp.int32, (L, L), 1)
    causal_t = key_i <= qry_i
    triu = jnp.where(causal_t, 1.0, 0.0).astype(BF16)
    wc = wconv_ref[...]

    def prepare(nb):
        x = qk_ref[nb].astype(F32)
        tail = tail_ref[nb]
        row8 = lax.broadcasted_iota(jnp.int32, tail.shape, 0)
        conv = x * wc[ML_CONV - 1:ML_CONV]
        for s in range(1, ML_CONV):
            xs = pltpu.roll(x, s, axis=0)
            head = jnp.where(row8 < s, pltpu.roll(tail, s, axis=0), xs[0:8])
            xs = jnp.concatenate([head, xs[8:]], axis=0)
            conv = conv + xs * wc[ML_CONV - 1 - s:ML_CONV - s]
        tail_ref[nb] = x[L - 8:L]
        act = conv * jax.nn.sigmoid(conv)
        q = act[:, :HQ].astype(BF16)
        k = (act[:, HQ:] * (ML_QK_DIM ** -0.5)).astype(BF16)

        g_rows = gate_ref[nb] + bias_ref[...]
        log_f = jnp.minimum(g_rows, 0.0) - jnp.log1p(jnp.exp(-jnp.abs(g_rows)))
        b_rows = sum(jnp.dot(part, triu, preferred_element_type=F32) for part in _split3(log_f))
        c_rows = g_rows[:ML_HEADS] - b_rows[ML_HEADS:2 * ML_HEADS]
        c_cols = jnp.concatenate([c_rows, jnp.zeros((L - ML_HEADS, L), F32)], axis=0).T
        return q, k, b_rows, c_rows, c_cols

    def head(nb, h, q, k, b_rows, c_rows, c_cols):
        b_row = b_rows[ML_HEADS + h:ML_HEADS + h + 1, :]
        c_row = c_rows[h:h + 1, :]
        c_col = c_cols[:, h:h + 1]
        m_prev = m_ref[nb, h][:, 0:1]
        qh = q[:, h * ML_QK_PAD:(h + 1) * ML_QK_PAD]
        kh = k[:, h * ML_QK_PAD:(h + 1) * ML_QK_PAD]
        vth = vt_ref[nb, h * ML_VT_ROWS:(h + 1) * ML_VT_ROWS, :]
        ct_prev = ct_ref[nb, h]

        cm = jnp.where(causal_t, c_col, -jnp.inf)
        m_row = jnp.maximum(m_prev, jnp.max(cm, axis=0, keepdims=True))
        w_t = jnp.exp(cm - m_row)
        a_row = jnp.exp(m_prev - m_row)
        s_t = (lax.dot_general(kh, qh, _NT, preferred_element_type=F32) * w_t).astype(BF16)
        num_t = (jnp.dot(vth, s_t, preferred_element_type=F32)
                 + a_row * lax.dot_general(ct_prev.astype(BF16), qh, _NT, preferred_element_type=F32))
        den = num_t[ML_ONES_ROW:ML_ONES_ROW + 1]
        inv = 1.0 / jnp.maximum(jnp.abs(den), jnp.exp(-(b_row + m_row)))
        hc_t = num_t[:ML_V_DIM] * inv
        ms = jnp.sum(hc_t * hc_t, axis=0, keepdims=True) * (1.0 / ML_V_DIM)
        hn_t = hc_t * lax.rsqrt(ms + EPS) * hnorm_ref[h]
        og_t = o_ref[nb, h * ML_V_DIM:(h + 1) * ML_V_DIM, :].astype(F32)
        y_ref[nb, 0, h * ML_V_DIM:(h + 1) * ML_V_DIM, :] = (hn_t * jax.nn.sigmoid(og_t)).astype(BF16)

        m_end = m_row[:, L - 1:L]
        u_row = jnp.exp(c_row - m_end)
        decay = jnp.exp(m_prev - m_end)
        uv_t = (vth.astype(F32) * u_row).astype(BF16)
        ct_ref[nb, h] = decay * ct_prev + jnp.dot(uv_t, kh, preferred_element_type=F32)
        m_ref[nb, h] = jnp.broadcast_to(b_row[:, L - 1:L] + m_end, (1, LANES))

    prepared = [prepare(nb) for nb in range(ML_NB)]
    for h in range(ML_HEADS):
        for nb in range(ML_NB):
            head(nb, h, *prepared[nb])


def _mlstm(qk, vt, ot, gates, wconv, bias, hnorm):
    B, S, _ = qk.shape
    L = ML_CHUNK
    assert B % ML_NB == 0 and S % TM_PROJ == 0
    per_tile = TM_PROJ // L
    tok = lambda w: pl.BlockSpec((ML_NB, L, w), lambda b, c: (b, c, 0))
    tok_t = lambda r: pl.BlockSpec((ML_NB, r, L), lambda b, c: (b, 0, c))
    return pl.pallas_call(
        _mlstm_kernel,
        grid=(B // ML_NB, S // L),
        in_specs=[tok(qk.shape[2]), tok_t(vt.shape[1]), tok_t(ot.shape[1]), tok_t(gates.shape[1]),
                  _const_spec(wconv.shape), _const_spec(bias.shape), _const_spec(hnorm.shape)],
        out_specs=pl.BlockSpec((ML_NB, 1, ot.shape[1], L), lambda b, c: (b, c // per_tile, 0, c % per_tile)),
        out_shape=jax.ShapeDtypeStruct((B, S // TM_PROJ, ot.shape[1], TM_PROJ), BF16),
        scratch_shapes=[pltpu.VMEM((ML_NB, ML_HEADS, ML_VT_ROWS, ML_QK_PAD), F32),
                        pltpu.VMEM((ML_NB, ML_HEADS, 1, LANES), F32),
                        pltpu.VMEM((ML_NB, 8, qk.shape[2]), F32)],
        compiler_params=_params("arbitrary", "arbitrary"),
        name="mlstm",
    )(qk, vt, ot, gates, wconv, bias, hnorm)


def _block_tail_kernel(x_ref, yt_ref, qmt_ref, km_ref, vmt_ref, wout_ref, g_ref, w1_ref, w2_ref, gf_ref,
                       out_ref, x1_ref, *, final_norm):
    tm = x_ref.shape[1]
    halves = [slice(r0, r0 + tm // 2) for r0 in range(0, tm, tm // 2)]
    scores = [[jnp.dot(km_ref[0, h], qmt_ref[0, :, cols], preferred_element_type=F32)
               for h in range(MEM_HEADS)] for cols in halves]
    for cols, s_heads in zip(halves, scores):
        ymem_t = None
        for h in range(MEM_HEADS):
            s = s_heads[h]
            e = jnp.exp(s - jnp.max(s, axis=0, keepdims=True))
            p = (e * (1.0 / jnp.sum(e, axis=0, keepdims=True))).astype(BF16)
            oh = jnp.dot(vmt_ref[0, h], p, preferred_element_type=F32)
            ymem_t = oh if ymem_t is None else ymem_t + oh
        ycat = jnp.concatenate([yt_ref[0, 0, :, cols], ymem_t.astype(BF16)], axis=0).T
        x1_ref[cols, :] = x_ref[0, cols, :] + jnp.dot(ycat, wout_ref[...], preferred_element_type=F32)

    x = x1_ref[...]
    hn = _rms(x, g_ref[...]).astype(BF16)
    acc = x
    for c in range(D_FF // FF_CHUNK):
        w1c = w1_ref[:, c * FF_CHUNK:(c + 1) * FF_CHUNK].astype(BF16)
        a = jnp.dot(hn, w1c, preferred_element_type=F32)
        a = jnp.square(jnp.maximum(a, 0.0)).astype(BF16)
        w2c = w2_ref[c * FF_CHUNK:(c + 1) * FF_CHUNK, :].astype(BF16)
        acc = acc + jnp.dot(a, w2c, preferred_element_type=F32)
    if final_norm:
        acc = _rms(acc, gf_ref[...])
    out_ref[0] = acc


def _block_tail(x, yt, qmt, km, vmt, wout, g, w1, w2, gf, final_norm):
    B, S, D = x.shape
    tm = yt.shape[3]
    tok = lambda w: pl.BlockSpec((1, tm, w), lambda b, i: (b, i, 0))
    tok_t = lambda r: pl.BlockSpec((1, r, tm), lambda b, i: (b, 0, i))
    per_b = pl.BlockSpec((1,) + km.shape[1:], lambda b, i: (b, 0, 0, 0))
    return pl.pallas_call(
        functools.partial(_block_tail_kernel, final_norm=final_norm),
        grid=(B, S // tm),
        in_specs=[tok(D), pl.BlockSpec((1, 1, yt.shape[2], tm), lambda b, i: (b, i, 0, 0)),
                  tok_t(qmt.shape[1]), per_b, per_b, _single_buffered_spec(wout.shape),
                  _const_spec(g.shape), _single_buffered_spec(w1.shape), _single_buffered_spec(w2.shape),
                  _const_spec(gf.shape)],
        out_specs=tok(D),
        out_shape=jax.ShapeDtypeStruct(x.shape, F32),
        scratch_shapes=[pltpu.VMEM((tm, D), F32)],
        compiler_params=_params("arbitrary", "arbitrary"),
        name="block_tail",
    )(x, yt, qmt, km, vmt, wout, g, w1, w2, gf)


def _rope_kernel(pos_ref, inv_ref, cos_ref, sin_ref):
    ang = pos_ref[0].astype(F32) * inv_ref[...]
    cos_ref[0] = jnp.cos(ang)
    sin_ref[0] = jnp.sin(ang)


def _rope_tables(positions):
    B, S = positions.shape
    nf = MLA_ROPE // 2
    inv = (ROPE_THETA ** (-jnp.arange(0, MLA_ROPE, 2, dtype=F32) / MLA_ROPE)).reshape(nf, 1)
    out = jax.ShapeDtypeStruct((B, nf, S), F32)
    spec = pl.BlockSpec((1, nf, S), lambda b: (b, 0, 0))
    return pl.pallas_call(
        _rope_kernel,
        grid=(B,),
        in_specs=[pl.BlockSpec((1, 1, S), lambda b: (b, 0, 0)), _const_spec(inv.shape)],
        out_specs=[spec, spec],
        out_shape=[out, out],
        compiler_params=_params("arbitrary"),
        name="rope_tables",
    )(positions.reshape(B, 1, S), inv)


def _inproj1_kernel(x_ref, g_ref, win_ref, wqmt_ref, wkrt_ref, gq_ref, gkv_ref, wuqt_ref, wuk_ref, wuvt_ref,
                    cos_ref, sin_ref, qt_ref, k_ref, vt_ref, qm_ref):
    half = MLA_ROPE // 2
    xn = _rms(x_ref[0], g_ref[...]).astype(BF16)
    c = jnp.dot(xn, win_ref[...], preferred_element_type=F32)
    o_kv = MLA_Q_RANK
    qm_ref[0] = lax.dot_general(wqmt_ref[...], xn, _NT, preferred_element_type=F32).astype(BF16)
    cq = _rms(c[:, :o_kv], gq_ref[...]).astype(BF16)
    ckv = _rms(c[:, o_kv:], gkv_ref[...]).astype(BF16)
    cos_t, sin_t = cos_ref[0], sin_ref[0]

    def rope_t(x1, x2):
        return x1 * cos_t - x2 * sin_t, x2 * cos_t + x1 * sin_t

    qscale = ((MLA_NOPE + MLA_ROPE) ** -0.5) * math.log2(math.e)
    qt = lax.dot_general(wuqt_ref[...], cq, _NT, preferred_element_type=F32)
    zpad = jnp.zeros((MLA_QK_PAD - MLA_NOPE - MLA_ROPE, qt.shape[1]), F32)
    for h in range(MLA_HEADS):
        b0 = h * MLA_QK_PAD
        r1, r2 = rope_t(qt[b0 + MLA_NOPE:b0 + MLA_NOPE + half],
                        qt[b0 + MLA_NOPE + half:b0 + MLA_NOPE + MLA_ROPE])
        blk = jnp.concatenate([qt[b0:b0 + MLA_NOPE], r1, r2, zpad], axis=0) * qscale
        qt_ref[0, 0, b0:b0 + MLA_QK_PAD] = blk.astype(BF16)

    krt = lax.dot_general(wkrt_ref[...], xn, _NT, preferred_element_type=F32)
    r1, r2 = rope_t(krt[MLA_NOPE:MLA_NOPE + half], krt[MLA_NOPE + half:MLA_NOPE + MLA_ROPE])
    kr = jnp.concatenate([krt[:MLA_NOPE], r1, r2, krt[MLA_NOPE + MLA_ROPE:]], axis=0).T
    k = jnp.dot(ckv, wuk_ref[...], preferred_element_type=F32)
    for h in range(MLA_HEADS):
        sl = slice(h * MLA_QK_PAD, (h + 1) * MLA_QK_PAD)
        k_ref[0, :, sl] = (k[:, sl] + kr).astype(BF16)

    vt = lax.dot_general(wuvt_ref[...], ckv, _NT, preferred_element_type=F32)
    rowid = lax.broadcasted_iota(jnp.int32, vt.shape, 0) % MLA_VT_ROWS
    vt_ref[0, 0] = jnp.where(rowid == MLA_V, 1.0, vt).astype(BF16)


def _inproj1(x, g, win, wqmt, wkrt, gq, gkv, wuqt, wuk, wuvt, cos_t, sin_t):
    B, S, D = x.shape
    tm = TQ
    assert S % tm == 0
    tok = lambda w: pl.BlockSpec((1, tm, w), lambda b, i: (b, i, 0))
    tok_t = lambda r: pl.BlockSpec((1, r, tm), lambda b, i: (b, 0, i))
    consts = (g, win, wqmt, wkrt, gq, gkv, wuqt, wuk, wuvt)
    nvt = wuvt.shape[0]
    return pl.pallas_call(
        _inproj1_kernel,
        grid=(B, S // tm),
        in_specs=[tok(D)] + [_const_spec(a.shape) for a in consts] + [tok_t(cos_t.shape[1])] * 2,
        out_specs=[pl.BlockSpec((1, 1, wuqt.shape[0], tm), lambda b, i: (b, i, 0, 0)), tok(wuk.shape[1]),
                   pl.BlockSpec((1, 1, nvt, tm), lambda b, i: (b, i, 0, 0)), tok_t(MEM_WIDTH)],
        out_shape=[jax.ShapeDtypeStruct((B, S // tm, wuqt.shape[0], tm), BF16),
                   jax.ShapeDtypeStruct((B, S, wuk.shape[1]), BF16),
                   jax.ShapeDtypeStruct((B, S // tm, nvt, tm), BF16),
                   jax.ShapeDtypeStruct((B, MEM_WIDTH, S), BF16)],
        compiler_params=_params("arbitrary", "arbitrary"),
        name="inproj1",
    )(x, *consts, cos_t, sin_t)


def _flash_kernel(qt_ref, k_ref, vt_ref, y_ref, sa_ref, sb_ref, smax_ref, acc_ref, m_ref):
    slots = (sa_ref, sb_ref)
    key_i = lax.broadcasted_iota(jnp.int32, (TQ, TQ), 0)
    qry_i = lax.broadcasted_iota(jnp.int32, (TQ, TQ), 1)

    def col_max(s):
        parts = [jnp.max(s[r:r + 128], axis=0, keepdims=True) for r in range(0, TQ, 128)]
        return jnp.maximum(jnp.maximum(parts[0], parts[1]), jnp.maximum(parts[2], parts[3]))

    def v_t(hh, blk):
        return vt_ref[0, blk, hh * MLA_VT_ROWS:(hh + 1) * MLA_VT_ROWS, :]

    def tile_pair(j, carry):
        _flash_tile_pair(j, qt_ref, k_ref, y_ref, slots, smax_ref, acc_ref, m_ref, key_i, qry_i, col_max, v_t)
        return carry

    lax.fori_loop(0, qt_ref.shape[1] // 2, tile_pair, 0)


def _flash_tile_pair(j, qt_ref, k_ref, y_ref, slots, smax_ref, acc_ref, m_ref, key_i, qry_i, col_max, v_t):
    ia = 2 * j
    ib = 2 * j + 1
    tiles = (ia, ib)

    def scores_t(qt, hh, blk):
        kh = k_ref[0, pl.ds(pl.multiple_of(blk * TQ, TQ), TQ), hh * MLA_QK_PAD:(hh + 1) * MLA_QK_PAD]
        qh = qt_ref[0, tiles[qt], hh * MLA_QK_PAD:(hh + 1) * MLA_QK_PAD, :]
        return jnp.dot(kh, qh, preferred_element_type=F32)

    def qk_stage(qt, slot, blk, diagonal=False):
        for hh in range(2):
            s = scores_t(qt, hh, blk)
            if diagonal:
                s = jnp.where(key_i <= qry_i, s, -jnp.inf)
            slots[slot][hh] = s
            smax_ref[slot, hh] = col_max(s)

    def pv_stage(qt, slot, blk):
        for hh in range(2):
            m_prev = m_ref[qt, hh]
            m_new = jnp.maximum(m_prev, smax_ref[slot, hh])
            alpha = jnp.exp2(m_prev - m_new)
            p = jnp.exp2(slots[slot][hh] - m_new).astype(BF16)
            acc_ref[qt, hh] = alpha * acc_ref[qt, hh] + jnp.dot(v_t(hh, blk), p, preferred_element_type=F32)
            m_ref[qt, hh] = m_new

    def finalize(qt):
        outs = []
        for hh in range(2):
            acc = acc_ref[qt, hh]
            outs.append(acc[:MLA_V] * (1.0 / acc[MLA_V:MLA_V + 1]))
        y_ref[0, tiles[qt]] = jnp.concatenate(outs, axis=0).astype(BF16)

    m_ref[...] = jnp.full(m_ref.shape, -jnp.inf, F32)
    acc_ref[...] = jnp.zeros_like(acc_ref)

    @pl.when(j == 0)
    def _():
        qk_stage(0, 0, 0, diagonal=True)

    @pl.when(j >= 1)
    def _():
        qk_stage(0, 0, 0)

    def body_a(t, carry):
        qk_stage(0, 1, 2 * t + 1)
        pv_stage(0, 0, 2 * t)
        qk_stage(0, 0, 2 * t + 2)
        pv_stage(0, 1, 2 * t + 1)
        return carry

    lax.fori_loop(0, j - 1, body_a, 0)

    @pl.when(j >= 1)
    def _():
        qk_stage(0, 1, ia - 1)
        pv_stage(0, 0, ia - 2)
        qk_stage(0, 0, ia, diagonal=True)
        pv_stage(0, 1, ia - 1)

    qk_stage(1, 1, 0)
    pv_stage(0, 0, ia)
    finalize(0)

    def body_b(t, carry):
        qk_stage(1, 0, 2 * t + 1)
        pv_stage(1, 1, 2 * t)
        qk_stage(1, 1, 2 * t + 2)
        pv_stage(1, 0, 2 * t + 1)
        return carry

    lax.fori_loop(0, j, body_b, 0)

    qk_stage(1, 0, ib, diagonal=True)
    pv_stage(1, 1, ib - 1)
    pv_stage(1, 0, ib)
    finalize(1)


def _flash(qt, k, vt):
    B, S, _ = k.shape
    assert S % (2 * TQ) == 0
    nq = S // TQ
    pairs = MLA_HEADS // 2
    return pl.pallas_call(
        _flash_kernel,
        grid=(B, pairs),
        in_specs=[pl.BlockSpec((1, nq, 2 * MLA_QK_PAD, TQ), lambda b, p: (b, 0, p, 0)),
                  pl.BlockSpec((1, S, 2 * MLA_QK_PAD), lambda b, p: (b, 0, p)),
                  pl.BlockSpec((1, nq, 2 * MLA_VT_ROWS, TQ), lambda b, p: (b, 0, p, 0))],
        out_specs=pl.BlockSpec((1, nq, 2 * MLA_V, TQ), lambda b, p: (b, 0, p, 0)),
        out_shape=jax.ShapeDtypeStruct((B, nq, MLA_HEADS * MLA_V, TQ), BF16),
        scratch_shapes=[pltpu.VMEM((2, TQ, TQ), F32),
                        pltpu.VMEM((2, TQ, TQ), F32),
                        pltpu.VMEM((2, 2, 1, TQ), F32),
                        pltpu.VMEM((2, 2, MLA_VT_ROWS, TQ), F32),
                        pltpu.VMEM((2, 2, 1, TQ), F32)],
        compiler_params=_params("arbitrary", "arbitrary"),
        name="mla_flash",
    )(qt, k, vt)


def _pad_heads(w, heads, d, dp):
    lead = w.shape[:-1]
    w = w.reshape(lead + (heads, d))
    w = jnp.pad(w, [(0, 0)] * len(lead) + [(0, 0), (0, dp - d)])
    return w.reshape(lead + (heads * dp,))


def kernel(x, mem, positions, mem_norm, w_mem_kv, norm_mix0, w_in0, b_igate0, b_fgate0, w_conv0, w_hnorm0, w_out0, norm_ffn0, w_ff1_0, w_ff2_0, norm_mix1, w_in1, w_qnorm1, w_uq1, w_kvnorm1, w_ukv1, w_out1, norm_ffn1, w_ff1_1, w_ff2_1, final_norm):
    row = lambda g: g.reshape(1, -1).astype(F32)

    km, vmt = _mem_kv(mem, row(mem_norm), w_mem_kv[:, :MEM_WIDTH].astype(BF16),
                      w_mem_kv[:, MEM_WIDTH:].T.astype(BF16))

    nq = ML_HEADS * ML_QK_DIM
    o_v, o_o, o_g = 2 * nq, 2 * nq + MIX_WIDTH, 2 * nq + 2 * MIX_WIDTH
    o_qm = o_g + 2 * ML_HEADS
    wqk = jnp.concatenate([_pad_heads(w_in0[:, :nq], ML_HEADS, ML_QK_DIM, ML_QK_PAD),
                           _pad_heads(w_in0[:, nq:o_v], ML_HEADS, ML_QK_DIM, ML_QK_PAD)], axis=1)
    wvgt0 = jnp.concatenate([_pad_heads(w_in0[:, o_v:o_o], ML_HEADS, ML_V_DIM, ML_VT_ROWS),
                             jnp.pad(w_in0[:, o_g:o_qm], ((0, 0), (0, ML_GATE_ROWS - 2 * ML_HEADS)))], axis=1).T
    wot0 = w_in0[:, o_o:o_g].T
    wqmt0 = w_in0[:, o_qm:].T
    wconv = jnp.concatenate([_pad_heads(w_conv0[:, :nq], ML_HEADS, ML_QK_DIM, ML_QK_PAD),
                             _pad_heads(w_conv0[:, nq:], ML_HEADS, ML_QK_DIM, ML_QK_PAD)], axis=1).astype(F32)
    gate_bias = jnp.pad(jnp.concatenate([b_igate0, b_fgate0]), (0, ML_GATE_ROWS - 2 * ML_HEADS)).astype(F32)
    gate_bias = jnp.broadcast_to(gate_bias[:, None], (ML_GATE_ROWS, LANES))
    hnorm = jnp.broadcast_to(w_hnorm0.astype(F32)[:, :, None], (ML_HEADS, ML_V_DIM, LANES))

    qk, vt0, ot0, qm0, gates = _inproj0(x, row(norm_mix0), wqk.astype(BF16), wvgt0.astype(BF16),
                                        wot0.astype(BF16), wqmt0.astype(BF16))
    yt0 = _mlstm(qk, vt0, ot0, gates, wconv, gate_bias, hnorm)
    x = _block_tail(x, yt0, qm0, km, vmt, w_out0.astype(BF16), row(norm_ffn0), w_ff1_0, w_ff2_0,
                    row(final_norm), False)

    o_kr = MLA_Q_RANK + MLA_KV_RANK
    o_qm1 = o_kr + MLA_ROPE
    win1 = w_in1[:, :o_kr]
    wqmt1 = w_in1[:, o_qm1:].T
    wkrt = jnp.pad(w_in1[:, o_kr:o_qm1].T, ((MLA_NOPE, LANES - MLA_NOPE - MLA_ROPE), (0, 0)))
    wuqt = _pad_heads(w_uq1, MLA_HEADS, MLA_NOPE + MLA_ROPE, MLA_QK_PAD).T
    ukv = w_ukv1.reshape(MLA_KV_RANK, MLA_HEADS, MLA_NOPE + MLA_V)
    wuk = _pad_heads(ukv[:, :, :MLA_NOPE].reshape(MLA_KV_RANK, -1), MLA_HEADS, MLA_NOPE, MLA_QK_PAD)
    wuvt = _pad_heads(ukv[:, :, MLA_NOPE:].reshape(MLA_KV_RANK, -1), MLA_HEADS, MLA_V, MLA_VT_ROWS).T

    cos_t, sin_t = _rope_tables(positions)
    qt1, k1, vt1, qm1 = _inproj1(x, row(norm_mix1), win1.astype(BF16), wqmt1.astype(BF16), wkrt.astype(BF16),
                                 row(w_qnorm1), row(w_kvnorm1), wuqt.astype(BF16), wuk.astype(BF16),
                                 wuvt.astype(BF16), cos_t, sin_t)
    yt1 = _flash(qt1, k1, vt1)
    return _block_tail(x, yt1, qm1, km, vmt, w_out1.astype(BF16), row(norm_ffn1), w_ff1_1, w_ff2_1,
                       row(final_norm), True)
```

```python
import functools
import math

import jax
import jax.numpy as jnp
from jax import lax
from jax.experimental import pallas as pl
from jax.experimental.pallas import tpu as pltpu

F32 = jnp.float32
BF16 = jnp.bfloat16
EPS = 1e-6

D_MODEL = 1024
N_MEM = 256
MEM_HEADS = 4
MEM_HEAD_DIM = 64
MEM_WIDTH = MEM_HEADS * MEM_HEAD_DIM
MIX_WIDTH = D_MODEL - MEM_WIDTH

ML_HEADS = 4
ML_V_DIM = MIX_WIDTH // ML_HEADS
ML_QK_DIM = ML_V_DIM // 2
ML_CONV = 4
ML_CHUNK = 128
ML_QK_PAD = 128
ML_VT_ROWS = 256
ML_ONES_ROW = ML_V_DIM
ML_NB = 4
ML_GATE_ROWS = 16

MLA_HEADS = 12
MLA_NOPE = 64
MLA_ROPE = 32
MLA_V = 64
MLA_Q_RANK = 384
MLA_KV_RANK = 256
MLA_QK_PAD = 128
MLA_VT_ROWS = 80
ROPE_THETA = 10000.0
D_FF = 4 * D_MODEL

LANES = 128
VMEM_LIMIT = 56 * 1024 * 1024

TM_PROJ = 512
FF_CHUNK = 1024
TQ = 512

_NT = (((1,), (1,)), ((), ()))


def _params(*sem):
    return pltpu.CompilerParams(dimension_semantics=sem, vmem_limit_bytes=VMEM_LIMIT)


def _rms(x, g):
    return x * lax.rsqrt(jnp.mean(x * x, axis=-1, keepdims=True) + EPS) * g


def _const_spec(shape):
    nd = len(shape)
    return pl.BlockSpec(shape, lambda *_: (0,) * nd)


def _single_buffered_spec(shape):
    nd = len(shape)
    return pl.BlockSpec(shape, lambda *_: (0,) * nd, pipeline_mode=pl.Buffered(1))


def _mem_kv_kernel(mem_ref, g_ref, wk_ref, wvt_ref, k_ref, vt_ref):
    xn = _rms(mem_ref[0], g_ref[...]).astype(BF16)
    k = jnp.dot(xn, wk_ref[...], preferred_element_type=F32)
    vt = lax.dot_general(wvt_ref[...], xn, _NT, preferred_element_type=F32)
    k = k * (MEM_HEAD_DIM ** -0.5)
    col_head = lax.broadcasted_iota(jnp.int32, k.shape, 1) // MEM_HEAD_DIM
    row_head = lax.broadcasted_iota(jnp.int32, vt.shape, 0) // MEM_HEAD_DIM
    for h in range(MEM_HEADS):
        k_ref[0, h] = jnp.where(col_head == h, k, 0.0).astype(BF16)
        vt_ref[0, h] = jnp.where(row_head == h, vt, 0.0).astype(BF16)


def _mem_kv(mem, g, wk, wvt):
    B = mem.shape[0]
    out = jax.ShapeDtypeStruct((B, MEM_HEADS, MEM_WIDTH, N_MEM), BF16)
    return pl.pallas_call(
        _mem_kv_kernel,
        grid=(B,),
        in_specs=[pl.BlockSpec((1, N_MEM, D_MODEL), lambda b: (b, 0, 0)),
                  _const_spec(g.shape), _const_spec(wk.shape), _const_spec(wvt.shape)],
        out_specs=[pl.BlockSpec((1, MEM_HEADS, N_MEM, MEM_WIDTH), lambda b: (b, 0, 0, 0)),
                   pl.BlockSpec((1, MEM_HEADS, MEM_WIDTH, N_MEM), lambda b: (b, 0, 0, 0))],
        out_shape=[out, out],
        compiler_params=_params("arbitrary"),
        name="mem_kv",
    )(mem, g, wk, wvt)


def _inproj0_kernel(x_ref, g_ref, wqk_ref, wv_ref, wo_ref, wqm_ref,
                    qk_ref, v_ref, o_ref, qm_ref, gate_ref):
    xn = _rms(x_ref[0], g_ref[...]).astype(BF16)
    qk_ref[0] = jnp.dot(xn, wqk_ref[...], preferred_element_type=F32).astype(BF16)
    vg = lax.dot_general(wv_ref[...], xn, _NT, preferred_element_type=F32)
    nv = v_ref.shape[1]
    vt = vg[:nv]
    rowid = lax.broadcasted_iota(jnp.int32, vt.shape, 0) % ML_VT_ROWS
    v_ref[0] = jnp.where(rowid == ML_ONES_ROW, 1.0, vt).astype(BF16)
    gate_ref[0] = vg[nv:]
    o_ref[0] = lax.dot_general(wo_ref[...], xn, _NT, preferred_element_type=F32).astype(BF16)
    qm_ref[0] = lax.dot_general(wqm_ref[...], xn, _NT, preferred_element_type=F32).astype(BF16)


def _inproj0(x, g, wqk, wvgt, wot, wqmt):
    B, S, D = x.shape
    tm = min(TM_PROJ, S)
    tok = lambda w: pl.BlockSpec((1, tm, w), lambda b, i: (b, i, 0))
    tok_t = lambda r: pl.BlockSpec((1, r, tm), lambda b, i: (b, 0, i))
    consts = (g, wqk, wvgt, wot, wqmt)
    nv = wvgt.shape[0] - ML_GATE_ROWS
    return pl.pallas_call(
        _inproj0_kernel,
        grid=(B, S // tm),
        in_specs=[tok(D)] + [_const_spec(a.shape) for a in consts],
        out_specs=[tok(wqk.shape[1]), tok_t(nv), tok_t(wot.shape[0]), tok_t(wqmt.shape[0]),
                   tok_t(ML_GATE_ROWS)],
        out_shape=[jax.ShapeDtypeStruct((B, S, wqk.shape[1]), BF16),
                   jax.ShapeDtypeStruct((B, nv, S), BF16),
                   jax.ShapeDtypeStruct((B, wot.shape[0], S), BF16),
                   jax.ShapeDtypeStruct((B, wqmt.shape[0], S), BF16),
                   jax.ShapeDtypeStruct((B, ML_GATE_ROWS, S), F32)],
        compiler_params=_params("arbitrary", "arbitrary"),
        name="inproj0",
    )(x, *consts)


def _split3(x):
    hi = x.astype(BF16)
    r = x - hi.astype(F32)
    mid = r.astype(BF16)
    lo = (r - mid.astype(F32)).astype(BF16)
    return hi, mid, lo


def _mlstm_kernel(qk_ref, vt_ref, o_ref, gate_ref, wconv_ref, bias_ref, hnorm_ref,
                  y_ref, ct_ref, m_ref, tail_ref):
    L = ML_CHUNK
    HQ = ML_HEADS * ML_QK_PAD

    @pl.when(pl.program_id(1) == 0)
    def _():
        ct_ref[...] = jnp.zeros_like(ct_ref)
        m_ref[...] = jnp.zeros_like(m_ref)
        tail_ref[...] = jnp.zeros_like(tail_ref)

    key_i = lax.broadcasted_iota(jnp.int32, (L, L), 0)
    qry_i = lax.broadcasted_iota(jnp.int32, (L, L), 1)
    causal_t = key_i <= qry_i
    triu = jnp.where(causal_t, 1.0, 0.0).astype(BF16)
    wc = wconv_ref[...]

    def prepare(nb):
        x = qk_ref[nb].astype(F32)
        tail = tail_ref[nb]
        row8 = lax.broadcasted_iota(jnp.int32, tail.shape, 0)
        conv = x * wc[ML_CONV - 1:ML_CONV]
        for s in range(1, ML_CONV):
            xs = pltpu.roll(x, s, axis=0)
            head = jnp.where(row8 < s, pltpu.roll(tail, s, axis=0), xs[0:8])
            xs = jnp.concatenate([head, xs[8:]], axis=0)
            conv = conv + xs * wc[ML_CONV - 1 - s:ML_CONV - s]
        tail_ref[nb] = x[L - 8:L]
        act = conv * jax.nn.sigmoid(conv)
        q = act[:, :HQ].astype(BF16)
        k = (act[:, HQ:] * (ML_QK_DIM ** -0.5)).astype(BF16)

        g_rows = gate_ref[nb] + bias_ref[...]
        log_f = jnp.minimum(g_rows, 0.0) - jnp.log1p(jnp.exp(-jnp.abs(g_rows)))
        b_rows = sum(jnp.dot(part, triu, preferred_element_type=F32) for part in _split3(log_f))
        c_rows = g_rows[:ML_HEADS] - b_rows[ML_HEADS:2 * ML_HEADS]
        c_cols = jnp.concatenate([c_rows, jnp.zeros((L - ML_HEADS, L), F32)], axis=0).T
        return q, k, b_rows, c_rows, c_cols

    def head(nb, h, q, k, b_rows, c_rows, c_cols):
        b_row = b_rows[ML_HEADS + h:ML_HEADS + h + 1, :]
        c_row = c_rows[h:h + 1, :]
        c_col = c_cols[:, h:h + 1]
        m_prev = m_ref[nb, h][:, 0:1]
        qh = q[:, h * ML_QK_PAD:(h + 1) * ML_QK_PAD]
        kh = k[:, h * ML_QK_PAD:(h + 1) * ML_QK_PAD]
        vth = vt_ref[nb, h * ML_VT_ROWS:(h + 1) * ML_VT_ROWS, :]
        ct_prev = ct_ref[nb, h]

        cm = jnp.where(causal_t, c_col, -jnp.inf)
        m_row = jnp.maximum(m_prev, jnp.max(cm, axis=0, keepdims=True))
        w_t = jnp.exp(cm - m_row)
        a_row = jnp.exp(m_prev - m_row)
        s_t = (lax.dot_general(kh, qh, _NT, preferred_element_type=F32) * w_t).astype(BF16)
        num_t = (jnp.dot(vth, s_t, preferred_element_type=F32)
                 + a_row * lax.dot_general(ct_prev.astype(BF16), qh, _NT, preferred_element_type=F32))
        den = num_t[ML_ONES_ROW:ML_ONES_ROW + 1]
        inv = 1.0 / jnp.maximum(jnp.abs(den), jnp.exp(-(b_row + m_row)))
        hc_t = num_t[:ML_V_DIM] * inv
        ms = jnp.sum(hc_t * hc_t, axis=0, keepdims=True) * (1.0 / ML_V_DIM)
        hn_t = hc_t * lax.rsqrt(ms + EPS) * hnorm_ref[h]
        og_t = o_ref[nb, h * ML_V_DIM:(h + 1) * ML_V_DIM, :].astype(F32)
        y_ref[nb, 0, h * ML_V_DIM:(h + 1) * ML_V_DIM, :] = (hn_t * jax.nn.sigmoid(og_t)).astype(BF16)

        m_end = m_row[:, L - 1:L]
        u_row = jnp.exp(c_row - m_end)
        decay = jnp.exp(m_prev - m_end)
        uv_t = (vth.astype(F32) * u_row).astype(BF16)
        ct_ref[nb, h] = decay * ct_prev + jnp.dot(uv_t, kh, preferred_element_type=F32)
        m_ref[nb, h] = jnp.broadcast_to(b_row[:, L - 1:L] + m_end, (1, LANES))

    prepared = [prepare(nb) for nb in range(ML_NB)]
    for h in range(ML_HEADS):
        for nb in range(ML_NB):
            head(nb, h, *prepared[nb])


def _mlstm(qk, vt, ot, gates, wconv, bias, hnorm):
    B, S, _ = qk.shape
    L = ML_CHUNK
    assert B % ML_NB == 0 and S % TM_PROJ == 0
    per_tile = TM_PROJ // L
    tok = lambda w: pl.BlockSpec((ML_NB, L, w), lambda b, c: (b, c, 0))
    tok_t = lambda r: pl.BlockSpec((ML_NB, r, L), lambda b, c: (b, 0, c))
    return pl.pallas_call(
        _mlstm_kernel,
        grid=(B // ML_NB, S // L),
        in_specs=[tok(qk.shape[2]), tok_t(vt.shape[1]), tok_t(ot.shape[1]), tok_t(gates.shape[1]),
                  _const_spec(wconv.shape), _const_spec(bias.shape), _const_spec(hnorm.shape)],
        out_specs=pl.BlockSpec((ML_NB, 1, ot.shape[1], L), lambda b, c: (b, c // per_tile, 0, c % per_tile)),
        out_shape=jax.ShapeDtypeStruct((B, S // TM_PROJ, ot.shape[1], TM_PROJ), BF16),
        scratch_shapes=[pltpu.VMEM((ML_NB, ML_HEADS, ML_VT_ROWS, ML_QK_PAD), F32),
                        pltpu.VMEM((ML_NB, ML_HEADS, 1, LANES), F32),
                        pltpu.VMEM((ML_NB, 8, qk.shape[2]), F32)],
        compiler_params=_params("arbitrary", "arbitrary"),
        name="mlstm",
    )(qk, vt, ot, gates, wconv, bias, hnorm)


def _block_tail_kernel(x_ref, yt_ref, qmt_ref, km_ref, vmt_ref, wout_ref, g_ref, w1_ref, w2_ref, gf_ref,
                       out_ref, x1_ref, *, final_norm):
    tm = x_ref.shape[1]
    halves = [slice(r0, r0 + tm // 2) for r0 in range(0, tm, tm // 2)]
    scores = [[jnp.dot(km_ref[0, h], qmt_ref[0, :, cols], preferred_element_type=F32)
               for h in range(MEM_HEADS)] for cols in halves]
    for cols, s_heads in zip(halves, scores):
        ymem_t = None
        for h in range(MEM_HEADS):
            s = s_heads[h]
            e = jnp.exp(s - jnp.max(s, axis=0, keepdims=True))
            p = (e * (1.0 / jnp.sum(e, axis=0, keepdims=True))).astype(BF16)
            oh = jnp.dot(vmt_ref[0, h], p, preferred_element_type=F32)
            ymem_t = oh if ymem_t is None else ymem_t + oh
        ycat = jnp.concatenate([yt_ref[0, 0, :, cols], ymem_t.astype(BF16)], axis=0).T
        x1_ref[cols, :] = x_ref[0, cols, :] + jnp.dot(ycat, wout_ref[...], preferred_element_type=F32)

    x = x1_ref[...]
    hn = _rms(x, g_ref[...]).astype(BF16)
    acc = x
    for c in range(D_FF // FF_CHUNK):
        w1c = w1_ref[:, c * FF_CHUNK:(c + 1) * FF_CHUNK].astype(BF16)
        a = jnp.dot(hn, w1c, preferred_element_type=F32)
        a = jnp.square(jnp.maximum(a, 0.0)).astype(BF16)
        w2c = w2_ref[c * FF_CHUNK:(c + 1) * FF_CHUNK, :].astype(BF16)
        acc = acc + jnp.dot(a, w2c, preferred_element_type=F32)
    if final_norm:
        acc = _rms(acc, gf_ref[...])
    out_ref[0] = acc


def _block_tail(x, yt, qmt, km, vmt, wout, g, w1, w2, gf, final_norm):
    B, S, D = x.shape
    tm = yt.shape[3]
    tok = lambda w: pl.BlockSpec((1, tm, w), lambda b, i: (b, i, 0))
    tok_t = lambda r: pl.BlockSpec((1, r, tm), lambda b, i: (b, 0, i))
    per_b = pl.BlockSpec((1,) + km.shape[1:], lambda b, i: (b, 0, 0, 0))
    return pl.pallas_call(
        functools.partial(_block_tail_kernel, final_norm=final_norm),
        grid=(B, S // tm),
        in_specs=[tok(D), pl.BlockSpec((1, 1, yt.shape[2], tm), lambda b, i: (b, i, 0, 0)),
                  tok_t(qmt.shape[1]), per_b, per_b, _single_buffered_spec(wout.shape),
                  _const_spec(g.shape), _single_buffered_spec(w1.shape), _single_buffered_spec(w2.shape),
                  _const_spec(gf.shape)],
        out_specs=tok(D),
        out_shape=jax.ShapeDtypeStruct(x.shape, F32),
        scratch_shapes=[pltpu.VMEM((tm, D), F32)],
        compiler_params=_params("arbitrary", "arbitrary"),
        name="block_tail",
    )(x, yt, qmt, km, vmt, wout, g, w1, w2, gf)


def _rope_kernel(pos_ref, inv_ref, cos_ref, sin_ref):
    ang = pos_ref[0].astype(F32) * inv_ref[...]
    cos_ref[0] = jnp.cos(ang)
    sin_ref[0] = jnp.sin(ang)


def _rope_tables(positions):
    B, S = positions.shape
    nf = MLA_ROPE // 2
    inv = (ROPE_THETA ** (-jnp.arange(0, MLA_ROPE, 2, dtype=F32) / MLA_ROPE)).reshape(nf, 1)
    out = jax.ShapeDtypeStruct((B, nf, S), F32)
    spec = pl.BlockSpec((1, nf, S), lambda b: (b, 0, 0))
    return pl.pallas_call(
        _rope_kernel,
        grid=(B,),
        in_specs=[pl.BlockSpec((1, 1, S), lambda b: (b, 0, 0)), _const_spec(inv.shape)],
        out_specs=[spec, spec],
        out_shape=[out, out],
        compiler_params=_params("arbitrary"),
        name="rope_tables",
    )(positions.reshape(B, 1, S), inv)


def _inproj1_kernel(x_ref, g_ref, win_ref, wqmt_ref, wkrt_ref, gq_ref, gkv_ref, wuqt_ref, wuk_ref, wuvt_ref,
                    cos_ref, sin_ref, qt_ref, k_ref, vt_ref, qm_ref):
    half = MLA_ROPE // 2
    xn = _rms(x_ref[0], g_ref[...]).astype(BF16)
    c = jnp.dot(xn, win_ref[...], preferred_element_type=F32)
    o_kv = MLA_Q_RANK
    qm_ref[0] = lax.dot_general(wqmt_ref[...], xn, _NT, preferred_element_type=F32).astype(BF16)
    cq = _rms(c[:, :o_kv], gq_ref[...]).astype(BF16)
    ckv = _rms(c[:, o_kv:], gkv_ref[...]).astype(BF16)
    cos_t, sin_t = cos_ref[0], sin_ref[0]

    def rope_t(x1, x2):
        return x1 * cos_t - x2 * sin_t, x2 * cos_t + x1 * sin_t

    qscale = ((MLA_NOPE + MLA_ROPE) ** -0.5) * math.log2(math.e)
    qt = lax.dot_general(wuqt_ref[...], cq, _NT, preferred_element_type=F32)
    zpad = jnp.zeros((MLA_QK_PAD - MLA_NOPE - MLA_ROPE, qt.shape[1]), F32)
    for h in range(MLA_HEADS):
        b0 = h * MLA_QK_PAD
        r1, r2 = rope_t(qt[b0 + MLA_NOPE:b0 + MLA_NOPE + half],
                        qt[b0 + MLA_NOPE + half:b0 + MLA_NOPE + MLA_ROPE])
        blk = jnp.concatenate([qt[b0:b0 + MLA_NOPE], r1, r2, zpad], axis=0) * qscale
        qt_ref[0, 0, b0:b0 + MLA_QK_PAD] = blk.astype(BF16)

    krt = lax.dot_general(wkrt_ref[...], xn, _NT, preferred_element_type=F32)
    r1, r2 = rope_t(krt[MLA_NOPE:MLA_NOPE + half], krt[MLA_NOPE + half:MLA_NOPE + MLA_ROPE])
    kr = jnp.concatenate([krt[:MLA_NOPE], r1, r2, krt[MLA_NOPE + MLA_ROPE:]], axis=0).T
    k = jnp.dot(ckv, wuk_ref[...], preferred_element_type=F32)
    for h in range(MLA_HEADS):
        sl = slice(h * MLA_QK_PAD, (h + 1) * MLA_QK_PAD)
        k_ref[0, :, sl] = (k[:, sl] + kr).astype(BF16)

    vt = lax.dot_general(wuvt_ref[...], ckv, _NT, preferred_element_type=F32)
    rowid = lax.broadcasted_iota(jnp.int32, vt.shape, 0) % MLA_VT_ROWS
    vt_ref[0, 0] = jnp.where(rowid == MLA_V, 1.0, vt).astype(BF16)


def _inproj1(x, g, win, wqmt, wkrt, gq, gkv, wuqt, wuk, wuvt, cos_t, sin_t):
    B, S, D = x.shape
    tm = TQ
    assert S % tm == 0
    tok = lambda w: pl.BlockSpec((1, tm, w), lambda b, i: (b, i, 0))
    tok_t = lambda r: pl.BlockSpec((1, r, tm), lambda b, i: (b, 0, i))
    consts = (g, win, wqmt, wkrt, gq, gkv, wuqt, wuk, wuvt)
    nvt = wuvt.shape[0]
    return pl.pallas_call(
        _inproj1_kernel,
        grid=(B, S // tm),
        in_specs=[tok(D)] + [_const_spec(a.shape) for a in consts] + [tok_t(cos_t.shape[1])] * 2,
        out_specs=[pl.BlockSpec((1, 1, wuqt.shape[0], tm), lambda b, i: (b, i, 0, 0)), tok(wuk.shape[1]),
                   pl.BlockSpec((1, 1, nvt, tm), lambda b, i: (b, i, 0, 0)), tok_t(MEM_WIDTH)],
        out_shape=[jax.ShapeDtypeStruct((B, S // tm, wuqt.shape[0], tm), BF16),
                   jax.ShapeDtypeStruct((B, S, wuk.shape[1]), BF16),
                   jax.ShapeDtypeStruct((B, S // tm, nvt, tm), BF16),
                   jax.ShapeDtypeStruct((B, MEM_WIDTH, S), BF16)],
        compiler_params=_params("arbitrary", "arbitrary"),
        name="inproj1",
    )(x, *consts, cos_t, sin_t)


def _flash_kernel(qt_ref, k_ref, vt_ref, y_ref, sa_ref, sb_ref, smax_ref, acc_ref, m_ref):
    slots = (sa_ref, sb_ref)
    key_i = lax.broadcasted_iota(jnp.int32, (TQ, TQ), 0)
    qry_i = lax.broadcasted_iota(jnp.int32, (TQ, TQ), 1)

    def col_max(s):
        parts = [jnp.max(s[r:r + 128], axis=0, keepdims=True) for r in range(0, TQ, 128)]
        return jnp.maximum(jnp.maximum(parts[0], parts[1]), jnp.maximum(parts[2], parts[3]))

    def v_t(hh, blk):
        return vt_ref[0, blk, hh * MLA_VT_ROWS:(hh + 1) * MLA_VT_ROWS, :]

    def tile_pair(j, carry):
        _flash_tile_pair(j, qt_ref, k_ref, y_ref, slots, smax_ref, acc_ref, m_ref, key_i, qry_i, col_max, v_t)
        return carry

    lax.fori_loop(0, qt_ref.shape[1] // 2, tile_pair, 0)


def _flash_tile_pair(j, qt_ref, k_ref, y_ref, slots, smax_ref, acc_ref, m_ref, key_i, qry_i, col_max, v_t):
    ia = 2 * j
    ib = 2 * j + 1
    tiles = (ia, ib)

    def scores_t(qt, hh, blk):
        kh = k_ref[0, pl.ds(pl.multiple_of(blk * TQ, TQ), TQ), hh * MLA_QK_PAD:(hh + 1) * MLA_QK_PAD]
        qh = qt_ref[0, tiles[qt], hh * MLA_QK_PAD:(hh + 1) * MLA_QK_PAD, :]
        return jnp.dot(kh, qh, preferred_element_type=F32)

    half = TQ // 2

    def qk_stage(qt, slot, blk, diagonal=False):
        for hh in range(2):
            if diagonal:
                kh = k_ref[0, pl.ds(pl.multiple_of(blk * TQ, TQ), TQ), hh * MLA_QK_PAD:(hh + 1) * MLA_QK_PAD]
                qh = qt_ref[0, tiles[qt], hh * MLA_QK_PAD:(hh + 1) * MLA_QK_PAD, :]
                top = jnp.dot(kh[:half], qh, preferred_element_type=F32)
                bot = jnp.dot(kh[half:], qh[:, half:], preferred_element_type=F32)
                s = jnp.concatenate([top, jnp.concatenate([jnp.zeros((half, half), F32), bot], axis=1)], axis=0)
                s = jnp.where(key_i <= qry_i, s, -jnp.inf)
            else:
                s = scores_t(qt, hh, blk)
            slots[slot][hh] = s
            smax_ref[slot, hh] = col_max(s)

    def pv_stage(qt, slot, blk, diagonal=False):
        for hh in range(2):
            m_prev = m_ref[qt, hh]
            m_new = jnp.maximum(m_prev, smax_ref[slot, hh])
            alpha = jnp.exp2(m_prev - m_new)
            p = jnp.exp2(slots[slot][hh] - m_new).astype(BF16)
            vth = v_t(hh, blk)
            if diagonal:
                pv = jnp.concatenate([jnp.dot(vth[:, :half], p[:half, :half], preferred_element_type=F32),
                                      jnp.dot(vth, p[:, half:], preferred_element_type=F32)], axis=1)
            else:
                pv = jnp.dot(vth, p, preferred_element_type=F32)
            acc_ref[qt, hh] = alpha * acc_ref[qt, hh] + pv
            m_ref[qt, hh] = m_new

    def finalize(qt):
        outs = []
        for hh in range(2):
            acc = acc_ref[qt, hh]
            outs.append(acc[:MLA_V] * (1.0 / acc[MLA_V:MLA_V + 1]))
        y_ref[0, tiles[qt]] = jnp.concatenate(outs, axis=0).astype(BF16)

    m_ref[...] = jnp.full(m_ref.shape, -jnp.inf, F32)
    acc_ref[...] = jnp.zeros_like(acc_ref)

    @pl.when(j == 0)
    def _():
        qk_stage(0, 0, 0, diagonal=True)

    @pl.when(j >= 1)
    def _():
        qk_stage(0, 0, 0)

    def body_a(t, carry):
        qk_stage(0, 1, 2 * t + 1)
        pv_stage(0, 0, 2 * t)
        qk_stage(0, 0, 2 * t + 2)
        pv_stage(0, 1, 2 * t + 1)
        return carry

    lax.fori_loop(0, j - 1, body_a, 0)

    @pl.when(j >= 1)
    def _():
        qk_stage(0, 1, ia - 1)
        pv_stage(0, 0, ia - 2)
        qk_stage(0, 0, ia, diagonal=True)
        pv_stage(0, 1, ia - 1)

    qk_stage(1, 1, 0)
    pv_stage(0, 0, ia, diagonal=True)
    finalize(0)

    def body_b(t, carry):
        qk_stage(1, 0, 2 * t + 1)
        pv_stage(1, 1, 2 * t)
        qk_stage(1, 1, 2 * t + 2)
        pv_stage(1, 0, 2 * t + 1)
        return carry

    lax.fori_loop(0, j, body_b, 0)

    qk_stage(1, 0, ib, diagonal=True)
    pv_stage(1, 1, ib - 1)
    pv_stage(1, 0, ib, diagonal=True)
    finalize(1)


def _flash(qt, k, vt):
    B, S, _ = k.shape
    assert S % (2 * TQ) == 0
    nq = S // TQ
    pairs = MLA_HEADS // 2
    return pl.pallas_call(
        _flash_kernel,
        grid=(B, pairs),
        in_specs=[pl.BlockSpec((1, nq, 2 * MLA_QK_PAD, TQ), lambda b, p: (b, 0, p, 0)),
                  pl.BlockSpec((1, S, 2 * MLA_QK_PAD), lambda b, p: (b, 0, p)),
                  pl.BlockSpec((1, nq, 2 * MLA_VT_ROWS, TQ), lambda b, p: (b, 0, p, 0))],
        out_specs=pl.BlockSpec((1, nq, 2 * MLA_V, TQ), lambda b, p: (b, 0, p, 0)),
        out_shape=jax.ShapeDtypeStruct((B, nq, MLA_HEADS * MLA_V, TQ), BF16),
        scratch_shapes=[pltpu.VMEM((2, TQ, TQ), F32),
                        pltpu.VMEM((2, TQ, TQ), F32),
                        pltpu.VMEM((2, 2, 1, TQ), F32),
                        pltpu.VMEM((2, 2, MLA_VT_ROWS, TQ), F32),
                        pltpu.VMEM((2, 2, 1, TQ), F32)],
        compiler_params=_params("arbitrary", "arbitrary"),
        name="mla_flash",
    )(qt, k, vt)


def _pad_heads(w, heads, d, dp):
    lead = w.shape[:-1]
    w = w.reshape(lead + (heads, d))
    w = jnp.pad(w, [(0, 0)] * len(lead) + [(0, 0), (0, dp - d)])
    return w.reshape(lead + (heads * dp,))


def kernel(x, mem, positions, mem_norm, w_mem_kv, norm_mix0, w_in0, b_igate0, b_fgate0, w_conv0, w_hnorm0, w_out0, norm_ffn0, w_ff1_0, w_ff2_0, norm_mix1, w_in1, w_qnorm1, w_uq1, w_kvnorm1, w_ukv1, w_out1, norm_ffn1, w_ff1_1, w_ff2_1, final_norm):
    row = lambda g: g.reshape(1, -1).astype(F32)

    km, vmt = _mem_kv(mem, row(mem_norm), w_mem_kv[:, :MEM_WIDTH].astype(BF16),
                      w_mem_kv[:, MEM_WIDTH:].T.astype(BF16))

    nq = ML_HEADS * ML_QK_DIM
    o_v, o_o, o_g = 2 * nq, 2 * nq + MIX_WIDTH, 2 * nq + 2 * MIX_WIDTH
    o_qm = o_g + 2 * ML_HEADS
    wqk = jnp.concatenate([_pad_heads(w_in0[:, :nq], ML_HEADS, ML_QK_DIM, ML_QK_PAD),
                           _pad_heads(w_in0[:, nq:o_v], ML_HEADS, ML_QK_DIM, ML_QK_PAD)], axis=1)
    wvgt0 = jnp.concatenate([_pad_heads(w_in0[:, o_v:o_o], ML_HEADS, ML_V_DIM, ML_VT_ROWS),
                             jnp.pad(w_in0[:, o_g:o_qm], ((0, 0), (0, ML_GATE_ROWS - 2 * ML_HEADS)))], axis=1).T
    wot0 = w_in0[:, o_o:o_g].T
    wqmt0 = w_in0[:, o_qm:].T
    wconv = jnp.concatenate([_pad_heads(w_conv0[:, :nq], ML_HEADS, ML_QK_DIM, ML_QK_PAD),
                             _pad_heads(w_conv0[:, nq:], ML_HEADS, ML_QK_DIM, ML_QK_PAD)], axis=1).astype(F32)
    gate_bias = jnp.pad(jnp.concatenate([b_igate0, b_fgate0]), (0, ML_GATE_ROWS - 2 * ML_HEADS)).astype(F32)
    gate_bias = jnp.broadcast_to(gate_bias[:, None], (ML_GATE_ROWS, LANES))
    hnorm = jnp.broadcast_to(w_hnorm0.astype(F32)[:, :, None], (ML_HEADS, ML_V_DIM, LANES))

    qk, vt0, ot0, qm0, gates = _inproj0(x, row(norm_mix0), wqk.astype(BF16), wvgt0.astype(BF16),
                                        wot0.astype(BF16), wqmt0.astype(BF16))
    yt0 = _mlstm(qk, vt0, ot0, gates, wconv, gate_bias, hnorm)
    x = _block_tail(x, yt0, qm0, km, vmt, w_out0.astype(BF16), row(norm_ffn0), w_ff1_0, w_ff2_0,
                    row(final_norm), False)

    o_kr = MLA_Q_RANK + MLA_KV_RANK
    o_qm1 = o_kr + MLA_ROPE
    win1 = w_in1[:, :o_kr]
    wqmt1 = w_in1[:, o_qm1:].T
    wkrt = jnp.pad(w_in1[:, o_kr:o_qm1].T, ((MLA_NOPE, LANES - MLA_NOPE - MLA_ROPE), (0, 0)))
    wuqt = _pad_heads(w_uq1, MLA_HEADS, MLA_NOPE + MLA_ROPE, MLA_QK_PAD).T
    ukv = w_ukv1.reshape(MLA_KV_RANK, MLA_HEADS, MLA_NOPE + MLA_V)
    wuk = _pad_heads(ukv[:, :, :MLA_NOPE].reshape(MLA_KV_RANK, -1), MLA_HEADS, MLA_NOPE, MLA_QK_PAD)
    wuvt = _pad_heads(ukv[:, :, MLA_NOPE:].reshape(MLA_KV_RANK, -1), MLA_HEADS, MLA_V, MLA_VT_ROWS).T

    cos_t, sin_t = _rope_tables(positions)
    qt1, k1, vt1, qm1 = _inproj1(x, row(norm_mix1), win1.astype(BF16), wqmt1.astype(BF16), wkrt.astype(BF16),
                                 row(w_qnorm1), row(w_kvnorm1), wuqt.astype(BF16), wuk.astype(BF16),
                                 wuvt.astype(BF16), cos_t, sin_t)
    yt1 = _flash(qt1, k1, vt1)
    return _block_tail(x, yt1, qm1, km, vmt, w_out1.astype(BF16), row(norm_ffn1), w_ff1_1, w_ff2_1,
                       row(final_norm), True)
```

```python
import functools
import math

import jax
import jax.numpy as jnp
from jax import lax
from jax.experimental import pallas as pl
from jax.experimental.pallas import tpu as pltpu

F32 = jnp.float32
BF16 = jnp.bfloat16
EPS = 1e-6

D_MODEL = 1024
N_MEM = 256
MEM_HEADS = 4
MEM_HEAD_DIM = 64
MEM_WIDTH = MEM_HEADS * MEM_HEAD_DIM
MIX_WIDTH = D_MODEL - MEM_WIDTH

ML_HEADS = 4
ML_V_DIM = MIX_WIDTH // ML_HEADS
ML_QK_DIM = ML_V_DIM // 2
ML_CONV = 4
ML_CHUNK = 128
ML_QK_PAD = 128
ML_VT_ROWS = 256
ML_ONES_ROW = ML_V_DIM
ML_NB = 4
ML_GATE_ROWS = 16

MLA_HEADS = 12
MLA_NOPE = 64
MLA_ROPE = 32
MLA_V = 64
MLA_Q_RANK = 384
MLA_KV_RANK = 256
MLA_QK_PAD = 128
MLA_VT_ROWS = 80
ROPE_THETA = 10000.0
D_FF = 4 * D_MODEL

LANES = 128
VMEM_LIMIT = 56 * 1024 * 1024

TM_PROJ = 512
FF_CHUNK = 1024
TQ = 512

_NT = (((1,), (1,)), ((), ()))


def _params(*sem):
    return pltpu.CompilerParams(dimension_semantics=sem, vmem_limit_bytes=VMEM_LIMIT)


def _rms(x, g):
    return x * lax.rsqrt(jnp.mean(x * x, axis=-1, keepdims=True) + EPS) * g


def _const_spec(shape):
    nd = len(shape)
    return pl.BlockSpec(shape, lambda *_: (0,) * nd)


def _single_buffered_spec(shape):
    nd = len(shape)
    return pl.BlockSpec(shape, lambda *_: (0,) * nd, pipeline_mode=pl.Buffered(1))


def _mem_kv_kernel(mem_ref, g_ref, wk_ref, wvt_ref, k_ref, vt_ref):
    xn = _rms(mem_ref[0], g_ref[...]).astype(BF16)
    k = jnp.dot(xn, wk_ref[...], preferred_element_type=F32)
    vt = lax.dot_general(wvt_ref[...], xn, _NT, preferred_element_type=F32)
    k = k * (MEM_HEAD_DIM ** -0.5)
    col_head = lax.broadcasted_iota(jnp.int32, k.shape, 1) // MEM_HEAD_DIM
    row_head = lax.broadcasted_iota(jnp.int32, vt.shape, 0) // MEM_HEAD_DIM
    for h in range(MEM_HEADS):
        k_ref[0, h] = jnp.where(col_head == h, k, 0.0).astype(BF16)
        vt_ref[0, h] = jnp.where(row_head == h, vt, 0.0).astype(BF16)


def _mem_kv(mem, g, wk, wvt):
    B = mem.shape[0]
    out = jax.ShapeDtypeStruct((B, MEM_HEADS, MEM_WIDTH, N_MEM), BF16)
    return pl.pallas_call(
        _mem_kv_kernel,
        grid=(B,),
        in_specs=[pl.BlockSpec((1, N_MEM, D_MODEL), lambda b: (b, 0, 0)),
                  _const_spec(g.shape), _const_spec(wk.shape), _const_spec(wvt.shape)],
        out_specs=[pl.BlockSpec((1, MEM_HEADS, N_MEM, MEM_WIDTH), lambda b: (b, 0, 0, 0)),
                   pl.BlockSpec((1, MEM_HEADS, MEM_WIDTH, N_MEM), lambda b: (b, 0, 0, 0))],
        out_shape=[out, out],
        compiler_params=_params("arbitrary"),
        name="mem_kv",
    )(mem, g, wk, wvt)


def _inproj0_kernel(x_ref, g_ref, wqk_ref, wv_ref, wo_ref, wqm_ref,
                    qk_ref, v_ref, o_ref, qm_ref, gate_ref):
    xn = _rms(x_ref[0], g_ref[...]).astype(BF16)
    qk_ref[0] = jnp.dot(xn, wqk_ref[...], preferred_element_type=F32).astype(BF16)
    vg = lax.dot_general(wv_ref[...], xn, _NT, preferred_element_type=F32)
    nv = v_ref.shape[1]
    vt = vg[:nv]
    rowid = lax.broadcasted_iota(jnp.int32, vt.shape, 0) % ML_VT_ROWS
    v_ref[0] = jnp.where(rowid == ML_ONES_ROW, 1.0, vt).astype(BF16)
    gate_ref[0] = vg[nv:]
    o_ref[0] = lax.dot_general(wo_ref[...], xn, _NT, preferred_element_type=F32).astype(BF16)
    qm_ref[0] = lax.dot_general(wqm_ref[...], xn, _NT, preferred_element_type=F32).astype(BF16)


def _inproj0(x, g, wqk, wvgt, wot, wqmt):
    B, S, D = x.shape
    tm = min(TM_PROJ, S)
    tok = lambda w: pl.BlockSpec((1, tm, w), lambda b, i: (b, i, 0))
    tok_t = lambda r: pl.BlockSpec((1, r, tm), lambda b, i: (b, 0, i))
    consts = (g, wqk, wvgt, wot, wqmt)
    nv = wvgt.shape[0] - ML_GATE_ROWS
    return pl.pallas_call(
        _inproj0_kernel,
        grid=(B, S // tm),
        in_specs=[tok(D)] + [_const_spec(a.shape) for a in consts],
        out_specs=[tok(wqk.shape[1]), tok_t(nv), tok_t(wot.shape[0]), tok_t(wqmt.shape[0]),
                   tok_t(ML_GATE_ROWS)],
        out_shape=[jax.ShapeDtypeStruct((B, S, wqk.shape[1]), BF16),
                   jax.ShapeDtypeStruct((B, nv, S), BF16),
                   jax.ShapeDtypeStruct((B, wot.shape[0], S), BF16),
                   jax.ShapeDtypeStruct((B, wqmt.shape[0], S), BF16),
                   jax.ShapeDtypeStruct((B, ML_GATE_ROWS, S), F32)],
        compiler_params=_params("arbitrary", "arbitrary"),
        name="inproj0",
    )(x, *consts)


def _split3(x):
    hi = x.astype(BF16)
    r = x - hi.astype(F32)
    mid = r.astype(BF16)
    lo = (r - mid.astype(F32)).astype(BF16)
    return hi, mid, lo


def _mlstm_kernel(qk_ref, vt_ref, o_ref, gate_ref, wconv_ref, bias_ref, hnorm_ref,
                  y_ref, ct_ref, m_ref, tail_ref):
    L = ML_CHUNK
    HQ = ML_HEADS * ML_QK_PAD

    @pl.when(pl.program_id(1) == 0)
    def _():
        ct_ref[...] = jnp.zeros_like(ct_ref)
        m_ref[...] = jnp.zeros_like(m_ref)
        tail_ref[...] = jnp.zeros_like(tail_ref)

    key_i = lax.broadcasted_iota(jnp.int32, (L, L), 0)
    qry_i = lax.broadcasted_iota(jnp.int32, (L, L), 1)
    causal_t = key_i <= qry_i
    triu = jnp.where(causal_t, 1.0, 0.0).astype(BF16)
    wc = wconv_ref[...]

    def prepare(nb):
        x = qk_ref[nb].astype(F32)
        tail = tail_ref[nb]
        row8 = lax.broadcasted_iota(jnp.int32, tail.shape, 0)
        conv = x * wc[ML_CONV - 1:ML_CONV]
        for s in range(1, ML_CONV):
            xs = pltpu.roll(x, s, axis=0)
            head = jnp.where(row8 < s, pltpu.roll(tail, s, axis=0), xs[0:8])
            xs = jnp.concatenate([head, xs[8:]], axis=0)
            conv = conv + xs * wc[ML_CONV - 1 - s:ML_CONV - s]
        tail_ref[nb] = x[L - 8:L]
        act = conv * jax.nn.sigmoid(conv)
        q = act[:, :HQ].astype(BF16)
        k = (act[:, HQ:] * (ML_QK_DIM ** -0.5)).astype(BF16)

        g_rows = gate_ref[nb] + bias_ref[...]
        log_f = jnp.minimum(g_rows, 0.0) - jnp.log1p(jnp.exp(-jnp.abs(g_rows)))
        b_rows = sum(jnp.dot(part, triu, preferred_element_type=F32) for part in _split3(log_f))
        c_rows = g_rows[:ML_HEADS] - b_rows[ML_HEADS:2 * ML_HEADS]
        c_cols = jnp.concatenate([c_rows, jnp.zeros((L - ML_HEADS, L), F32)], axis=0).T
        return q, k, b_rows, c_rows, c_cols

    def head(nb, h, q, k, b_rows, c_rows, c_cols):
        b_row = b_rows[ML_HEADS + h:ML_HEADS + h + 1, :]
        c_row = c_rows[h:h + 1, :]
        c_col = c_cols[:, h:h + 1]
        m_prev = m_ref[nb, h][:, 0:1]
        qh = q[:, h * ML_QK_PAD:(h + 1) * ML_QK_PAD]
        kh = k[:, h * ML_QK_PAD:(h + 1) * ML_QK_PAD]
        vth = vt_ref[nb, h * ML_VT_ROWS:(h + 1) * ML_VT_ROWS, :]
        ct_prev = ct_ref[nb, h]

        cm = jnp.where(causal_t, c_col, -jnp.inf)
        m_row = jnp.maximum(m_prev, jnp.max(cm, axis=0, keepdims=True))
        w_t = jnp.exp(cm - m_row)
        a_row = jnp.exp(m_prev - m_row)
        s_t = (lax.dot_general(kh, qh, _NT, preferred_element_type=F32) * w_t).astype(BF16)
        num_t = (jnp.dot(vth, s_t, preferred_element_type=F32)
                 + a_row * lax.dot_general(ct_prev.astype(BF16), qh, _NT, preferred_element_type=F32))
        den = num_t[ML_ONES_ROW:ML_ONES_ROW + 1]
        inv = 1.0 / jnp.maximum(jnp.abs(den), jnp.exp(-(b_row + m_row)))
        hc_t = num_t[:ML_V_DIM] * inv
        ms = jnp.sum(hc_t * hc_t, axis=0, keepdims=True) * (1.0 / ML_V_DIM)
        hn_t = hc_t * lax.rsqrt(ms + EPS) * hnorm_ref[h]
        og_t = o_ref[nb, h * ML_V_DIM:(h + 1) * ML_V_DIM, :].astype(F32)
        y_ref[nb, 0, h * ML_V_DIM:(h + 1) * ML_V_DIM, :] = (hn_t * jax.nn.sigmoid(og_t)).astype(BF16)

        m_end = m_row[:, L - 1:L]
        u_row = jnp.exp(c_row - m_end)
        decay = jnp.exp(m_prev - m_end)
        uv_t = (vth.astype(F32) * u_row).astype(BF16)
        ct_ref[nb, h] = decay * ct_prev + jnp.dot(uv_t, kh, preferred_element_type=F32)
        m_ref[nb, h] = jnp.broadcast_to(b_row[:, L - 1:L] + m_end, (1, LANES))

    prepared = [prepare(nb) for nb in range(ML_NB)]
    for h in range(ML_HEADS):
        for nb in range(ML_NB):
            head(nb, h, *prepared[nb])


def _mlstm(qk, vt, ot, gates, wconv, bias, hnorm):
    B, S, _ = qk.shape
    L = ML_CHUNK
    assert B % ML_NB == 0 and S % TM_PROJ == 0
    per_tile = TM_PROJ // L
    tok = lambda w: pl.BlockSpec((ML_NB, L, w), lambda b, c: (b, c, 0))
    tok_t = lambda r: pl.BlockSpec((ML_NB, r, L), lambda b, c: (b, 0, c))
    return pl.pallas_call(
        _mlstm_kernel,
        grid=(B // ML_NB, S // L),
        in_specs=[tok(qk.shape[2]), tok_t(vt.shape[1]), tok_t(ot.shape[1]), tok_t(gates.shape[1]),
                  _const_spec(wconv.shape), _const_spec(bias.shape), _const_spec(hnorm.shape)],
        out_specs=pl.BlockSpec((ML_NB, 1, ot.shape[1], L), lambda b, c: (b, c // per_tile, 0, c % per_tile)),
        out_shape=jax.ShapeDtypeStruct((B, S // TM_PROJ, ot.shape[1], TM_PROJ), BF16),
        scratch_shapes=[pltpu.VMEM((ML_NB, ML_HEADS, ML_VT_ROWS, ML_QK_PAD), F32),
                        pltpu.VMEM((ML_NB, ML_HEADS, 1, LANES), F32),
                        pltpu.VMEM((ML_NB, 8, qk.shape[2]), F32)],
        compiler_params=_params("arbitrary", "arbitrary"),
        name="mlstm",
    )(qk, vt, ot, gates, wconv, bias, hnorm)


def _block_tail_kernel(x_ref, yt_ref, qmt_ref, km_ref, vmt_ref, wout_ref, g_ref, w1_ref, w2_ref, gf_ref,
                       out_ref, x1_ref, *, final_norm):
    tm = x_ref.shape[1]
    halves = [slice(r0, r0 + tm // 2) for r0 in range(0, tm, tm // 2)]
    scores = [[jnp.dot(km_ref[0, h], qmt_ref[0, :, cols], preferred_element_type=F32)
               for h in range(MEM_HEADS)] for cols in halves]
    for cols, s_heads in zip(halves, scores):
        ymem_t = None
        for h in range(MEM_HEADS):
            s = s_heads[h]
            e = jnp.exp(s - jnp.max(s, axis=0, keepdims=True))
            p = (e * (1.0 / jnp.sum(e, axis=0, keepdims=True))).astype(BF16)
            oh = jnp.dot(vmt_ref[0, h], p, preferred_element_type=F32)
            ymem_t = oh if ymem_t is None else ymem_t + oh
        ycat = jnp.concatenate([yt_ref[0, 0, :, cols], ymem_t.astype(BF16)], axis=0).T
        x1_ref[cols, :] = x_ref[0, cols, :] + jnp.dot(ycat, wout_ref[...], preferred_element_type=F32)

    x = x1_ref[...]
    hn = _rms(x, g_ref[...]).astype(BF16)
    acc = x
    for c in range(D_FF // FF_CHUNK):
        w1c = w1_ref[:, c * FF_CHUNK:(c + 1) * FF_CHUNK].astype(BF16)
        a = jnp.dot(hn, w1c, preferred_element_type=F32)
        a = jnp.square(jnp.maximum(a, 0.0)).astype(BF16)
        w2c = w2_ref[c * FF_CHUNK:(c + 1) * FF_CHUNK, :].astype(BF16)
        acc = acc + jnp.dot(a, w2c, preferred_element_type=F32)
    if final_norm:
        acc = _rms(acc, gf_ref[...])
    out_ref[0] = acc


def _block_tail(x, yt, qmt, km, vmt, wout, g, w1, w2, gf, final_norm):
    B, S, D = x.shape
    tm = yt.shape[3]
    tok = lambda w: pl.BlockSpec((1, tm, w), lambda b, i: (b, i, 0))
    tok_t = lambda r: pl.BlockSpec((1, r, tm), lambda b, i: (b, 0, i))
    per_b = pl.BlockSpec((1,) + km.shape[1:], lambda b, i: (b, 0, 0, 0))
    return pl.pallas_call(
        functools.partial(_block_tail_kernel, final_norm=final_norm),
        grid=(B, S // tm),
        in_specs=[tok(D), pl.BlockSpec((1, 1, yt.shape[2], tm), lambda b, i: (b, i, 0, 0)),
                  tok_t(qmt.shape[1]), per_b, per_b, _single_buffered_spec(wout.shape),
                  _const_spec(g.shape), _single_buffered_spec(w1.shape), _single_buffered_spec(w2.shape),
                  _const_spec(gf.shape)],
        out_specs=tok(D),
        out_shape=jax.ShapeDtypeStruct(x.shape, F32),
        scratch_shapes=[pltpu.VMEM((tm, D), F32)],
        compiler_params=_params("arbitrary", "arbitrary"),
        name="block_tail",
    )(x, yt, qmt, km, vmt, wout, g, w1, w2, gf)


def _rope_kernel(pos_ref, inv_ref, cos_ref, sin_ref):
    ang = pos_ref[0].astype(F32) * inv_ref[...]
    cos_ref[0] = jnp.cos(ang)
    sin_ref[0] = jnp.sin(ang)


def _rope_tables(positions):
    B, S = positions.shape
    nf = MLA_ROPE // 2
    inv = (ROPE_THETA ** (-jnp.arange(0, MLA_ROPE, 2, dtype=F32) / MLA_ROPE)).reshape(nf, 1)
    out = jax.ShapeDtypeStruct((B, nf, S), F32)
    spec = pl.BlockSpec((1, nf, S), lambda b: (b, 0, 0))
    return pl.pallas_call(
        _rope_kernel,
        grid=(B,),
        in_specs=[pl.BlockSpec((1, 1, S), lambda b: (b, 0, 0)), _const_spec(inv.shape)],
        out_specs=[spec, spec],
        out_shape=[out, out],
        compiler_params=_params("arbitrary"),
        name="rope_tables",
    )(positions.reshape(B, 1, S), inv)


def _inproj1_kernel(x_ref, g_ref, win_ref, wqmt_ref, wkrt_ref, gq_ref, gkv_ref, wuqt_ref, wuk_ref, wuvt_ref,
                    cos_ref, sin_ref, qt_ref, k_ref, vt_ref, qm_ref):
    half = MLA_ROPE // 2
    xn = _rms(x_ref[0], g_ref[...]).astype(BF16)
    c = jnp.dot(xn, win_ref[...], preferred_element_type=F32)
    o_kv = MLA_Q_RANK
    qm_ref[0] = lax.dot_general(wqmt_ref[...], xn, _NT, preferred_element_type=F32).astype(BF16)
    cq = _rms(c[:, :o_kv], gq_ref[...]).astype(BF16)
    ckv = _rms(c[:, o_kv:], gkv_ref[...]).astype(BF16)
    cos_t, sin_t = cos_ref[0], sin_ref[0]

    def rope_t(x1, x2):
        return x1 * cos_t - x2 * sin_t, x2 * cos_t + x1 * sin_t

    qscale = ((MLA_NOPE + MLA_ROPE) ** -0.5) * math.log2(math.e)
    qt = lax.dot_general(wuqt_ref[...], cq, _NT, preferred_element_type=F32)
    zpad = jnp.zeros((MLA_QK_PAD - MLA_NOPE - MLA_ROPE, qt.shape[1]), F32)
    for h in range(MLA_HEADS):
        b0 = h * MLA_QK_PAD
        r1, r2 = rope_t(qt[b0 + MLA_NOPE:b0 + MLA_NOPE + half],
                        qt[b0 + MLA_NOPE + half:b0 + MLA_NOPE + MLA_ROPE])
        blk = jnp.concatenate([qt[b0:b0 + MLA_NOPE], r1, r2, zpad], axis=0) * qscale
        qt_ref[0, 0, b0:b0 + MLA_QK_PAD] = blk.astype(BF16)

    krt = lax.dot_general(wkrt_ref[...], xn, _NT, preferred_element_type=F32)
    r1, r2 = rope_t(krt[MLA_NOPE:MLA_NOPE + half], krt[MLA_NOPE + half:MLA_NOPE + MLA_ROPE])
    kr = jnp.concatenate([krt[:MLA_NOPE], r1, r2, krt[MLA_NOPE + MLA_ROPE:]], axis=0).T
    k = jnp.dot(ckv, wuk_ref[...], preferred_element_type=F32)
    for h in range(MLA_HEADS):
        sl = slice(h * MLA_QK_PAD, (h + 1) * MLA_QK_PAD)
        k_ref[0, :, sl] = (k[:, sl] + kr).astype(BF16)

    vt = lax.dot_general(wuvt_ref[...], ckv, _NT, preferred_element_type=F32)
    rowid = lax.broadcasted_iota(jnp.int32, vt.shape, 0) % MLA_VT_ROWS
    vt_ref[0, 0] = jnp.where(rowid == MLA_V, 1.0, vt).astype(BF16)


def _inproj1(x, g, win, wqmt, wkrt, gq, gkv, wuqt, wuk, wuvt, cos_t, sin_t):
    B, S, D = x.shape
    tm = TQ
    assert S % tm == 0
    tok = lambda w: pl.BlockSpec((1, tm, w), lambda b, i: (b, i, 0))
    tok_t = lambda r: pl.BlockSpec((1, r, tm), lambda b, i: (b, 0, i))
    consts = (g, win, wqmt, wkrt, gq, gkv, wuqt, wuk, wuvt)
    nvt = wuvt.shape[0]
    return pl.pallas_call(
        _inproj1_kernel,
        grid=(B, S // tm),
        in_specs=[tok(D)] + [_const_spec(a.shape) for a in consts] + [tok_t(cos_t.shape[1])] * 2,
        out_specs=[pl.BlockSpec((1, 1, wuqt.shape[0], tm), lambda b, i: (b, i, 0, 0)), tok(wuk.shape[1]),
                   pl.BlockSpec((1, 1, nvt, tm), lambda b, i: (b, i, 0, 0)), tok_t(MEM_WIDTH)],
        out_shape=[jax.ShapeDtypeStruct((B, S // tm, wuqt.shape[0], tm), BF16),
                   jax.ShapeDtypeStruct((B, S, wuk.shape[1]), BF16),
                   jax.ShapeDtypeStruct((B, S // tm, nvt, tm), BF16),
                   jax.ShapeDtypeStruct((B, MEM_WIDTH, S), BF16)],
        compiler_params=_params("arbitrary", "arbitrary"),
        name="inproj1",
    )(x, *consts, cos_t, sin_t)


def _flash_kernel(qt_ref, k_ref, vt_ref, y_ref, sa_ref, sb_ref, smax_ref, acc_ref, m_ref):
    slots = (sa_ref, sb_ref)
    npairs = qt_ref.shape[1] // 2
    key_i = lax.broadcasted_iota(jnp.int32, (TQ, TQ), 0)
    qry_i = lax.broadcasted_iota(jnp.int32, (TQ, TQ), 1)
    half = TQ // 2

    def col_max(s):
        parts = [jnp.max(s[r:r + 128], axis=0, keepdims=True) for r in range(0, TQ, 128)]
        return jnp.maximum(jnp.maximum(parts[0], parts[1]), jnp.maximum(parts[2], parts[3]))

    def qk_stage(tile, slot, blk, diagonal=False):
        for hh in range(2):
            kh = k_ref[0, pl.ds(pl.multiple_of(blk * TQ, TQ), TQ), hh * MLA_QK_PAD:(hh + 1) * MLA_QK_PAD]
            qh = qt_ref[0, tile, hh * MLA_QK_PAD:(hh + 1) * MLA_QK_PAD, :]
            if diagonal:
                top = jnp.dot(kh[:half], qh, preferred_element_type=F32)
                bot = jnp.dot(kh[half:], qh[:, half:], preferred_element_type=F32)
                s = jnp.concatenate([top, jnp.concatenate([jnp.zeros((half, half), F32), bot], axis=1)], axis=0)
                s = jnp.where(key_i <= qry_i, s, -jnp.inf)
            else:
                s = jnp.dot(kh, qh, preferred_element_type=F32)
            slots[slot][hh] = s
            smax_ref[slot, hh] = col_max(s)

    def pv_stage(qt, slot, blk, diagonal=False):
        for hh in range(2):
            m_prev = m_ref[qt, hh]
            m_new = jnp.maximum(m_prev, smax_ref[slot, hh])
            alpha = jnp.exp2(m_prev - m_new)
            p = jnp.exp2(slots[slot][hh] - m_new).astype(BF16)
            vth = vt_ref[0, blk, hh * MLA_VT_ROWS:(hh + 1) * MLA_VT_ROWS, :]
            if diagonal:
                pv = jnp.concatenate([jnp.dot(vth[:, :half], p[:half, :half], preferred_element_type=F32),
                                      jnp.dot(vth, p[:, half:], preferred_element_type=F32)], axis=1)
            else:
                pv = jnp.dot(vth, p, preferred_element_type=F32)
            acc_ref[qt, hh] = alpha * acc_ref[qt, hh] + pv
            m_ref[qt, hh] = m_new

    def finalize(qt, tile):
        outs = []
        for hh in range(2):
            acc = acc_ref[qt, hh]
            outs.append(acc[:MLA_V] * (1.0 / acc[MLA_V:MLA_V + 1]))
        y_ref[0, tile] = jnp.concatenate(outs, axis=0).astype(BF16)

    def tile_pair(j, parity, has_next):
        ia = 2 * j
        ib = 2 * j + 1
        s0, s1 = parity, 1 - parity
        m_ref[...] = jnp.full(m_ref.shape, -jnp.inf, F32)
        acc_ref[...] = jnp.zeros_like(acc_ref)

        def body_a(t, carry):
            qk_stage(ia, s1, 2 * t + 1)
            pv_stage(0, s0, 2 * t)
            qk_stage(ia, s0, 2 * t + 2)
            pv_stage(0, s1, 2 * t + 1)
            return carry

        lax.fori_loop(0, j - 1, body_a, 0)

        @pl.when(j >= 1)
        def _():
            qk_stage(ia, s1, ia - 1)
            pv_stage(0, s0, ia - 2)
            qk_stage(ia, s0, ia, diagonal=True)
            pv_stage(0, s1, ia - 1)

        qk_stage(ib, s1, 0)
        pv_stage(0, s0, ia, diagonal=True)
        finalize(0, ia)

        def body_b(t, carry):
            qk_stage(ib, s0, 2 * t + 1)
            pv_stage(1, s1, 2 * t)
            qk_stage(ib, s1, 2 * t + 2)
            pv_stage(1, s0, 2 * t + 1)
            return carry

        lax.fori_loop(0, j, body_b, 0)

        qk_stage(ib, s0, ib, diagonal=True)
        pv_stage(1, s1, ib - 1)

        if has_next is None:
            qk_stage(ia + 2, s1, 0)
            pv_stage(1, s0, ib, diagonal=True)
        else:
            @pl.when(has_next)
            def _():
                qk_stage(ia + 2, s1, 0)
                pv_stage(1, s0, ib, diagonal=True)

            @pl.when(jnp.logical_not(has_next))
            def _():
                pv_stage(1, s0, ib, diagonal=True)
        finalize(1, ib)

    qk_stage(0, 0, 0, diagonal=True)

    def two_pairs(t, carry):
        tile_pair(2 * t, 0, None)
        tile_pair(2 * t + 1, 1, 2 * t + 2 < npairs)
        return carry

    lax.fori_loop(0, npairs // 2, two_pairs, 0)


def _flash(qt, k, vt):
    B, S, _ = k.shape
    assert S % (4 * TQ) == 0
    nq = S // TQ
    pairs = MLA_HEADS // 2
    return pl.pallas_call(
        _flash_kernel,
        grid=(B, pairs),
        in_specs=[pl.BlockSpec((1, nq, 2 * MLA_QK_PAD, TQ), lambda b, p: (b, 0, p, 0)),
                  pl.BlockSpec((1, S, 2 * MLA_QK_PAD), lambda b, p: (b, 0, p)),
                  pl.BlockSpec((1, nq, 2 * MLA_VT_ROWS, TQ), lambda b, p: (b, 0, p, 0))],
        out_specs=pl.BlockSpec((1, nq, 2 * MLA_V, TQ), lambda b, p: (b, 0, p, 0)),
        out_shape=jax.ShapeDtypeStruct((B, nq, MLA_HEADS * MLA_V, TQ), BF16),
        scratch_shapes=[pltpu.VMEM((2, TQ, TQ), F32),
                        pltpu.VMEM((2, TQ, TQ), F32),
                        pltpu.VMEM((2, 2, 1, TQ), F32),
                        pltpu.VMEM((2, 2, MLA_VT_ROWS, TQ), F32),
                        pltpu.VMEM((2, 2, 1, TQ), F32)],
        compiler_params=_params("arbitrary", "arbitrary"),
        name="mla_flash",
    )(qt, k, vt)


def _pad_heads(w, heads, d, dp):
    lead = w.shape[:-1]
    w = w.reshape(lead + (heads, d))
    w = jnp.pad(w, [(0, 0)] * len(lead) + [(0, 0), (0, dp - d)])
    return w.reshape(lead + (heads * dp,))


def kernel(x, mem, positions, mem_norm, w_mem_kv, norm_mix0, w_in0, b_igate0, b_fgate0, w_conv0, w_hnorm0, w_out0, norm_ffn0, w_ff1_0, w_ff2_0, norm_mix1, w_in1, w_qnorm1, w_uq1, w_kvnorm1, w_ukv1, w_out1, norm_ffn1, w_ff1_1, w_ff2_1, final_norm):
    row = lambda g: g.reshape(1, -1).astype(F32)

    km, vmt = _mem_kv(mem, row(mem_norm), w_mem_kv[:, :MEM_WIDTH].astype(BF16),
                      w_mem_kv[:, MEM_WIDTH:].T.astype(BF16))

    nq = ML_HEADS * ML_QK_DIM
    o_v, o_o, o_g = 2 * nq, 2 * nq + MIX_WIDTH, 2 * nq + 2 * MIX_WIDTH
    o_qm = o_g + 2 * ML_HEADS
    wqk = jnp.concatenate([_pad_heads(w_in0[:, :nq], ML_HEADS, ML_QK_DIM, ML_QK_PAD),
                           _pad_heads(w_in0[:, nq:o_v], ML_HEADS, ML_QK_DIM, ML_QK_PAD)], axis=1)
    wvgt0 = jnp.concatenate([_pad_heads(w_in0[:, o_v:o_o], ML_HEADS, ML_V_DIM, ML_VT_ROWS),
                             jnp.pad(w_in0[:, o_g:o_qm], ((0, 0), (0, ML_GATE_ROWS - 2 * ML_HEADS)))], axis=1).T
    wot0 = w_in0[:, o_o:o_g].T
    wqmt0 = w_in0[:, o_qm:].T
    wconv = jnp.concatenate([_pad_heads(w_conv0[:, :nq], ML_HEADS, ML_QK_DIM, ML_QK_PAD),
                             _pad_heads(w_conv0[:, nq:], ML_HEADS, ML_QK_DIM, ML_QK_PAD)], axis=1).astype(F32)
    gate_bias = jnp.pad(jnp.concatenate([b_igate0, b_fgate0]), (0, ML_GATE_ROWS - 2 * ML_HEADS)).astype(F32)
    gate_bias = jnp.broadcast_to(gate_bias[:, None], (ML_GATE_ROWS, LANES))
    hnorm = jnp.broadcast_to(w_hnorm0.astype(F32)[:, :, None], (ML_HEADS, ML_V_DIM, LANES))

    qk, vt0, ot0, qm0, gates = _inproj0(x, row(norm_mix0), wqk.astype(BF16), wvgt0.astype(BF16),
                                        wot0.astype(BF16), wqmt0.astype(BF16))
    yt0 = _mlstm(qk, vt0, ot0, gates, wconv, gate_bias, hnorm)
    x = _block_tail(x, yt0, qm0, km, vmt, w_out0.astype(BF16), row(norm_ffn0), w_ff1_0, w_ff2_0,
                    row(final_norm), False)

    o_kr = MLA_Q_RANK + MLA_KV_RANK
    o_qm1 = o_kr + MLA_ROPE
    win1 = w_in1[:, :o_kr]
    wqmt1 = w_in1[:, o_qm1:].T
    wkrt = jnp.pad(w_in1[:, o_kr:o_qm1].T, ((MLA_NOPE, LANES - MLA_NOPE - MLA_ROPE), (0, 0)))
    wuqt = _pad_heads(w_uq1, MLA_HEADS, MLA_NOPE + MLA_ROPE, MLA_QK_PAD).T
    ukv = w_ukv1.reshape(MLA_KV_RANK, MLA_HEADS, MLA_NOPE + MLA_V)
    wuk = _pad_heads(ukv[:, :, :MLA_NOPE].reshape(MLA_KV_RANK, -1), MLA_HEADS, MLA_NOPE, MLA_QK_PAD)
    wuvt = _pad_heads(ukv[:, :, MLA_NOPE:].reshape(MLA_KV_RANK, -1), MLA_HEADS, MLA_V, MLA_VT_ROWS).T

    cos_t, sin_t = _rope_tables(positions)
    qt1, k1, vt1, qm1 = _inproj1(x, row(norm_mix1), win1.astype(BF16), wqmt1.astype(BF16), wkrt.astype(BF16),
                                 row(w_qnorm1), row(w_kvnorm1), wuqt.astype(BF16), wuk.astype(BF16),
                                 wuvt.astype(BF16), cos_t, sin_t)
    yt1 = _flash(qt1, k1, vt1)
    return _block_tail(x, yt1, qm1, km, vmt, w_out1.astype(BF16), row(norm_ffn1), w_ff1_1, w_ff2_1,
                       row(final_norm), True)
```

```python
import functools
import math

import jax
import jax.numpy as jnp
from jax import lax
from jax.experimental import pallas as pl
from jax.experimental.pallas import tpu as pltpu

F32 = jnp.float32
BF16 = jnp.bfloat16
EPS = 1e-6

D_MODEL = 1024
N_MEM = 256
MEM_HEADS = 4
MEM_HEAD_DIM = 64
MEM_WIDTH = MEM_HEADS * MEM_HEAD_DIM
MIX_WIDTH = D_MODEL - MEM_WIDTH

ML_HEADS = 4
ML_V_DIM = MIX_WIDTH // ML_HEADS
ML_QK_DIM = ML_V_DIM // 2
ML_CONV = 4
ML_CHUNK = 128
ML_QK_PAD = 128
ML_VT_ROWS = 256
ML_ONES_ROW = ML_V_DIM
ML_NB = 8
ML_GATE_ROWS = 16

MLA_HEADS = 12
MLA_NOPE = 64
MLA_ROPE = 32
MLA_V = 64
MLA_Q_RANK = 384
MLA_KV_RANK = 256
MLA_QK_PAD = 128
MLA_VT_ROWS = 80
ROPE_THETA = 10000.0
D_FF = 4 * D_MODEL

LANES = 128
VMEM_LIMIT = 56 * 1024 * 1024

TM_PROJ = 512
FF_CHUNK = 1024
TQ = 512

_NT = (((1,), (1,)), ((), ()))


def _params(*sem):
    return pltpu.CompilerParams(dimension_semantics=sem, vmem_limit_bytes=VMEM_LIMIT)


def _rms(x, g):
    return x * lax.rsqrt(jnp.mean(x * x, axis=-1, keepdims=True) + EPS) * g


def _const_spec(shape):
    nd = len(shape)
    return pl.BlockSpec(shape, lambda *_: (0,) * nd)


def _single_buffered_spec(shape):
    nd = len(shape)
    return pl.BlockSpec(shape, lambda *_: (0,) * nd, pipeline_mode=pl.Buffered(1))


def _mem_kv_kernel(mem_ref, g_ref, wk_ref, wvt_ref, k_ref, vt_ref):
    xn = _rms(mem_ref[0], g_ref[...]).astype(BF16)
    k = jnp.dot(xn, wk_ref[...], preferred_element_type=F32)
    vt = lax.dot_general(wvt_ref[...], xn, _NT, preferred_element_type=F32)
    k = k * (MEM_HEAD_DIM ** -0.5)
    col_head = lax.broadcasted_iota(jnp.int32, k.shape, 1) // MEM_HEAD_DIM
    row_head = lax.broadcasted_iota(jnp.int32, vt.shape, 0) // MEM_HEAD_DIM
    for h in range(MEM_HEADS):
        k_ref[0, h] = jnp.where(col_head == h, k, 0.0).astype(BF16)
        vt_ref[0, h] = jnp.where(row_head == h, vt, 0.0).astype(BF16)


def _mem_kv(mem, g, wk, wvt):
    B = mem.shape[0]
    out = jax.ShapeDtypeStruct((B, MEM_HEADS, MEM_WIDTH, N_MEM), BF16)
    return pl.pallas_call(
        _mem_kv_kernel,
        grid=(B,),
        in_specs=[pl.BlockSpec((1, N_MEM, D_MODEL), lambda b: (b, 0, 0)),
                  _const_spec(g.shape), _const_spec(wk.shape), _const_spec(wvt.shape)],
        out_specs=[pl.BlockSpec((1, MEM_HEADS, N_MEM, MEM_WIDTH), lambda b: (b, 0, 0, 0)),
                   pl.BlockSpec((1, MEM_HEADS, MEM_WIDTH, N_MEM), lambda b: (b, 0, 0, 0))],
        out_shape=[out, out],
        compiler_params=_params("arbitrary"),
        name="mem_kv",
    )(mem, g, wk, wvt)


def _inproj0_kernel(x_ref, g_ref, wqk_ref, wv_ref, wo_ref, wqm_ref,
                    qk_ref, v_ref, o_ref, qm_ref, gate_ref):
    xn = _rms(x_ref[0], g_ref[...]).astype(BF16)
    qk_ref[0] = jnp.dot(xn, wqk_ref[...], preferred_element_type=F32).astype(BF16)
    vg = lax.dot_general(wv_ref[...], xn, _NT, preferred_element_type=F32)
    nv = v_ref.shape[1]
    vt = vg[:nv]
    rowid = lax.broadcasted_iota(jnp.int32, vt.shape, 0) % ML_VT_ROWS
    v_ref[0] = jnp.where(rowid == ML_ONES_ROW, 1.0, vt).astype(BF16)
    gate_ref[0] = vg[nv:]
    o_ref[0] = lax.dot_general(wo_ref[...], xn, _NT, preferred_element_type=F32).astype(BF16)
    qm_ref[0] = lax.dot_general(wqm_ref[...], xn, _NT, preferred_element_type=F32).astype(BF16)


def _inproj0(x, g, wqk, wvgt, wot, wqmt):
    B, S, D = x.shape
    tm = min(TM_PROJ, S)
    tok = lambda w: pl.BlockSpec((1, tm, w), lambda b, i: (b, i, 0))
    tok_t = lambda r: pl.BlockSpec((1, r, tm), lambda b, i: (b, 0, i))
    consts = (g, wqk, wvgt, wot, wqmt)
    nv = wvgt.shape[0] - ML_GATE_ROWS
    return pl.pallas_call(
        _inproj0_kernel,
        grid=(B, S // tm),
        in_specs=[tok(D)] + [_const_spec(a.shape) for a in consts],
        out_specs=[tok(wqk.shape[1]), tok_t(nv), tok_t(wot.shape[0]), tok_t(wqmt.shape[0]),
                   tok_t(ML_GATE_ROWS)],
        out_shape=[jax.ShapeDtypeStruct((B, S, wqk.shape[1]), BF16),
                   jax.ShapeDtypeStruct((B, nv, S), BF16),
                   jax.ShapeDtypeStruct((B, wot.shape[0], S), BF16),
                   jax.ShapeDtypeStruct((B, wqmt.shape[0], S), BF16),
                   jax.ShapeDtypeStruct((B, ML_GATE_ROWS, S), F32)],
        compiler_params=_params("arbitrary", "arbitrary"),
        name="inproj0",
    )(x, *consts)


def _split3(x):
    hi = x.astype(BF16)
    r = x - hi.astype(F32)
    mid = r.astype(BF16)
    lo = (r - mid.astype(F32)).astype(BF16)
    return hi, mid, lo


def _mlstm_kernel(qk_ref, vt_ref, o_ref, gate_ref, wconv_ref, bias_ref, hnorm_ref,
                  y_ref, ct_ref, m_ref, tail_ref):
    L = ML_CHUNK
    HQ = ML_HEADS * ML_QK_PAD

    @pl.when(pl.program_id(1) == 0)
    def _():
        ct_ref[...] = jnp.zeros_like(ct_ref)
        m_ref[...] = jnp.zeros_like(m_ref)
        tail_ref[...] = jnp.zeros_like(tail_ref)

    key_i = lax.broadcasted_iota(jnp.int32, (L, L), 0)
    qry_i = lax.broadcasted_iota(jnp.int32, (L, L), 1)
    causal_t = key_i <= qry_i
    triu = jnp.where(causal_t, 1.0, 0.0).astype(BF16)
    wc = wconv_ref[...]

    def prepare(nb):
        x = qk_ref[nb].astype(F32)
        tail = tail_ref[nb]
        row8 = lax.broadcasted_iota(jnp.int32, tail.shape, 0)
        conv = x * wc[ML_CONV - 1:ML_CONV]
        for s in range(1, ML_CONV):
            xs = pltpu.roll(x, s, axis=0)
            head = jnp.where(row8 < s, pltpu.roll(tail, s, axis=0), xs[0:8])
            xs = jnp.concatenate([head, xs[8:]], axis=0)
            conv = conv + xs * wc[ML_CONV - 1 - s:ML_CONV - s]
        tail_ref[nb] = x[L - 8:L]
        act = conv * jax.nn.sigmoid(conv)
        q = act[:, :HQ].astype(BF16)
        k = (act[:, HQ:] * (ML_QK_DIM ** -0.5)).astype(BF16)

        g_rows = gate_ref[nb] + bias_ref[...]
        log_f = jnp.minimum(g_rows, 0.0) - jnp.log1p(jnp.exp(-jnp.abs(g_rows)))
        b_rows = sum(jnp.dot(part, triu, preferred_element_type=F32) for part in _split3(log_f))
        c_rows = g_rows[:ML_HEADS] - b_rows[ML_HEADS:2 * ML_HEADS]
        c_cols = jnp.concatenate([c_rows, jnp.zeros((L - ML_HEADS, L), F32)], axis=0).T
        return q, k, b_rows, c_rows, c_cols

    def head(nb, h, q, k, b_rows, c_rows, c_cols):
        b_row = b_rows[ML_HEADS + h:ML_HEADS + h + 1, :]
        c_row = c_rows[h:h + 1, :]
        c_col = c_cols[:, h:h + 1]
        m_prev = m_ref[nb, h][:, 0:1]
        qh = q[:, h * ML_QK_PAD:(h + 1) * ML_QK_PAD]
        kh = k[:, h * ML_QK_PAD:(h + 1) * ML_QK_PAD]
        vth = vt_ref[nb, h * ML_VT_ROWS:(h + 1) * ML_VT_ROWS, :]
        ct_prev = ct_ref[nb, h]

        cm = jnp.where(causal_t, c_col, -jnp.inf)
        m_row = jnp.maximum(m_prev, jnp.max(cm, axis=0, keepdims=True))
        w_t = jnp.exp(cm - m_row)
        a_row = jnp.exp(m_prev - m_row)
        s_t = (lax.dot_general(kh, qh, _NT, preferred_element_type=F32) * w_t).astype(BF16)
        num_t = (jnp.dot(vth, s_t, preferred_element_type=F32)
                 + a_row * lax.dot_general(ct_prev.astype(BF16), qh, _NT, preferred_element_type=F32))
        den = num_t[ML_ONES_ROW:ML_ONES_ROW + 1]
        inv = 1.0 / jnp.maximum(jnp.abs(den), jnp.exp(-(b_row + m_row)))
        hc_t = num_t[:ML_V_DIM] * inv
        ms = jnp.sum(hc_t * hc_t, axis=0, keepdims=True) * (1.0 / ML_V_DIM)
        hn_t = hc_t * lax.rsqrt(ms + EPS) * hnorm_ref[h]
        og_t = o_ref[nb, h * ML_V_DIM:(h + 1) * ML_V_DIM, :].astype(F32)
        y_ref[nb, 0, h * ML_V_DIM:(h + 1) * ML_V_DIM, :] = (hn_t * jax.nn.sigmoid(og_t)).astype(BF16)

        m_end = m_row[:, L - 1:L]
        u_row = jnp.exp(c_row - m_end)
        decay = jnp.exp(m_prev - m_end)
        uv_t = (vth.astype(F32) * u_row).astype(BF16)
        ct_ref[nb, h] = decay * ct_prev + jnp.dot(uv_t, kh, preferred_element_type=F32)
        m_ref[nb, h] = jnp.broadcast_to(b_row[:, L - 1:L] + m_end, (1, LANES))

    prepared = [prepare(nb) for nb in range(ML_NB)]
    for h in range(ML_HEADS):
        for nb in range(ML_NB):
            head(nb, h, *prepared[nb])


def _mlstm(qk, vt, ot, gates, wconv, bias, hnorm):
    B, S, _ = qk.shape
    L = ML_CHUNK
    assert B % ML_NB == 0 and S % TM_PROJ == 0
    per_tile = TM_PROJ // L
    tok = lambda w: pl.BlockSpec((ML_NB, L, w), lambda b, c: (b, c, 0))
    tok_t = lambda r: pl.BlockSpec((ML_NB, r, L), lambda b, c: (b, 0, c))
    return pl.pallas_call(
        _mlstm_kernel,
        grid=(B // ML_NB, S // L),
        in_specs=[tok(qk.shape[2]), tok_t(vt.shape[1]), tok_t(ot.shape[1]), tok_t(gates.shape[1]),
                  _const_spec(wconv.shape), _const_spec(bias.shape), _const_spec(hnorm.shape)],
        out_specs=pl.BlockSpec((ML_NB, 1, ot.shape[1], L), lambda b, c: (b, c // per_tile, 0, c % per_tile)),
        out_shape=jax.ShapeDtypeStruct((B, S // TM_PROJ, ot.shape[1], TM_PROJ), BF16),
        scratch_shapes=[pltpu.VMEM((ML_NB, ML_HEADS, ML_VT_ROWS, ML_QK_PAD), F32),
                        pltpu.VMEM((ML_NB, ML_HEADS, 1, LANES), F32),
                        pltpu.VMEM((ML_NB, 8, qk.shape[2]), F32)],
        compiler_params=_params("arbitrary", "arbitrary"),
        name="mlstm",
    )(qk, vt, ot, gates, wconv, bias, hnorm)


def _block_tail_kernel(x_ref, yt_ref, qmt_ref, km_ref, vmt_ref, wout_ref, g_ref, w1_ref, w2_ref, gf_ref,
                       out_ref, x1_ref, *, final_norm):
    tm = x_ref.shape[1]
    halves = [slice(r0, r0 + tm // 2) for r0 in range(0, tm, tm // 2)]
    scores = [[jnp.dot(km_ref[0, h], qmt_ref[0, :, cols], preferred_element_type=F32)
               for h in range(MEM_HEADS)] for cols in halves]
    for cols, s_heads in zip(halves, scores):
        ymem_t = None
        for h in range(MEM_HEADS):
            s = s_heads[h]
            e = jnp.exp(s - jnp.max(s, axis=0, keepdims=True))
            p = (e * (1.0 / jnp.sum(e, axis=0, keepdims=True))).astype(BF16)
            oh = jnp.dot(vmt_ref[0, h], p, preferred_element_type=F32)
            ymem_t = oh if ymem_t is None else ymem_t + oh
        ycat = jnp.concatenate([yt_ref[0, 0, :, cols], ymem_t.astype(BF16)], axis=0).T
        x1_ref[cols, :] = x_ref[0, cols, :] + jnp.dot(ycat, wout_ref[...], preferred_element_type=F32)

    x = x1_ref[...]
    hn = _rms(x, g_ref[...]).astype(BF16)
    acc = x
    for c in range(D_FF // FF_CHUNK):
        w1c = w1_ref[:, c * FF_CHUNK:(c + 1) * FF_CHUNK].astype(BF16)
        a = jnp.dot(hn, w1c, preferred_element_type=F32)
        a = jnp.square(jnp.maximum(a, 0.0)).astype(BF16)
        w2c = w2_ref[c * FF_CHUNK:(c + 1) * FF_CHUNK, :].astype(BF16)
        acc = acc + jnp.dot(a, w2c, preferred_element_type=F32)
    if final_norm:
        acc = _rms(acc, gf_ref[...])
    out_ref[0] = acc


def _block_tail(x, yt, qmt, km, vmt, wout, g, w1, w2, gf, final_norm):
    B, S, D = x.shape
    tm = yt.shape[3]
    tok = lambda w: pl.BlockSpec((1, tm, w), lambda b, i: (b, i, 0))
    tok_t = lambda r: pl.BlockSpec((1, r, tm), lambda b, i: (b, 0, i))
    per_b = pl.BlockSpec((1,) + km.shape[1:], lambda b, i: (b, 0, 0, 0))
    return pl.pallas_call(
        functools.partial(_block_tail_kernel, final_norm=final_norm),
        grid=(B, S // tm),
        in_specs=[tok(D), pl.BlockSpec((1, 1, yt.shape[2], tm), lambda b, i: (b, i, 0, 0)),
                  tok_t(qmt.shape[1]), per_b, per_b, _single_buffered_spec(wout.shape),
                  _const_spec(g.shape), _single_buffered_spec(w1.shape), _single_buffered_spec(w2.shape),
                  _const_spec(gf.shape)],
        out_specs=tok(D),
        out_shape=jax.ShapeDtypeStruct(x.shape, F32),
        scratch_shapes=[pltpu.VMEM((tm, D), F32)],
        compiler_params=_params("arbitrary", "arbitrary"),
        name="block_tail",
    )(x, yt, qmt, km, vmt, wout, g, w1, w2, gf)


def _rope_kernel(pos_ref, inv_ref, cos_ref, sin_ref):
    ang = pos_ref[0].astype(F32) * inv_ref[...]
    cos_ref[0] = jnp.cos(ang)
    sin_ref[0] = jnp.sin(ang)


def _rope_tables(positions):
    B, S = positions.shape
    nf = MLA_ROPE // 2
    inv = (ROPE_THETA ** (-jnp.arange(0, MLA_ROPE, 2, dtype=F32) / MLA_ROPE)).reshape(nf, 1)
    out = jax.ShapeDtypeStruct((B, nf, S), F32)
    spec = pl.BlockSpec((1, nf, S), lambda b: (b, 0, 0))
    return pl.pallas_call(
        _rope_kernel,
        grid=(B,),
        in_specs=[pl.BlockSpec((1, 1, S), lambda b: (b, 0, 0)), _const_spec(inv.shape)],
        out_specs=[spec, spec],
        out_shape=[out, out],
        compiler_params=_params("arbitrary"),
        name="rope_tables",
    )(positions.reshape(B, 1, S), inv)


def _inproj1_kernel(x_ref, g_ref, win_ref, wqmt_ref, wkrt_ref, gq_ref, gkv_ref, wuqt_ref, wuk_ref, wuvt_ref,
                    cos_ref, sin_ref, qt_ref, k_ref, vt_ref, qm_ref):
    half = MLA_ROPE // 2
    xn = _rms(x_ref[0], g_ref[...]).astype(BF16)
    c = jnp.dot(xn, win_ref[...], preferred_element_type=F32)
    o_kv = MLA_Q_RANK
    qm_ref[0] = lax.dot_general(wqmt_ref[...], xn, _NT, preferred_element_type=F32).astype(BF16)
    cq = _rms(c[:, :o_kv], gq_ref[...]).astype(BF16)
    ckv = _rms(c[:, o_kv:], gkv_ref[...]).astype(BF16)
    cos_t, sin_t = cos_ref[0], sin_ref[0]

    def rope_t(x1, x2):
        return x1 * cos_t - x2 * sin_t, x2 * cos_t + x1 * sin_t

    qscale = ((MLA_NOPE + MLA_ROPE) ** -0.5) * math.log2(math.e)
    qt = lax.dot_general(wuqt_ref[...], cq, _NT, preferred_element_type=F32)
    zpad = jnp.zeros((MLA_QK_PAD - MLA_NOPE - MLA_ROPE, qt.shape[1]), F32)
    for h in range(MLA_HEADS):
        b0 = h * MLA_QK_PAD
        r1, r2 = rope_t(qt[b0 + MLA_NOPE:b0 + MLA_NOPE + half],
                        qt[b0 + MLA_NOPE + half:b0 + MLA_NOPE + MLA_ROPE])
        blk = jnp.concatenate([qt[b0:b0 + MLA_NOPE], r1, r2, zpad], axis=0) * qscale
        qt_ref[0, 0, b0:b0 + MLA_QK_PAD] = blk.astype(BF16)

    krt = lax.dot_general(wkrt_ref[...], xn, _NT, preferred_element_type=F32)
    r1, r2 = rope_t(krt[MLA_NOPE:MLA_NOPE + half], krt[MLA_NOPE + half:MLA_NOPE + MLA_ROPE])
    kr = jnp.concatenate([krt[:MLA_NOPE], r1, r2, krt[MLA_NOPE + MLA_ROPE:]], axis=0).T
    k = jnp.dot(ckv, wuk_ref[...], preferred_element_type=F32)
    for h in range(MLA_HEADS):
        sl = slice(h * MLA_QK_PAD, (h + 1) * MLA_QK_PAD)
        k_ref[0, :, sl] = (k[:, sl] + kr).astype(BF16)

    vt = lax.dot_general(wuvt_ref[...], ckv, _NT, preferred_element_type=F32)
    rowid = lax.broadcasted_iota(jnp.int32, vt.shape, 0) % MLA_VT_ROWS
    vt_ref[0, 0] = jnp.where(rowid == MLA_V, 1.0, vt).astype(BF16)


def _inproj1(x, g, win, wqmt, wkrt, gq, gkv, wuqt, wuk, wuvt, cos_t, sin_t):
    B, S, D = x.shape
    tm = TQ
    assert S % tm == 0
    tok = lambda w: pl.BlockSpec((1, tm, w), lambda b, i: (b, i, 0))
    tok_t = lambda r: pl.BlockSpec((1, r, tm), lambda b, i: (b, 0, i))
    consts = (g, win, wqmt, wkrt, gq, gkv, wuqt, wuk, wuvt)
    nvt = wuvt.shape[0]
    return pl.pallas_call(
        _inproj1_kernel,
        grid=(B, S // tm),
        in_specs=[tok(D)] + [_const_spec(a.shape) for a in consts] + [tok_t(cos_t.shape[1])] * 2,
        out_specs=[pl.BlockSpec((1, 1, wuqt.shape[0], tm), lambda b, i: (b, i, 0, 0)), tok(wuk.shape[1]),
                   pl.BlockSpec((1, 1, nvt, tm), lambda b, i: (b, i, 0, 0)), tok_t(MEM_WIDTH)],
        out_shape=[jax.ShapeDtypeStruct((B, S // tm, wuqt.shape[0], tm), BF16),
                   jax.ShapeDtypeStruct((B, S, wuk.shape[1]), BF16),
                   jax.ShapeDtypeStruct((B, S // tm, nvt, tm), BF16),
                   jax.ShapeDtypeStruct((B, MEM_WIDTH, S), BF16)],
        compiler_params=_params("arbitrary", "arbitrary"),
        name="inproj1",
    )(x, *consts, cos_t, sin_t)


def _flash_kernel(qt_ref, k_ref, vt_ref, y_ref, sa_ref, sb_ref, smax_ref, acc_ref, m_ref):
    slots = (sa_ref, sb_ref)
    npairs = qt_ref.shape[1] // 2
    key_i = lax.broadcasted_iota(jnp.int32, (TQ, TQ), 0)
    qry_i = lax.broadcasted_iota(jnp.int32, (TQ, TQ), 1)
    half = TQ // 2

    def col_max(s):
        parts = [jnp.max(s[r:r + 128], axis=0, keepdims=True) for r in range(0, TQ, 128)]
        return jnp.maximum(jnp.maximum(parts[0], parts[1]), jnp.maximum(parts[2], parts[3]))

    def qk_stage(tile, slot, blk, diagonal=False):
        for hh in range(2):
            kh = k_ref[0, pl.ds(pl.multiple_of(blk * TQ, TQ), TQ), hh * MLA_QK_PAD:(hh + 1) * MLA_QK_PAD]
            qh = qt_ref[0, tile, hh * MLA_QK_PAD:(hh + 1) * MLA_QK_PAD, :]
            if diagonal:
                top = jnp.dot(kh[:half], qh, preferred_element_type=F32)
                bot = jnp.dot(kh[half:], qh[:, half:], preferred_element_type=F32)
                s = jnp.concatenate([top, jnp.concatenate([jnp.zeros((half, half), F32), bot], axis=1)], axis=0)
                s = jnp.where(key_i <= qry_i, s, -jnp.inf)
            else:
                s = jnp.dot(kh, qh, preferred_element_type=F32)
            slots[slot][hh] = s
            smax_ref[slot, hh] = col_max(s)

    def pv_stage(qt, slot, blk, diagonal=False):
        for hh in range(2):
            m_prev = m_ref[qt, hh]
            m_new = jnp.maximum(m_prev, smax_ref[slot, hh])
            alpha = jnp.exp2(m_prev - m_new)
            p = jnp.exp2(slots[slot][hh] - m_new).astype(BF16)
            vth = vt_ref[0, blk, hh * MLA_VT_ROWS:(hh + 1) * MLA_VT_ROWS, :]
            if diagonal:
                pv = jnp.concatenate([jnp.dot(vth[:, :half], p[:half, :half], preferred_element_type=F32),
                                      jnp.dot(vth, p[:, half:], preferred_element_type=F32)], axis=1)
            else:
                pv = jnp.dot(vth, p, preferred_element_type=F32)
            acc_ref[qt, hh] = alpha * acc_ref[qt, hh] + pv
            m_ref[qt, hh] = m_new

    def finalize(qt, tile):
        outs = []
        for hh in range(2):
            acc = acc_ref[qt, hh]
            outs.append(acc[:MLA_V] * (1.0 / acc[MLA_V:MLA_V + 1]))
        y_ref[0, tile] = jnp.concatenate(outs, axis=0).astype(BF16)

    def tile_pair(j, parity, has_next):
        ia = 2 * j
        ib = 2 * j + 1
        s0, s1 = parity, 1 - parity
        m_ref[...] = jnp.full(m_ref.shape, -jnp.inf, F32)
        acc_ref[...] = jnp.zeros_like(acc_ref)

        def body_a(t, carry):
            qk_stage(ia, s1, 2 * t + 1)
            pv_stage(0, s0, 2 * t)
            qk_stage(ia, s0, 2 * t + 2)
            pv_stage(0, s1, 2 * t + 1)
            return carry

        lax.fori_loop(0, j - 1, body_a, 0)

        @pl.when(j >= 1)
        def _():
            qk_stage(ia, s1, ia - 1)
            pv_stage(0, s0, ia - 2)
            qk_stage(ia, s0, ia, diagonal=True)
            pv_stage(0, s1, ia - 1)

        qk_stage(ib, s1, 0)
        pv_stage(0, s0, ia, diagonal=True)
        finalize(0, ia)

        def body_b(t, carry):
            qk_stage(ib, s0, 2 * t + 1)
            pv_stage(1, s1, 2 * t)
            qk_stage(ib, s1, 2 * t + 2)
            pv_stage(1, s0, 2 * t + 1)
            return carry

        lax.fori_loop(0, j, body_b, 0)

        qk_stage(ib, s0, ib, diagonal=True)
        pv_stage(1, s1, ib - 1)

        if has_next is None:
            qk_stage(ia + 2, s1, 0)
            pv_stage(1, s0, ib, diagonal=True)
        else:
            @pl.when(has_next)
            def _():
                qk_stage(ia + 2, s1, 0)
                pv_stage(1, s0, ib, diagonal=True)

            @pl.when(jnp.logical_not(has_next))
            def _():
                pv_stage(1, s0, ib, diagonal=True)
        finalize(1, ib)

    qk_stage(0, 0, 0, diagonal=True)

    def two_pairs(t, carry):
        tile_pair(2 * t, 0, None)
        tile_pair(2 * t + 1, 1, 2 * t + 2 < npairs)
        return carry

    lax.fori_loop(0, npairs // 2, two_pairs, 0)


def _flash(qt, k, vt):
    B, S, _ = k.shape
    assert S % (4 * TQ) == 0
    nq = S // TQ
    pairs = MLA_HEADS // 2
    return pl.pallas_call(
        _flash_kernel,
        grid=(B, pairs),
        in_specs=[pl.BlockSpec((1, nq, 2 * MLA_QK_PAD, TQ), lambda b, p: (b, 0, p, 0)),
                  pl.BlockSpec((1, S, 2 * MLA_QK_PAD), lambda b, p: (b, 0, p)),
                  pl.BlockSpec((1, nq, 2 * MLA_VT_ROWS, TQ), lambda b, p: (b, 0, p, 0))],
        out_specs=pl.BlockSpec((1, nq, 2 * MLA_V, TQ), lambda b, p: (b, 0, p, 0)),
        out_shape=jax.ShapeDtypeStruct((B, nq, MLA_HEADS * MLA_V, TQ), BF16),
        scratch_shapes=[pltpu.VMEM((2, TQ, TQ), F32),
                        pltpu.VMEM((2, TQ, TQ), F32),
                        pltpu.VMEM((2, 2, 1, TQ), F32),
                        pltpu.VMEM((2, 2, MLA_VT_ROWS, TQ), F32),
                        pltpu.VMEM((2, 2, 1, TQ), F32)],
        compiler_params=_params("arbitrary", "arbitrary"),
        name="mla_flash",
    )(qt, k, vt)


def _pad_heads(w, heads, d, dp):
    lead = w.shape[:-1]
    w = w.reshape(lead + (heads, d))
    w = jnp.pad(w, [(0, 0)] * len(lead) + [(0, 0), (0, dp - d)])
    return w.reshape(lead + (heads * dp,))


def kernel(x, mem, positions, mem_norm, w_mem_kv, norm_mix0, w_in0, b_igate0, b_fgate0, w_conv0, w_hnorm0, w_out0, norm_ffn0, w_ff1_0, w_ff2_0, norm_mix1, w_in1, w_qnorm1, w_uq1, w_kvnorm1, w_ukv1, w_out1, norm_ffn1, w_ff1_1, w_ff2_1, final_norm):
    row = lambda g: g.reshape(1, -1).astype(F32)

    km, vmt = _mem_kv(mem, row(mem_norm), w_mem_kv[:, :MEM_WIDTH].astype(BF16),
                      w_mem_kv[:, MEM_WIDTH:].T.astype(BF16))

    nq = ML_HEADS * ML_QK_DIM
    o_v, o_o, o_g = 2 * nq, 2 * nq + MIX_WIDTH, 2 * nq + 2 * MIX_WIDTH
    o_qm = o_g + 2 * ML_HEADS
    wqk = jnp.concatenate([_pad_heads(w_in0[:, :nq], ML_HEADS, ML_QK_DIM, ML_QK_PAD),
                           _pad_heads(w_in0[:, nq:o_v], ML_HEADS, ML_QK_DIM, ML_QK_PAD)], axis=1)
    wvgt0 = jnp.concatenate([_pad_heads(w_in0[:, o_v:o_o], ML_HEADS, ML_V_DIM, ML_VT_ROWS),
                             jnp.pad(w_in0[:, o_g:o_qm], ((0, 0), (0, ML_GATE_ROWS - 2 * ML_HEADS)))], axis=1).T
    wot0 = w_in0[:, o_o:o_g].T
    wqmt0 = w_in0[:, o_qm:].T
    wconv = jnp.concatenate([_pad_heads(w_conv0[:, :nq], ML_HEADS, ML_QK_DIM, ML_QK_PAD),
                             _pad_heads(w_conv0[:, nq:], ML_HEADS, ML_QK_DIM, ML_QK_PAD)], axis=1).astype(F32)
    gate_bias = jnp.pad(jnp.concatenate([b_igate0, b_fgate0]), (0, ML_GATE_ROWS - 2 * ML_HEADS)).astype(F32)
    gate_bias = jnp.broadcast_to(gate_bias[:, None], (ML_GATE_ROWS, LANES))
    hnorm = jnp.broadcast_to(w_hnorm0.astype(F32)[:, :, None], (ML_HEADS, ML_V_DIM, LANES))

    qk, vt0, ot0, qm0, gates = _inproj0(x, row(norm_mix0), wqk.astype(BF16), wvgt0.astype(BF16),
                                        wot0.astype(BF16), wqmt0.astype(BF16))
    yt0 = _mlstm(qk, vt0, ot0, gates, wconv, gate_bias, hnorm)
    x = _block_tail(x, yt0, qm0, km, vmt, w_out0.astype(BF16), row(norm_ffn0), w_ff1_0, w_ff2_0,
                    row(final_norm), False)

    o_kr = MLA_Q_RANK + MLA_KV_RANK
    o_qm1 = o_kr + MLA_ROPE
    win1 = w_in1[:, :o_kr]
    wqmt1 = w_in1[:, o_qm1:].T
    wkrt = jnp.pad(w_in1[:, o_kr:o_qm1].T, ((MLA_NOPE, LANES - MLA_NOPE - MLA_ROPE), (0, 0)))
    wuqt = _pad_heads(w_uq1, MLA_HEADS, MLA_NOPE + MLA_ROPE, MLA_QK_PAD).T
    ukv = w_ukv1.reshape(MLA_KV_RANK, MLA_HEADS, MLA_NOPE + MLA_V)
    wuk = _pad_heads(ukv[:, :, :MLA_NOPE].reshape(MLA_KV_RANK, -1), MLA_HEADS, MLA_NOPE, MLA_QK_PAD)
    wuvt = _pad_heads(ukv[:, :, MLA_NOPE:].reshape(MLA_KV_RANK, -1), MLA_HEADS, MLA_V, MLA_VT_ROWS).T

    cos_t, sin_t = _rope_tables(positions)
    qt1, k1, vt1, qm1 = _inproj1(x, row(norm_mix1), win1.astype(BF16), wqmt1.astype(BF16), wkrt.astype(BF16),
                                 row(w_qnorm1), row(w_kvnorm1), wuqt.astype(BF16), wuk.astype(BF16),
                                 wuvt.astype(BF16), cos_t, sin_t)
    yt1 = _flash(qt1, k1, vt1)
    return _block_tail(x, yt1, qm1, km, vmt, w_out1.astype(BF16), row(norm_ffn1), w_ff1_1, w_ff2_1,
                       row(final_norm), True)
```
